```python
import jax, jax.numpy as jnp
from jax import lax
import numpy as np

D_MODEL = 1024
BATCH = 16
SEQ = 256
DEPTH = 4
DEC_BATCH = 4
DEC_SEQ = 1024
PAST_LEN = 512

GRID_W = 64
HEAD_DIM = 64
A_HEADS = 8
A_KV_HEADS = 2
A_GROUP = A_HEADS // A_KV_HEADS
Q_BLOCK = 128
ROPE_THETA = 10000.0
ROT_AXIS_DIM = HEAD_DIM // 2
B_HEADS = 8
B_HEAD_DIM = 64
B_WIDTH = B_HEADS * B_HEAD_DIM
A_Q_WIDTH = A_HEADS * HEAD_DIM
A_KV_WIDTH = A_KV_HEADS * HEAD_DIM
MIX_WIDTH = A_Q_WIDTH + B_WIDTH
IN_WIDTH = A_Q_WIDTH + 2 * A_KV_WIDTH + 5 * B_WIDTH
GLA_CHUNK = 16
CONV_DIM = D_MODEL
CONV_WIDTH = 31
MOE_GROUPS = 4
EXPERTS_PER_GROUP = 4
N_EXPERTS = MOE_GROUPS * EXPERTS_PER_GROUP
EXPERT_DIM = 256
TOP_K = 2
N_AB_LAYERS = (DEPTH + 1) // 2
N_CONV_LAYERS = DEPTH // 2
EPS = 1e-6

kernel_name = "hybrid_dit_gqa_hgrn2_conformer_hmoe_step"


def rms_norm(x, g):
    xf = x.astype(jnp.float32)
    y = xf * lax.rsqrt(jnp.mean(xf * xf, axis=-1, keepdims=True) + EPS)
    return (y * g.astype(jnp.float32)).astype(x.dtype)


def layer_norm(x, g, b):
    xf = x.astype(jnp.float32)
    mu = jnp.mean(xf, axis=-1, keepdims=True)
    var = jnp.mean(jnp.square(xf - mu), axis=-1, keepdims=True)
    y = (xf - mu) * lax.rsqrt(var + EPS) * g.astype(jnp.float32) + b.astype(jnp.float32)
    return y.astype(x.dtype)


def adaln(cond, w, b):
    m = jax.nn.silu(cond) @ w + b
    return jnp.split(m[:, None, :], 6, axis=-1)


def modulate(x, g, shift, scale):
    return rms_norm(x, g) * (1 + scale) + shift


def axial_rope_tables(n_tokens):
    rows = n_tokens // GRID_W
    row = jnp.repeat(jnp.arange(rows, dtype=jnp.float32), GRID_W)
    col = jnp.tile(jnp.arange(GRID_W, dtype=jnp.float32), rows)
    inv = ROPE_THETA ** (-jnp.arange(0, ROT_AXIS_DIM, 2, dtype=jnp.float32) / ROT_AXIS_DIM)
    ang_r = row[:, None] * inv[None, :]
    ang_c = col[:, None] * inv[None, :]
    return (jnp.cos(ang_r), jnp.sin(ang_r), jnp.cos(ang_c), jnp.sin(ang_c))


def _rotate(x, cos, sin):
    x1, x2 = jnp.split(x, 2, axis=-1)
    cos = cos[None, :, None, :].astype(x.dtype)
    sin = sin[None, :, None, :].astype(x.dtype)
    return jnp.concatenate([x1 * cos - x2 * sin, x2 * cos + x1 * sin], axis=-1)


def apply_axial_rope(x, tables):
    cr, sr, cc, sc = tables
    xr, xc = jnp.split(x, 2, axis=-1)
    return jnp.concatenate([_rotate(xr, cr, sr), _rotate(xc, cc, sc)], axis=-1)


def gqa_attend(q, k, v):
    bsz, t = q.shape[:2]
    n_blk = t // Q_BLOCK
    scale = HEAD_DIM ** -0.5
    q_blocks = jnp.moveaxis(q.reshape(bsz, n_blk, Q_BLOCK, A_KV_HEADS, A_GROUP, HEAD_DIM), 1, 0)

    def attend_block(qb):
        s = jnp.einsum("bqhgd,bkhd->bhgqk", qb, k).astype(jnp.float32) * scale
        p = jax.nn.softmax(s, axis=-1).astype(v.dtype)
        return jnp.einsum("bhgqk,bkhd->bqhgd", p, v)

    o = lax.map(attend_block, q_blocks)
    return jnp.moveaxis(o, 0, 1).reshape(bsz, t, A_Q_WIDTH)


def gla_chunked(q, k, v, log_f, s0):
    bsz, t, h, dk = q.shape
    dv = v.shape[-1]
    n = t // GLA_CHUNK
    rs = lambda a: a.reshape(bsz, n, GLA_CHUNK, h, a.shape[-1]).astype(jnp.float32)
    qc, kc, vc, gc = rs(q), rs(k), rs(v), rs(log_f)
    g_cum = jnp.cumsum(gc, axis=2)
    g_last = g_cum[:, :, -1]
    causal = jnp.tril(jnp.ones((GLA_CHUNK, GLA_CHUNK), dtype=bool))
    diff = g_cum[:, :, :, None] - g_cum[:, :, None, :]
    decay = jnp.exp(jnp.where(causal[None, None, :, :, None, None], diff, -jnp.inf))
    a = jnp.einsum("bnihd,bnijhd,bnjhd->bnhij", qc, decay, kc)
    o_intra = jnp.einsum("bnhij,bnjhe->bnihe", a, vc)
    upd = jnp.einsum("bnchd,bnche->bnhde", kc * jnp.exp(g_last[:, :, None] - g_cum), vc)
    chunk_decay = jnp.exp(g_last)

    def step(s, inp):
        dec, u = inp
        return dec[..., None] * s + u, s

    s_final, s_starts = lax.scan(step, s0.astype(jnp.float32),
                                 (jnp.moveaxis(chunk_decay, 1, 0), jnp.moveaxis(upd, 1, 0)))
    s_starts = jnp.moveaxis(s_starts, 0, 1)
    o_inter = jnp.einsum("bnchd,bnhde->bnche", qc * jnp.exp(g_cum), s_starts)
    o = (o_intra + o_inter).reshape(bsz, t, h, dv)
    return o.astype(v.dtype), s_final.astype(q.dtype)


def forget_gates(raw, lb):
    lb = lb.astype(jnp.float32).reshape(B_HEADS, B_HEAD_DIM)
    f = lb + (1.0 - lb) * jax.nn.sigmoid(raw.astype(jnp.float32))
    return 1.0 - f, jnp.log(f)


def hgrn2_bidirectional(q, v, gate, gates, s0, out_gain):
    (k_f, lf_f), (k_b, lf_b) = gates
    o_f, s_f = gla_chunked(q, k_f, v, lf_f, s0[:, 0])
    rev = lambda a: jnp.flip(a, axis=1)
    o_b, s_b = gla_chunked(rev(q), rev(k_b), rev(v), rev(lf_b), s0[:, 1])
    o = rms_norm(o_f + rev(o_b), out_gain) * jax.nn.silu(gate)
    bsz, t = q.shape[:2]
    return o.reshape(bsz, t, B_WIDTH), jnp.stack([s_f, s_b], axis=1)


def ab_project(h, w_in, q_gain, k_gain, lb):
    bsz, t, _ = h.shape
    cuts = [A_Q_WIDTH, A_Q_WIDTH + A_KV_WIDTH, A_Q_WIDTH + 2 * A_KV_WIDTH]
    cuts = cuts + [cuts[-1] + B_WIDTH * i for i in range(1, 5)]
    qa, ka, va, qb, f_fwd, f_bwd, ib, gb = jnp.split(h @ w_in, cuts, axis=-1)
    qa = rms_norm(qa.reshape(bsz, t, A_KV_HEADS, A_GROUP, HEAD_DIM), q_gain)
    ka = rms_norm(ka.reshape(bsz, t, A_KV_HEADS, HEAD_DIM), k_gain)
    va = va.reshape(bsz, t, A_KV_HEADS, HEAD_DIM)
    heads_b = lambda a: a.reshape(bsz, t, B_HEADS, B_HEAD_DIM)
    gates = (forget_gates(heads_b(f_fwd), lb[0]), forget_gates(heads_b(f_bwd), lb[1]))
    return qa, ka, va, heads_b(qb), heads_b(ib), heads_b(gb), gates


def ab_mixer_context(h, w_in, w_out, q_gain, k_gain, lb, out_gain):
    qa, ka, va, qb, ib, gb, gates = ab_project(h, w_in, q_gain, k_gain, lb)
    oa = gqa_attend(qa, ka, va)
    s0 = jnp.zeros((h.shape[0], 2, B_HEADS, B_HEAD_DIM, B_HEAD_DIM), jnp.float32)
    ob, s_end = hgrn2_bidirectional(qb, ib, gb, gates, s0, out_gain)
    out = jnp.concatenate([oa, ob], axis=-1) @ w_out
    return out, ka, va, s_end


def ab_mixer_latent(h, w_in, w_out, q_gain, k_gain, lb, out_gain, rope, ctx_k, ctx_v, ctx_state):
    bsz, t, _ = h.shape
    qa, ka, va, qb, ib, gb, gates = ab_project(h, w_in, q_gain, k_gain, lb)
    qa = apply_axial_rope(qa.reshape(bsz, t, A_HEADS, HEAD_DIM), rope)
    qa = qa.reshape(bsz, t, A_KV_HEADS, A_GROUP, HEAD_DIM)
    ka = apply_axial_rope(ka, rope)
    k_all = jnp.concatenate([ctx_k.astype(ka.dtype), ka], axis=1)
    v_all = jnp.concatenate([ctx_v.astype(va.dtype), va], axis=1)
    oa = gqa_attend(qa, k_all, v_all)
    ob, _ = hgrn2_bidirectional(qb, ib, gb, gates, ctx_state, out_gain)
    return jnp.concatenate([oa, ob], axis=-1) @ w_out


def conformer_conv(h, pw1, dw, dw_b, ln_g, ln_b, pw2):
    a, b = jnp.split(h @ pw1, 2, axis=-1)
    u = a * jax.nn.sigmoid(b)
    u = lax.conv_general_dilated(u, dw[:, None, :].astype(u.dtype), window_strides=(1,),
                                 padding=[(CONV_WIDTH // 2, CONV_WIDTH // 2)],
                                 dimension_numbers=("NWC", "WIO", "NWC"),
                                 feature_group_count=CONV_DIM) + dw_b
    u = jax.nn.silu(layer_norm(u, ln_g, ln_b))
    return u @ pw2


def hier_moe(h, w_group, b_group, w_expert, b_expert, w1, w3, w2):
    bsz, t, d = h.shape
    x = h.reshape(bsz * t, d)
    n_tok = x.shape[0]
    p_group = jax.nn.softmax((x @ w_group + b_group).astype(jnp.float32), axis=-1)
    g_val, g_idx = lax.top_k(p_group, 1)
    logits_e = (x @ w_expert + b_expert).astype(jnp.float32).reshape(n_tok, MOE_GROUPS, EXPERTS_PER_GROUP)
    logits_sel = jnp.take_along_axis(logits_e, g_idx[:, :, None], axis=1)[:, 0]
    p_exp = jax.nn.softmax(logits_sel, axis=-1)
    e_val, e_idx = lax.top_k(p_exp, TOP_K)
    e_val = e_val / jnp.sum(e_val, axis=-1, keepdims=True)
    expert_id = g_idx * EXPERTS_PER_GROUP + e_idx
    weight = g_val * e_val
    combine = jnp.sum(jax.nn.one_hot(expert_id, N_EXPERTS, dtype=jnp.float32) * weight[..., None], axis=1)
    hid = jax.nn.silu(jnp.einsum("nd,edf->nef", x, w1)) * jnp.einsum("nd,edf->nef", x, w3)
    y = jnp.einsum("nef,efd->nd", hid * combine[..., None].astype(hid.dtype), w2)
    return y.reshape(bsz, t, d)


def setup_inputs(seed: int = 0) -> dict:
    key = jax.random.key(seed)
    ks = iter(jax.random.split(key, 48))
    f32 = jnp.float32
    nrm = lambda shape, scale: scale * jax.random.normal(next(ks), shape, f32)
    gain = lambda shape: 1.0 + 0.1 * jax.random.normal(next(ks), shape, f32)
    return {
        "x_prompt": nrm((BATCH, SEQ, D_MODEL), 1.0),
        "x_sample": nrm((DEC_BATCH, DEC_SEQ, D_MODEL), 1.0),
        "cache_k": nrm((DEC_BATCH, N_AB_LAYERS, PAST_LEN, A_KV_HEADS, HEAD_DIM), 1.0),
        "cache_v": nrm((DEC_BATCH, N_AB_LAYERS, PAST_LEN, A_KV_HEADS, HEAD_DIM), 1.0),
        "state_hgrn": nrm((DEC_BATCH, N_AB_LAYERS, 2, B_HEADS, B_HEAD_DIM, B_HEAD_DIM), 0.5),
        "c": nrm((DEC_BATCH, D_MODEL), 1.0),
        "c_ctx": nrm((D_MODEL,), 1.0),
        "ada_w": nrm((DEPTH, D_MODEL, 6 * D_MODEL), 0.5 * D_MODEL ** -0.5),
        "ada_b": nrm((DEPTH, 6 * D_MODEL), 0.02),
        "norm1_g": gain((DEPTH, D_MODEL)),
        "norm2_g": gain((DEPTH, D_MODEL)),
        "mix_w_in": nrm((N_AB_LAYERS, D_MODEL, IN_WIDTH), D_MODEL ** -0.5),
        "mix_w_out": nrm((N_AB_LAYERS, MIX_WIDTH, D_MODEL), MIX_WIDTH ** -0.5),
        "attn_q_gain": gain((N_AB_LAYERS, HEAD_DIM)),
        "attn_k_gain": gain((N_AB_LAYERS, HEAD_DIM)),
        "hgrn_lower_bound": nrm((N_AB_LAYERS, 2, B_WIDTH), 0.5),
        "hgrn_out_gain": gain((N_AB_LAYERS, B_HEAD_DIM)),
        "conv_pw1": nrm((N_CONV_LAYERS, D_MODEL, 2 * CONV_DIM), D_MODEL ** -0.5),
        "conv_dw": nrm((N_CONV_LAYERS, CONV_WIDTH, CONV_DIM), CONV_WIDTH ** -0.5),
        "conv_dw_b": nrm((N_CONV_LAYERS, CONV_DIM), 0.02),
        "conv_ln_g": gain((N_CONV_LAYERS, CONV_DIM)),
        "conv_ln_b": nrm((N_CONV_LAYERS, CONV_DIM), 0.02),
        "conv_pw2": nrm((N_CONV_LAYERS, CONV_DIM, D_MODEL), CONV_DIM ** -0.5),
        "moe_w_group": nrm((DEPTH, D_MODEL, MOE_GROUPS), D_MODEL ** -0.5),
        "moe_b_group": nrm((DEPTH, MOE_GROUPS), 0.01),
        "moe_w_expert": nrm((DEPTH, D_MODEL, N_EXPERTS), D_MODEL ** -0.5),
        "moe_b_expert": nrm((DEPTH, N_EXPERTS), 0.01),
        "moe_w1": nrm((DEPTH, N_EXPERTS, D_MODEL, EXPERT_DIM), D_MODEL ** -0.5),
        "moe_w3": nrm((DEPTH, N_EXPERTS, D_MODEL, EXPERT_DIM), D_MODEL ** -0.5),
        "moe_w2": nrm((DEPTH, N_EXPERTS, EXPERT_DIM, D_MODEL), EXPERT_DIM ** -0.5),
        "final_g": gain((D_MODEL,)),
    }


def reference(x_prompt, x_sample, cache_k, cache_v, state_hgrn, c, c_ctx, ada_w, ada_b, norm1_g, norm2_g,
              mix_w_in, mix_w_out, attn_q_gain, attn_k_gain, hgrn_lower_bound, hgrn_out_gain,
              conv_pw1, conv_dw, conv_dw_b, conv_ln_g, conv_ln_b, conv_pw2,
              moe_w_group, moe_b_group, moe_w_expert, moe_b_expert, moe_w1, moe_w3, moe_w2, final_g):
    rope = axial_rope_tables(x_sample.shape[1])
    lb_soft = jax.nn.softmax(hgrn_lower_bound.astype(jnp.float32), axis=0)
    lower_bounds = jnp.cumsum(lb_soft, axis=0) - lb_soft[0]
    xp, xs = x_prompt, x_sample
    new_k, new_v, new_s = [], [], []
    for l in range(DEPTH):
        p_sh1, p_sc1, p_g1, p_sh2, p_sc2, p_g2 = adaln(c_ctx[None, :], ada_w[l], ada_b[l])
        s_sh1, s_sc1, s_g1, s_sh2, s_sc2, s_g2 = adaln(c, ada_w[l], ada_b[l])
        hp = modulate(xp, norm1_g[l], p_sh1, p_sc1)
        hs = modulate(xs, norm1_g[l], s_sh1, s_sc1)
        if l % 2 == 0:
            e = l // 2
            ab_w = (mix_w_in[e], mix_w_out[e], attn_q_gain[e], attn_k_gain[e], lower_bounds[e], hgrn_out_gain[e])
            mp, k_ctx, v_ctx, s_ctx = ab_mixer_context(hp, *ab_w)
            ms = ab_mixer_latent(hs, *ab_w, rope, cache_k[:, e], cache_v[:, e], state_hgrn[:, e])
            new_k.append(k_ctx)
            new_v.append(v_ctx)
            new_s.append(s_ctx)
        else:
            o = l // 2
            cw = (conv_pw1[o], conv_dw[o], conv_dw_b[o], conv_ln_g[o], conv_ln_b[o], conv_pw2[o])
            mp = conformer_conv(hp, *cw)
            ms = conformer_conv(hs, *cw)
        xp = xp + p_g1 * mp
        xs = xs + s_g1 * ms
        mw = (moe_w_group[l], moe_b_group[l], moe_w_expert[l], moe_b_expert[l], moe_w1[l], moe_w3[l], moe_w2[l])
        xp = xp + p_g2 * hier_moe(modulate(xp, norm2_g[l], p_sh2, p_sc2), *mw)
        xs = xs + s_g2 * hier_moe(modulate(xs, norm2_g[l], s_sh2, s_sc2), *mw)
    y_prompt = rms_norm(xp, final_g)
    y_sample = rms_norm(xs, final_g)
    new_cache_k = jnp.stack(new_k, axis=1)
    new_cache_v = jnp.stack(new_v, axis=1)
    new_state_hgrn = jnp.stack(new_s, axis=1)
    return (y_prompt, y_sample, new_cache_k, new_cache_v, new_state_hgrn)
```

```python
import functools

import jax
import jax.numpy as jnp
from jax import lax
from jax.experimental import pallas as pl
from jax.experimental.pallas import tpu as pltpu

F32 = jnp.float32
BF16 = jnp.bfloat16

HEAD_DIM = 64
A_HEADS = 8
A_KV_HEADS = 2
A_GROUP = A_HEADS // A_KV_HEADS
B_HEADS = 8
GRID_W = 64
ROPE_THETA = 10000.0
GLA_CHUNK = 16
CONV_WIDTH = 31
MOE_GROUPS = 4
EXPERTS_PER_GROUP = 4
EPS = 1e-6

LANES = 128
VMEM_LIMIT = 56 * 1024 * 1024
TOKEN_TILE = 512
MOE_TILE = 256
ATTN_Q_TILE = 256
ROUTE_W = LANES
ROUTE_GID_LANE = 8
CONV_ROWS = 64

_NT = (((1,), (1,)), ((), ()))


def _params(n_axes):
    return pltpu.CompilerParams(dimension_semantics=("arbitrary",) * n_axes, vmem_limit_bytes=VMEM_LIMIT)


def _modulate(x, g, sh, sc):
    ms = jnp.mean(x * x, axis=-1, keepdims=True)
    return (x * lax.rsqrt(ms + EPS) * g) * (1.0 + sc) + sh


def _silu(x):
    return x * jax.nn.sigmoid(x)


def _split3(x):
    hi = x.astype(BF16)
    r1 = x - hi.astype(F32)
    mid = r1.astype(BF16)
    lo = (r1 - mid.astype(F32)).astype(BF16)
    return hi, mid, lo


def _ada_body(c_ref, w_ref, b_ref, o_ref):
    s = _silu(c_ref[...])
    o_ref[...] = jnp.dot(s.astype(BF16), w_ref[...].astype(BF16), preferred_element_type=F32) + b_ref[...]


def _ada(cond8, ada_w, ada_b):
    depth, d, d6 = ada_w.shape
    tn = d6 // 4
    return pl.pallas_call(
        _ada_body,
        grid=(depth, d6 // tn),
        in_specs=[
            pl.BlockSpec((8, d), lambda l, j: (0, 0)),
            pl.BlockSpec((None, d, tn), lambda l, j: (l, 0, j)),
            pl.BlockSpec((None, 1, tn), lambda l, j: (l, 0, j)),
        ],
        out_specs=pl.BlockSpec((None, 8, tn), lambda l, j: (l, 0, j)),
        out_shape=jax.ShapeDtypeStruct((depth, 8, d6), F32),
        compiler_params=_params(2),
        name="adaln",
    )(cond8, ada_w, ada_b.reshape(depth, 1, d6))


class _Geom:
    def __init__(self, n_ctx, n_smp, smp_len, d):
        self.n_ctx, self.n_smp, self.smp_len, self.d = n_ctx, n_smp, smp_len, d
        self.n = n_ctx + n_smp

    def mod_spec(self, layer, j, tm):
        ctx_tiles = self.n_ctx // tm
        per_smp = self.smp_len // tm

        def idx(i):
            row = jnp.where(i < ctx_tiles, 0, 1 + (i - ctx_tiles) // per_smp)
            return (layer, row, j, 0, 0)

        return pl.BlockSpec((None, None, None, 1, self.d), idx)


def _row_spec(d):
    return pl.BlockSpec((1, d), lambda i: (0, 0))


def _full_spec(shape):
    nd = len(shape)
    return pl.BlockSpec(shape, lambda i: (0,) * nd)


def _inproj_body(has_y, n_t, *refs):
    if has_y:
        xa, y, g2, gn, sh, sc, wn, wt, qkv_o, hgt_o = refs
        x = xa[...] + g2[...] * y[...]
    else:
        xa, gn, sh, sc, wn, wt, qkv_o, hgt_o = refs
        x = xa[...]
    h = _modulate(x, gn[...], sh[...], sc[...]).astype(BF16)
    qkv_o[...] = jnp.dot(h, wn[...], preferred_element_type=F32)
    step = 512
    for r in range(0, n_t, step):
        hgt_o[r:r + step, :] = lax.dot_general(wt[r:r + step, :], h, _NT, preferred_element_type=F32)


def _inproj(geom, mod, layer, xa, y, gn, wn, wt):
    tm = TOKEN_TILE
    d = geom.d
    n_nat, n_t = wn.shape[1], wt.shape[0]
    has_y = y is not None
    tok = pl.BlockSpec((tm, d), lambda i: (i, 0))
    in_specs = [tok]
    args = [xa]
    if has_y:
        in_specs += [tok, geom.mod_spec(layer - 1, 5, tm)]
        args += [y, mod]
    in_specs += [_row_spec(d), geom.mod_spec(layer, 0, tm), geom.mod_spec(layer, 1, tm),
                 _full_spec(wn.shape), _full_spec(wt.shape)]
    args += [gn, mod, mod, wn, wt]
    return pl.pallas_call(
        functools.partial(_inproj_body, has_y, n_t),
        grid=(geom.n // tm,),
        in_specs=in_specs,
        out_specs=[pl.BlockSpec((tm, n_nat), lambda i: (i, 0)), pl.BlockSpec((n_t, tm), lambda i: (0, i))],
        out_shape=[jax.ShapeDtypeStruct((geom.n, n_nat), F32), jax.ShapeDtypeStruct((n_t, geom.n), F32)],
        compiler_params=_params(1),
        name="ab_inproj",
    )(*args)


def _head_norm(x, gain_row, n_heads):
    outs = []
    for h in range(n_heads):
        xh = x[:, h * HEAD_DIM:(h + 1) * HEAD_DIM]
        ms = jnp.mean(xh * xh, axis=-1, keepdims=True)
        outs.append(xh * lax.rsqrt(ms + EPS))
    return jnp.concatenate(outs, axis=1) * gain_row


def _rope(x, cos, sin_signed):
    w = x.shape[1]
    lane = lax.broadcasted_iota(jnp.int32, (1, w), 1)
    first_half = (lane & 16) == 0
    partner = jnp.where(first_half, pltpu.roll(x, w - 16, 1), pltpu.roll(x, 16, 1))
    return x * cos + partner * sin_signed


def _attn_body(rope, tq, *refs):
    if rope:
        q_r, k_r, v_r, ck_r, cv_r, qg_r, kg_r, cq_r, sq_r, ckk_r, skk_r, o_r = refs
    else:
        q_r, k_r, v_r, qg_r, kg_r, o_r, kh_r = refs
    qn = _head_norm(q_r[...], qg_r[...], A_HEADS)
    kn = _head_norm(k_r[...], kg_r[...], A_KV_HEADS)
    if rope:
        qn = _rope(qn, cq_r[...], sq_r[...])
        kn = _rope(kn, ckk_r[...], skk_r[...])
        keys = jnp.concatenate([ck_r[...], kn], axis=0)
        vals = jnp.concatenate([cv_r[...], v_r[...]], axis=0)
    else:
        kh_r[...] = kn
        keys, vals = kn, v_r[...]
    scale = HEAD_DIM ** -0.5
    outs = [None] * A_HEADS
    for j in range(A_KV_HEADS):
        kj = keys[:, j * HEAD_DIM:(j + 1) * HEAD_DIM].astype(BF16)
        vj = vals[:, j * HEAD_DIM:(j + 1) * HEAD_DIM].astype(BF16)
        heads = [A_GROUP * j + g for g in range(A_GROUP)]
        q4 = jnp.concatenate([qn[:, h * HEAD_DIM:(h + 1) * HEAD_DIM] for h in heads], axis=0).astype(BF16)
        s = lax.dot_general(q4, kj, _NT, preferred_element_type=F32) * scale
        m = jnp.max(s, axis=-1, keepdims=True)
        p = jnp.exp(s - m)
        l = jnp.sum(p, axis=-1, keepdims=True)
        o = jnp.dot(p.astype(BF16), vj, preferred_element_type=F32) / l
        for g, h in enumerate(heads):
            outs[h] = o[g * tq:(g + 1) * tq, :]
    o_r[...] = jnp.concatenate(outs, axis=1).astype(BF16)


def _attention(qkv, row0, n_batch, t, q_gain, k_gain, cache=None, rope_tabs=None):
    tq = ATTN_Q_TILE
    nq = t // tq
    qw, kw = A_HEADS * HEAD_DIM, A_KV_HEADS * HEAD_DIM
    rope = cache is not None
    q_spec = pl.BlockSpec((tq, qw), lambda b, qi: (row0 // tq + b * nq + qi, 0))
    k_spec = pl.BlockSpec((t, kw), lambda b, qi: (row0 // t + b, qw // kw))
    v_spec = pl.BlockSpec((t, kw), lambda b, qi: (row0 // t + b, qw // kw + 1))
    gq = pl.BlockSpec((1, qw), lambda b, qi: (0, 0))
    gk = pl.BlockSpec((1, kw), lambda b, qi: (0, 0))
    out_o = pl.BlockSpec((tq, qw), lambda b, qi: (b * nq + qi, 0))
    o_shape = jax.ShapeDtypeStruct((n_batch * t, qw), BF16)
    if rope:
        ck, cv = cache
        past = ck.shape[1]
        c_spec = pl.BlockSpec((None, past, kw), lambda b, qi: (b, 0, 0))
        cq, sq, ckk, skk = rope_tabs
        in_specs = [q_spec, k_spec, v_spec, c_spec, c_spec, gq, gk,
                    pl.BlockSpec((tq, qw), lambda b, qi: (qi, 0)), pl.BlockSpec((tq, qw), lambda b, qi: (qi, 0)),
                    pl.BlockSpec((t, kw), lambda b, qi: (0, 0)), pl.BlockSpec((t, kw), lambda b, qi: (0, 0))]
        args = [qkv, qkv, qkv, ck, cv, q_gain, k_gain, cq, sq, ckk, skk]
        out_specs, out_shape = out_o, o_shape
    else:
        in_specs = [q_spec, k_spec, v_spec, gq, gk]
        args = [qkv, qkv, qkv, q_gain, k_gain]
        out_specs = [out_o, pl.BlockSpec((t, kw), lambda b, qi: (b, 0))]
        out_shape = [o_shape, jax.ShapeDtypeStruct((n_batch * t, kw), F32)]
    return pl.pallas_call(
        functools.partial(_attn_body, rope, tq),
        grid=(n_batch, nq),
        in_specs=in_specs,
        out_specs=out_specs,
        out_shape=out_shape,
        compiler_params=_params(2),
        name="gqa_latent" if rope else "gqa_context",
    )(*args)


def _hgrn_body(zero_init, t, *refs):
    if zero_init:
        q_r, ff_r, fb_r, v_r, g_r, lb_r, gain_r, ob_o, send_o, o_scr, on_scr = refs
        s0_r = None
    else:
        q_r, ff_r, fb_r, v_r, g_r, lb_r, gain_r, s0_r, ob_o, send_o, o_scr, on_scr = refs
    nt = t // LANES
    hd = HEAD_DIM
    cpt = LANES // GLA_CHUNK
    lane = lax.broadcasted_iota(jnp.int32, (1, LANES), 1)
    pos = lane & (GLA_CHUNK - 1)
    ri = lax.broadcasted_iota(jnp.int32, (LANES, LANES), 0)
    ci = lax.broadcasted_iota(jnp.int32, (LANES, LANES), 1)
    same = (ri >> 4) == (ci >> 4)

    def ind(cond):
        return jnp.where(cond, 1.0, 0.0).astype(BF16)

    m_fwd = jnp.concatenate([ind(same & (ri <= ci)), ind(same & (ri > ci))], axis=1)
    m_bwd = jnp.concatenate([ind(same & (ri >= ci)), ind(same & (ri < ci))], axis=1)
    sel_r = lax.broadcasted_iota(jnp.int32, (cpt, LANES), 0)
    sel_c = lax.broadcasted_iota(jnp.int32, (cpt, LANES), 1)
    sel = ind((sel_c >> 4) == sel_r)
    bd_r = lax.broadcasted_iota(jnp.int32, (cpt * hd, LANES), 0)
    bd_c = lax.broadcasted_iota(jnp.int32, (cpt * hd, LANES), 1)
    bdmask = (bd_r >> 6) == (bd_c >> 4)
    gain = gain_r[...]

    def colsum(a):
        return jnp.sum(a, axis=0, keepdims=True)

    def run_dir(bwd, rows, h):
        raw_r = fb_r if bwd else ff_r
        m_cs = m_bwd if bwd else m_fwd
        pm = (GLA_CHUNK - 1 - pos) if bwd else pos
        d_idx = 1 if bwd else 0
        lbv = lb_r[d_idx, rows, :]

        def tile_step(ti, s):
            tt = (nt - 1 - ti) if bwd else ti
            ls = pl.ds(pl.multiple_of(tt * LANES, LANES), LANES)
            f = lbv + (1.0 - lbv) * jax.nn.sigmoid(raw_r[rows, ls])
            q = q_r[rows, ls]
            v = v_r[rows, ls]
            g_prev = colsum(q)
            o = jnp.zeros((hd, LANES), F32)
            dprod = f
            for dl in range(GLA_CHUNK):
                if dl > 0:
                    sh = (LANES - dl) if bwd else dl
                    dprod = dprod * pltpu.roll(f, sh, 1)
                    vr = pltpu.roll(v, sh, 1)
                else:
                    vr = v
                g_cur = colsum(q * dprod)
                a = jnp.where(pm >= dl, g_prev - g_cur, 0.0)
                o = o + a * vr
                g_prev = g_cur
            hi, mid, lo = _split3(jnp.log(f))
            cs = (jnp.dot(hi, m_cs, preferred_element_type=F32) + jnp.dot(mid, m_cs, preferred_element_type=F32)
                  + jnp.dot(lo, m_cs, preferred_element_type=F32))
            qt = q * jnp.exp(cs[:, :LANES])
            kt = (1.0 - f) * jnp.exp(cs[:, LANES:])
            dsum = (lax.dot_general(sel, hi, _NT, preferred_element_type=F32)
                    + lax.dot_general(sel, mid, _NT, preferred_element_type=F32)
                    + lax.dot_general(sel, lo, _NT, preferred_element_type=F32))
            dec = jnp.exp(dsum)
            kbd = jnp.where(bdmask, jnp.concatenate([kt] * cpt, axis=0), 0.0).astype(BF16)
            qbd = jnp.where(bdmask, jnp.concatenate([qt] * cpt, axis=0), 0.0).astype(BF16)
            u = lax.dot_general(v.astype(BF16), kbd, _NT, preferred_element_type=F32)
            pieces = [None] * cpt
            for c in (range(cpt - 1, -1, -1) if bwd else range(cpt)):
                pieces[c] = s
                s = dec[c:c + 1, :] * s + u[:, c * hd:(c + 1) * hd]
            sst = jnp.concatenate(pieces, axis=1).astype(BF16)
            o = o + jnp.dot(sst, qbd, preferred_element_type=F32)
            if bwd:
                o = o + o_scr[rows, ls]
                ms = jnp.mean(o * o, axis=0, keepdims=True)
                gate = g_r[rows, ls]
                on_scr[rows, ls] = o * lax.rsqrt(ms + EPS) * gain * _silu(gate)
            else:
                o_scr[rows, ls] = o
            return s

        if zero_init:
            s0 = jnp.zeros((hd, hd), F32)
        else:
            s0 = s0_r[d_idx, h]
        s_end = lax.fori_loop(0, nt, tile_step, s0)
        send_o[d_idx, h] = s_end

    def head_step(h, carry):
        rows = pl.ds(pl.multiple_of(h * hd, hd), hd)
        run_dir(False, rows, h)
        run_dir(True, rows, h)
        return carry

    lax.fori_loop(0, B_HEADS, head_step, 0)
    ob_o[...] = on_scr[...].T.astype(BF16)


def _hgrn(hgt, col0, n_batch, t, lb_b, gain_b, s0t):
    bw = B_HEADS * HEAD_DIM
    zero_init = s0t is None
    specs = [pl.BlockSpec((bw, t), functools.partial(lambda r, b: (r, col0 // t + b), r)) for r in range(5)]
    in_specs = specs + [_full_spec(lb_b.shape), _full_spec(gain_b.shape)]
    args = [hgt] * 5 + [lb_b, gain_b]
    st_spec = pl.BlockSpec((None, 2, B_HEADS, HEAD_DIM, HEAD_DIM), lambda b: (b, 0, 0, 0, 0))
    if not zero_init:
        in_specs.append(st_spec)
        args.append(s0t)
    return pl.pallas_call(
        functools.partial(_hgrn_body, zero_init, t),
        grid=(n_batch,),
        in_specs=in_specs,
        out_specs=[pl.BlockSpec((t, bw), lambda b: (b, 0)), st_spec],
        out_shape=[jax.ShapeDtypeStruct((n_batch * t, bw), BF16),
                   jax.ShapeDtypeStruct((n_batch, 2, B_HEADS, HEAD_DIM, HEAD_DIM), F32)],
        scratch_shapes=[pltpu.VMEM((bw, t), F32), pltpu.VMEM((bw, t), F32)],
        compiler_params=_params(1),
        name="hgrn2_context" if zero_init else "hgrn2_latent",
    )(*args)


def _route(logits):
    lane_i = lax.broadcasted_iota(jnp.int32, logits.shape, 1)
    lane = lane_i.astype(F32)
    neg = -jnp.inf
    big = 1e9
    gl = jnp.where(lane_i < MOE_GROUPS, logits, neg)
    gmax = jnp.max(gl, axis=-1, keepdims=True)
    g_val = 1.0 / jnp.sum(jnp.exp(gl - gmax), axis=-1, keepdims=True)
    g_idx = jnp.min(jnp.where(gl == gmax, lane, big), axis=-1, keepdims=True)
    n_e = MOE_GROUPS * EXPERTS_PER_GROUP
    lane_group = ((lane_i - MOE_GROUPS) >> 2).astype(F32)
    in_group = (lane_i >= MOE_GROUPS) & (lane_i < MOE_GROUPS + n_e) & (lane_group == g_idx)
    el = jnp.where(in_group, logits, neg)
    m1 = jnp.max(el, axis=-1, keepdims=True)
    i1 = jnp.min(jnp.where(el == m1, lane, big), axis=-1, keepdims=True)
    el2 = jnp.where(lane == i1, neg, el)
    m2 = jnp.max(el2, axis=-1, keepdims=True)
    i2 = jnp.min(jnp.where(el2 == m2, lane, big), axis=-1, keepdims=True)
    tt = jnp.exp(m2 - m1)
    w1 = g_val / (1.0 + tt)
    w2 = g_val * tt / (1.0 + tt)
    base = MOE_GROUPS + EXPERTS_PER_GROUP * g_idx
    route = jnp.where(lane == i1 - base, w1, 0.0) + jnp.where(lane == i2 - base, w2, 0.0)
    return jnp.where(lane_i == ROUTE_GID_LANE, g_idx, route)


def _proj_res_body(n_lhs, has_y, d, *refs):
    lhs = refs[:n_lhs]
    ws = refs[n_lhs:2 * n_lhs]
    rest = refs[2 * n_lhs:]
    if has_y:
        xa, y, g2, g1, gn2, sh2, sc2, wrh, wrl, br, xnew_o, xr_o, route_o = rest
        x = xa[...] + g2[...] * y[...]
    else:
        xa, g1, gn2, sh2, sc2, wrh, wrl, br, xnew_o, xr_o, route_o = rest
        x = xa[...]
    acc = jnp.dot(lhs[0][...], ws[0][...], preferred_element_type=F32)
    for a, w in zip(lhs[1:], ws[1:]):
        acc = acc + jnp.dot(a[...], w[...], preferred_element_type=F32)
    xn = x + g1[...] * acc
    xnew_o[...] = xn
    hm = _modulate(xn, gn2[...], sh2[...], sc2[...])
    hi = hm.astype(BF16)
    lo = (hm - hi.astype(F32)).astype(BF16)
    logits = (jnp.dot(hi, wrh[...], preferred_element_type=F32) + jnp.dot(lo, wrh[...], preferred_element_type=F32)
              + jnp.dot(hi, wrl[...], preferred_element_type=F32)) + br[...]
    route = _route(logits)
    xr_o[:, :d] = hm
    xr_o[:, d:] = route
    route_o[...] = route


def _proj_res(geom, mod, layer, lhs, ws, xa, y, gn2, wrh, wrl, br, name):
    tm = TOKEN_TILE
    d = geom.d
    has_y = y is not None
    tok = pl.BlockSpec((tm, d), lambda i: (i, 0))
    in_specs = [pl.BlockSpec((tm, a.shape[1]), lambda i: (i, 0)) for a in lhs]
    in_specs += [_full_spec(w.shape) for w in ws]
    args = list(lhs) + list(ws)
    in_specs.append(tok)
    args.append(xa)
    if has_y:
        in_specs += [tok, geom.mod_spec(layer - 1, 5, tm)]
        args += [y, mod]
    in_specs += [geom.mod_spec(layer, 2, tm), _row_spec(d), geom.mod_spec(layer, 3, tm), geom.mod_spec(layer, 4, tm),
                 _full_spec(wrh.shape), _full_spec(wrl.shape), _full_spec(br.shape)]
    args += [mod, gn2, mod, mod, wrh, wrl, br]
    return pl.pallas_call(
        functools.partial(_proj_res_body, len(lhs), has_y, d),
        grid=(geom.n // tm,),
        in_specs=in_specs,
        out_specs=[tok, pl.BlockSpec((tm, d + ROUTE_W), lambda i: (i, 0)), pl.BlockSpec((tm, ROUTE_W), lambda i: (i, 0))],
        out_shape=[jax.ShapeDtypeStruct((geom.n, d), F32), jax.ShapeDtypeStruct((geom.n, d + ROUTE_W), F32),
                   jax.ShapeDtypeStruct((geom.n, ROUTE_W), F32)],
        compiler_params=_params(1),
        name=name,
    )(*args)


def _dispatch(route, n, tile):
    n_tiles = n // tile + MOE_GROUPS
    n_slots = n_tiles * tile
    gid = route[:, ROUTE_GID_LANE].astype(jnp.int32)
    onehot = (gid[:, None] == jnp.arange(MOE_GROUPS, dtype=jnp.int32)[None, :]).astype(jnp.int32)
    csum = jnp.cumsum(onehot, axis=0)
    counts = csum[-1]
    rank = jnp.sum(csum * onehot, axis=1) - 1
    tiles_g = (counts + tile - 1) // tile
    tile_end = jnp.cumsum(tiles_g)
    tile_start = tile_end - tiles_g
    dest = (tile_start * tile)[gid] + rank
    tok = jnp.arange(n, dtype=jnp.int32)
    src = jnp.zeros((n_slots,), jnp.int32).at[dest].set(tok)
    t_idx = jnp.arange(n_tiles, dtype=jnp.int32)
    tile_gid = jnp.minimum(jnp.sum((t_idx[:, None] >= tile_end[None, :]).astype(jnp.int32), axis=1), MOE_GROUPS - 1)
    n_used = tile_end[-1:].astype(jnp.int32)
    valid = jnp.clip(counts[tile_gid] - (t_idx - tile_start[tile_gid]) * tile, 0, tile)
    tile_rows = jnp.where(t_idx < n_used[0], valid, 0).astype(jnp.int32)
    return tile_gid, src, tile_rows, n_used, n_slots


def _moe_body(tile, d, tg_ref, src_ref, rows_ref, nu_ref, xr_hbm, w1_ref, w3_ref, w2_ref, y_hbm, xbuf, ybuf, isem, osem):
    t = pl.program_id(0)
    n_used = nu_ref[0]
    slot = t % 2

    def row_in(tile_idx, s, i):
        row = src_ref[tile_idx * tile + i]
        return pltpu.make_async_copy(xr_hbm.at[pl.ds(row, 1), :], xbuf.at[s, pl.ds(i, 1), :], isem.at[s])

    def row_out(tile_idx, s, i):
        row = src_ref[tile_idx * tile + i]
        return pltpu.make_async_copy(ybuf.at[s, pl.ds(i, 1), :], y_hbm.at[pl.ds(row, 1), :], osem.at[s])

    def start_in(tile_idx, s):
        def body(i, c):
            row_in(tile_idx, s, i).start()
            return c
        lax.fori_loop(0, tile, body, 0, unroll=8)

    def wait_in(s):
        def body(i, c):
            row_in(0, s, i).wait()
            return c
        lax.fori_loop(0, tile, body, 0, unroll=8)

    def start_out(tile_idx, s):
        def body(i, c):
            row_out(tile_idx, s, i).start()
            return c
        lax.fori_loop(0, rows_ref[tile_idx], body, 0)

    def wait_out(tile_idx, s):
        def body(i, c):
            row_out(tile_idx, s, i).wait()
            return c
        lax.fori_loop(0, rows_ref[tile_idx], body, 0)

    @pl.when(t == 0)
    def _():
        start_in(0, 0)

    @pl.when(t + 1 < n_used)
    def _():
        start_in(t + 1, 1 - slot)

    @pl.when(t < n_used)
    def _():
        wait_in(slot)
        xt = xbuf[slot]
        xb = xt[:, :d].astype(BF16)
        cw = xt[:, d:]
        acc = jnp.zeros((tile, d), F32)
        for e in range(EXPERTS_PER_GROUP):
            a = jnp.dot(xb, w1_ref[e], preferred_element_type=F32)
            b = jnp.dot(xb, w3_ref[e], preferred_element_type=F32)
            hid = (_silu(a) * b * cw[:, e:e + 1]).astype(BF16)
            acc = acc + jnp.dot(hid, w2_ref[e], preferred_element_type=F32)

        @pl.when(t >= 2)
        def _():
            wait_out(t - 2, slot)

        ybuf[slot] = acc
        start_out(t, slot)

    @pl.when(t == pl.num_programs(0) - 1)
    def _():
        @pl.when(n_used >= 2)
        def _():
            wait_out(n_used - 2, (n_used - 2) % 2)

        wait_out(n_used - 1, (n_used - 1) % 2)


def _moe(xr, route, w1, w3, w2, n, d):
    tile = MOE_TILE
    tile_gid, src, tile_rows, n_used, n_slots = _dispatch(route, n, tile)
    n_tiles = n_slots // tile
    wspec13 = pl.BlockSpec((None,) + w1.shape[1:], lambda t, tg, s, ds_, nu: (tg[t], 0, 0, 0))
    wspec2 = pl.BlockSpec((None,) + w2.shape[1:], lambda t, tg, s, ds_, nu: (tg[t], 0, 0, 0))
    grid_spec = pltpu.PrefetchScalarGridSpec(
        num_scalar_prefetch=4,
        grid=(n_tiles,),
        in_specs=[pl.BlockSpec(memory_space=pl.ANY), wspec13, wspec13, wspec2],
        out_specs=pl.BlockSpec(memory_space=pl.ANY),
        scratch_shapes=[pltpu.VMEM((2, tile, d + ROUTE_W), F32), pltpu.VMEM((2, tile, d), F32),
                        pltpu.SemaphoreType.DMA((2,)), pltpu.SemaphoreType.DMA((2,))],
    )
    return pl.pallas_call(
        functools.partial(_moe_body, tile, d),
        grid_spec=grid_spec,
        out_shape=jax.ShapeDtypeStruct((n, d), F32),
        compiler_params=_params(1),
        name="moe_ffn",
    )(tile_gid, src, tile_rows, n_used, xr, w1, w3, w2)


def _glu_body(xa, y, g2, gn, sh, sc, w, u_o):
    x = xa[...] + g2[...] * y[...]
    h = _modulate(x, gn[...], sh[...], sc[...]).astype(BF16)
    ab = jnp.dot(h, w[...], preferred_element_type=F32)
    c = ab.shape[1] // 2
    u_o[...] = ab[:, :c] * jax.nn.sigmoid(ab[:, c:])


def _glu(geom, mod, layer, xa, y, gn, w):
    tm = TOKEN_TILE
    d = geom.d
    tok = pl.BlockSpec((tm, d), lambda i: (i, 0))
    c = w.shape[1] // 2
    return pl.pallas_call(
        _glu_body,
        grid=(geom.n // tm,),
        in_specs=[tok, tok, geom.mod_spec(layer - 1, 5, tm), _row_spec(d), geom.mod_spec(layer, 0, tm),
                  geom.mod_spec(layer, 1, tm), _full_spec(w.shape)],
        out_specs=pl.BlockSpec((tm, c), lambda i: (i, 0)),
        out_shape=jax.ShapeDtypeStruct((geom.n, c), F32),
        compiler_params=_params(1),
        name="conv_glu",
    )(xa, y, mod, gn, mod, mod, w)


def _dwconv_body(t, u_r, dw_r, b_r, g_r, be_r, o_r, pad_scr, cv_scr):
    c = u_r.shape[1]
    halo = 16
    pad_scr[0:halo, :] = jnp.zeros((halo, c), F32)
    pad_scr[t + halo:t + 2 * halo, :] = jnp.zeros((halo, c), F32)
    pad_scr[halo:t + halo, :] = u_r[...]
    rb_n = CONV_ROWS
    n_lb = c // LANES
    first = halo - CONV_WIDTH // 2

    def blk(lb, carry):
        ls = pl.ds(pl.multiple_of(lb * LANES, LANES), LANES)
        taps = [dw_r[k:k + 1, ls] for k in range(CONV_WIDTH)]
        bias = b_r[:, ls]
        for r0 in range(0, t, rb_n):
            win = pad_scr[r0:r0 + rb_n + 2 * halo, ls]
            acc = jnp.zeros((rb_n, LANES), F32)
            for k in range(CONV_WIDTH):
                acc = acc + win[first + k:first + k + rb_n, :] * taps[k]
            cv_scr[r0:r0 + rb_n, ls] = acc + bias
        return carry

    lax.fori_loop(0, n_lb, blk, 0)

    def ln(rb, carry):
        rs = pl.ds(pl.multiple_of(rb * rb_n, rb_n), rb_n)
        x = cv_scr[rs, :]
        mu = jnp.mean(x, axis=-1, keepdims=True)
        xc = x - mu
        var = jnp.mean(xc * xc, axis=-1, keepdims=True)
        yv = xc * lax.rsqrt(var + EPS) * g_r[...] + be_r[...]
        o_r[rs, :] = _silu(yv).astype(BF16)
        return carry

    lax.fori_loop(0, t // rb_n, ln, 0)


def _dwconv(u, row0, n_batch, t, dw, b, g, be):
    c = u.shape[1]
    return pl.pallas_call(
        functools.partial(_dwconv_body, t),
        grid=(n_batch,),
        in_specs=[pl.BlockSpec((t, c), lambda bi: (row0 // t + bi, 0)), _full_spec(dw.shape),
                  _row_spec(c), _row_spec(c), _row_spec(c)],
        out_specs=pl.BlockSpec((t, c), lambda bi: (bi, 0)),
        out_shape=jax.ShapeDtypeStruct((n_batch * t, c), BF16),
        scratch_shapes=[pltpu.VMEM((t + 32, c), F32), pltpu.VMEM((t, c), F32)],
        compiler_params=_params(1),
        name="conv_dw",
    )(u, dw, b, g, be)


def _final_body(xa, y, g2, gf, o):
    x = xa[...] + g2[...] * y[...]
    ms = jnp.mean(x * x, axis=-1, keepdims=True)
    o[...] = x * lax.rsqrt(ms + EPS) * gf[...]


def _final(geom, mod, depth, xa, y, gf):
    tm = TOKEN_TILE
    d = geom.d
    tok = pl.BlockSpec((tm, d), lambda i: (i, 0))
    return pl.pallas_call(
        _final_body,
        grid=(geom.n // tm,),
        in_specs=[tok, tok, geom.mod_spec(depth - 1, 5, tm), _row_spec(d)],
        out_specs=tok,
        out_shape=jax.ShapeDtypeStruct((geom.n, d), F32),
        compiler_params=_params(1),
        name="final_norm",
    )(xa, y, mod, gf)


def _rope_tables(n_tokens):
    rows = n_tokens // GRID_W
    row = jnp.repeat(jnp.arange(rows, dtype=F32), GRID_W)
    col = jnp.tile(jnp.arange(GRID_W, dtype=F32), rows)
    half = HEAD_DIM // 2
    inv = ROPE_THETA ** (-jnp.arange(0, half, 2, dtype=F32) / half)
    ang_r = row[:, None] * inv[None, :]
    ang_c = col[:, None] * inv[None, :]
    cr, sr, cc, sc = jnp.cos(ang_r), jnp.sin(ang_r), jnp.cos(ang_c), jnp.sin(ang_c)
    cos = jnp.concatenate([cr, cr, cc, cc], axis=1)
    sin = jnp.concatenate([-sr, sr, -sc, sc], axis=1)
    return (jnp.tile(cos, (1, A_HEADS)), jnp.tile(sin, (1, A_HEADS)),
            jnp.tile(cos, (1, A_KV_HEADS)), jnp.tile(sin, (1, A_KV_HEADS)))


def _router_weights(w_group, b_group, w_expert, b_expert):
    d = w_group.shape[0]
    n_used = MOE_GROUPS + MOE_GROUPS * EXPERTS_PER_GROUP
    w = jnp.concatenate([w_group, w_expert, jnp.zeros((d, ROUTE_W - n_used), F32)], axis=1)
    b = jnp.concatenate([b_group, b_expert, jnp.zeros((ROUTE_W - n_used,), F32)])[None, :]
    hi = w.astype(BF16)
    lo = (w - hi.astype(F32)).astype(BF16)
    return hi, lo, b


def kernel(x_prompt, x_sample, cache_k, cache_v, state_hgrn, c, c_ctx, ada_w, ada_b, norm1_g, norm2_g, mix_w_in, mix_w_out, attn_q_gain, attn_k_gain, hgrn_lower_bound, hgrn_out_gain, conv_pw1, conv_dw, conv_dw_b, conv_ln_g, conv_ln_b, conv_pw2, moe_w_group, moe_b_group, moe_w_expert, moe_b_expert, moe_w1, moe_w3, moe_w2, final_g):
    batch, seq, d = x_prompt.shape
    dec_batch, dec_seq, _ = x_sample.shape
    depth = ada_w.shape[0]
    past = cache_k.shape[2]
    n_ctx, n_smp = batch * seq, dec_batch * dec_seq
    n = n_ctx + n_smp
    geom = _Geom(n_ctx, n_smp, dec_seq, d)
    qw, kw, bw = A_HEADS * HEAD_DIM, A_KV_HEADS * HEAD_DIM, B_HEADS * HEAD_DIM
    n_nat = qw + 2 * kw

    cond8 = jnp.concatenate([c_ctx[None, :], c, jnp.zeros((8 - 1 - dec_batch, d), F32)], axis=0)
    mod = _ada(cond8, ada_w, ada_b).reshape(depth, 8, 6, 1, d)

    lb_soft = jax.nn.softmax(hgrn_lower_bound.astype(F32), axis=0)
    lower_bounds = jnp.cumsum(lb_soft, axis=0) - lb_soft[0]
    rope_tabs = _rope_tables(dec_seq)

    xa = jnp.concatenate([x_prompt.reshape(n_ctx, d), x_sample.reshape(n_smp, d)], axis=0)
    y = None
    new_k, new_v, new_s = [], [], []
    for l in range(depth):
        wrh, wrl, br = _router_weights(moe_w_group[l], moe_b_group[l], moe_w_expert[l], moe_b_expert[l])
        gn1 = norm1_g[l][None, :]
        gn2 = norm2_g[l][None, :]
        if l % 2 == 0:
            e = l // 2
            w_in = mix_w_in[e]
            wn = w_in[:, :n_nat].astype(BF16)
            wt = w_in[:, n_nat:].T.astype(BF16)
            qkv, hgt = _inproj(geom, mod, l, xa, y, gn1, wn, wt)
            qg = jnp.tile(attn_q_gain[e], A_HEADS)[None, :]
            kg = jnp.tile(attn_k_gain[e], A_KV_HEADS)[None, :]
            oa_c, khat = _attention(qkv, 0, batch, seq, qg, kg)
            ck = cache_k[:, e].reshape(dec_batch, past, kw)
            cv = cache_v[:, e].reshape(dec_batch, past, kw)
            oa_s = _attention(qkv, n_ctx, dec_batch, dec_seq, qg, kg, cache=(ck, cv), rope_tabs=rope_tabs)
            lb_b = jnp.broadcast_to(lower_bounds[e][:, :, None], (2, bw, LANES))
            gain_b = jnp.broadcast_to(hgrn_out_gain[e][:, None], (HEAD_DIM, LANES))
            ob_c, s_ctx = _hgrn(hgt, 0, batch, seq, lb_b, gain_b, None)
            s0t = jnp.swapaxes(state_hgrn[:, e], -1, -2)
            ob_s, _ = _hgrn(hgt, n_ctx, dec_batch, dec_seq, lb_b, gain_b, s0t)
            oa = jnp.concatenate([oa_c, oa_s], axis=0)
            ob = jnp.concatenate([ob_c, ob_s], axis=0)
            w_out = mix_w_out[e].astype(BF16)
            xnew, xr, route = _proj_res(geom, mod, l, [oa, ob], [w_out[:qw], w_out[qw:]], xa, y, gn2, wrh, wrl, br,
                                        "ab_outproj")
            new_k.append(khat.reshape(batch, seq, A_KV_HEADS, HEAD_DIM))
            new_v.append(qkv[:n_ctx, qw + kw:].reshape(batch, seq, A_KV_HEADS, HEAD_DIM))
            new_s.append(jnp.swapaxes(s_ctx, -1, -2))
        else:
            o = l // 2
            u = _glu(geom, mod, l, xa, y, gn1, conv_pw1[o].astype(BF16))
            cargs = (conv_dw[o], conv_dw_b[o][None, :], conv_ln_g[o][None, :], conv_ln_b[o][None, :])
            c_c = _dwconv(u, 0, batch, seq, *cargs)
            c_s = _dwconv(u, n_ctx, dec_batch, dec_seq, *cargs)
            cc = jnp.concatenate([c_c, c_s], axis=0)
            xnew, xr, route = _proj_res(geom, mod, l, [cc], [conv_pw2[o].astype(BF16)], xa, y, gn2, wrh, wrl, br,
                                        "conv_outproj")
        f = moe_w1.shape[-1]
        w1 = moe_w1[l].astype(BF16).reshape(MOE_GROUPS, EXPERTS_PER_GROUP, d, f)
        w3 = moe_w3[l].astype(BF16).reshape(MOE_GROUPS, EXPERTS_PER_GROUP, d, f)
        w2 = moe_w2[l].astype(BF16).reshape(MOE_GROUPS, EXPERTS_PER_GROUP, f, d)
        y = _moe(xr, route, w1, w3, w2, n, d)
        xa = xnew
    out = _final(geom, mod, depth, xa, y, final_g[None, :])
    y_prompt = out[:n_ctx].reshape(batch, seq, d)
    y_sample = out[n_ctx:].reshape(dec_batch, dec_seq, d)
    return (y_prompt, y_sample, jnp.stack(new_k, axis=1), jnp.stack(new_v, axis=1), jnp.stack(new_s, axis=1))
```

```python
import functools

import jax
import jax.numpy as jnp
from jax import lax
from jax.experimental import pallas as pl
from jax.experimental.pallas import tpu as pltpu

F32 = jnp.float32
BF16 = jnp.bfloat16

HEAD_DIM = 64
A_HEADS = 8
A_KV_HEADS = 2
A_GROUP = A_HEADS // A_KV_HEADS
B_HEADS = 8
GRID_W = 64
ROPE_THETA = 10000.0
GLA_CHUNK = 16
CONV_WIDTH = 31
MOE_GROUPS = 4
EXPERTS_PER_GROUP = 4
EPS = 1e-6

LANES = 128
VMEM_LIMIT = 56 * 1024 * 1024
TOKEN_TILE = 512
MOE_TILE = 256
ATTN_Q_TILE = 256
ROUTE_W = LANES
ROUTE_GID_LANE = 8
CONV_ROWS = 64

_NT = (((1,), (1,)), ((), ()))


def _params(n_axes):
    return pltpu.CompilerParams(dimension_semantics=("arbitrary",) * n_axes, vmem_limit_bytes=VMEM_LIMIT)


def _modulate(x, g, sh, sc):
    ms = jnp.mean(x * x, axis=-1, keepdims=True)
    return (x * lax.rsqrt(ms + EPS) * g) * (1.0 + sc) + sh


def _silu(x):
    return x * jax.nn.sigmoid(x)


def _split3(x):
    hi = x.astype(BF16)
    r1 = x - hi.astype(F32)
    mid = r1.astype(BF16)
    lo = (r1 - mid.astype(F32)).astype(BF16)
    return hi, mid, lo


def _ada_body(c_ref, w_ref, b_ref, o_ref):
    s = _silu(c_ref[...])
    o_ref[...] = jnp.dot(s.astype(BF16), w_ref[...].astype(BF16), preferred_element_type=F32) + b_ref[...]


def _ada(cond8, ada_w, ada_b):
    depth, d, d6 = ada_w.shape
    tn = d6 // 4
    return pl.pallas_call(
        _ada_body,
        grid=(depth, d6 // tn),
        in_specs=[
            pl.BlockSpec((8, d), lambda l, j: (0, 0)),
            pl.BlockSpec((None, d, tn), lambda l, j: (l, 0, j)),
            pl.BlockSpec((None, 1, tn), lambda l, j: (l, 0, j)),
        ],
        out_specs=pl.BlockSpec((None, 8, tn), lambda l, j: (l, 0, j)),
        out_shape=jax.ShapeDtypeStruct((depth, 8, d6), F32),
        compiler_params=_params(2),
        name="adaln",
    )(cond8, ada_w, ada_b.reshape(depth, 1, d6))


class _Geom:
    def __init__(self, n_ctx, n_smp, smp_len, d):
        self.n_ctx, self.n_smp, self.smp_len, self.d = n_ctx, n_smp, smp_len, d
        self.n = n_ctx + n_smp
        self.ctx_tiles = n_ctx // TOKEN_TILE

    def mod_spec(self, layer, j):
        ctx_tiles = self.ctx_tiles
        per_smp = self.smp_len // TOKEN_TILE

        def idx(i):
            row = jnp.where(i < ctx_tiles, 0, 1 + (i - ctx_tiles) // per_smp)
            return (layer, row, j, 0, 0)

        return pl.BlockSpec((None, None, None, 1, self.d), idx)

    def tok_specs(self, a):
        tm = TOKEN_TILE
        if isinstance(a, tuple):
            ct = self.ctx_tiles
            w = a[0].shape[1]
            return ([pl.BlockSpec((tm, w), lambda i: (jnp.minimum(i, ct - 1), 0)),
                     pl.BlockSpec((tm, w), lambda i: (jnp.maximum(i - ct, 0), 0))], list(a))
        return [pl.BlockSpec((tm, a.shape[1]), lambda i: (i, 0))], [a]

    def tok_load(self, refs):
        if len(refs) == 1:
            return refs[0][...]
        return jnp.where(pl.program_id(0) < self.ctx_tiles, refs[0][...], refs[1][...])


def _n_tok(a):
    return 2 if isinstance(a, tuple) else 1


def _row_spec(d):
    return pl.BlockSpec((1, d), lambda i: (0, 0))


def _resident(shape):
    nd = len(shape)
    return pl.BlockSpec(shape, lambda *_: (0,) * nd, pipeline_mode=pl.Buffered(1))


def _full_spec(shape):
    nd = len(shape)
    return pl.BlockSpec(shape, lambda *_: (0,) * nd)


def _inproj_body(geom, n_x, has_y, n_nat, *refs):
    xs, refs = refs[:n_x], refs[n_x:]
    x = geom.tok_load(xs)
    if has_y:
        y, g2, refs = refs[0], refs[1], refs[2:]
        x = x + g2[...] * y[...]
    gn, sh, sc, w, qkv_o, hgt_o, wn_s, wt_s = refs
    step = 512
    n_t = wt_s.shape[0]

    @pl.when(pl.program_id(0) == 0)
    def _():
        wn_s[...] = w[:, :n_nat].astype(BF16)
        for r in range(0, n_t, step):
            wt_s[r:r + step, :] = w[:, n_nat + r:n_nat + r + step].T.astype(BF16)

    h = _modulate(x, gn[...], sh[...], sc[...]).astype(BF16)
    qkv_o[...] = jnp.dot(h, wn_s[...], preferred_element_type=F32)
    for r in range(0, n_t, step):
        hgt_o[r:r + step, :] = lax.dot_general(wt_s[r:r + step, :], h, _NT, preferred_element_type=F32)


def _inproj(geom, mod, layer, xa, y, gn, w_in, n_nat):
    tm = TOKEN_TILE
    d = geom.d
    n_t = w_in.shape[1] - n_nat
    has_y = y is not None
    in_specs, args = geom.tok_specs(xa)
    if has_y:
        in_specs += [pl.BlockSpec((tm, d), lambda i: (i, 0)), geom.mod_spec(layer - 1, 5)]
        args += [y, mod]
    in_specs += [_row_spec(d), geom.mod_spec(layer, 0), geom.mod_spec(layer, 1), _resident(w_in.shape)]
    args += [gn, mod, mod, w_in]
    return pl.pallas_call(
        functools.partial(_inproj_body, geom, _n_tok(xa), has_y, n_nat),
        grid=(geom.n // tm,),
        in_specs=in_specs,
        out_specs=[pl.BlockSpec((tm, n_nat), lambda i: (i, 0)), pl.BlockSpec((n_t, tm), lambda i: (0, i))],
        out_shape=[jax.ShapeDtypeStruct((geom.n, n_nat), F32), jax.ShapeDtypeStruct((n_t, geom.n), F32)],
        scratch_shapes=[pltpu.VMEM((d, n_nat), BF16), pltpu.VMEM((n_t, d), BF16)],
        compiler_params=_params(1),
        name="ab_inproj",
    )(*args)


def _head_norm(x, gain_row, n_heads):
    outs = []
    for h in range(n_heads):
        xh = x[:, h * HEAD_DIM:(h + 1) * HEAD_DIM]
        ms = jnp.mean(xh * xh, axis=-1, keepdims=True)
        outs.append(xh * lax.rsqrt(ms + EPS))
    return jnp.concatenate(outs, axis=1) * gain_row


def _rope(x, cos, sin_signed):
    w = x.shape[1]
    lane = lax.broadcasted_iota(jnp.int32, (1, w), 1)
    first_half = (lane & 16) == 0
    partner = jnp.where(first_half, pltpu.roll(x, w - 16, 1), pltpu.roll(x, 16, 1))
    return x * cos + partner * sin_signed


def _attn_body(rope, tq, *refs):
    if rope:
        q_r, k_r, v_r, ck_r, cv_r, qg_r, kg_r, cq_r, sq_r, ckk_r, skk_r, o_r = refs
    else:
        q_r, k_r, v_r, qg_r, kg_r, o_r, kh_r = refs
    qn = _head_norm(q_r[...], qg_r[...], A_HEADS)
    kn = _head_norm(k_r[...], kg_r[...], A_KV_HEADS)
    if rope:
        qn = _rope(qn, cq_r[...], sq_r[...])
        kn = _rope(kn, ckk_r[...], skk_r[...])
        keys = jnp.concatenate([ck_r[...], kn], axis=0)
        vals = jnp.concatenate([cv_r[...], v_r[...]], axis=0)
    else:
        kh_r[...] = kn
        keys, vals = kn, v_r[...]
    scale = HEAD_DIM ** -0.5
    outs = [None] * A_HEADS
    for j in range(A_KV_HEADS):
        kj = keys[:, j * HEAD_DIM:(j + 1) * HEAD_DIM].astype(BF16)
        vj = vals[:, j * HEAD_DIM:(j + 1) * HEAD_DIM].astype(BF16)
        heads = [A_GROUP * j + g for g in range(A_GROUP)]
        q4 = jnp.concatenate([qn[:, h * HEAD_DIM:(h + 1) * HEAD_DIM] for h in heads], axis=0).astype(BF16)
        s = lax.dot_general(q4, kj, _NT, preferred_element_type=F32) * scale
        m = jnp.max(s, axis=-1, keepdims=True)
        p = jnp.exp(s - m)
        l = jnp.sum(p, axis=-1, keepdims=True)
        o = jnp.dot(p.astype(BF16), vj, preferred_element_type=F32) / l
        for g, h in enumerate(heads):
            outs[h] = o[g * tq:(g + 1) * tq, :]
    o_r[...] = jnp.concatenate(outs, axis=1).astype(BF16)


def _attention(qkv, row0, n_batch, t, q_gain, k_gain, cache=None, rope_tabs=None):
    tq = ATTN_Q_TILE
    nq = t // tq
    qw, kw = A_HEADS * HEAD_DIM, A_KV_HEADS * HEAD_DIM
    rope = cache is not None
    q_spec = pl.BlockSpec((tq, qw), lambda b, qi: (row0 // tq + b * nq + qi, 0))
    k_spec = pl.BlockSpec((t, kw), lambda b, qi: (row0 // t + b, qw // kw))
    v_spec = pl.BlockSpec((t, kw), lambda b, qi: (row0 // t + b, qw // kw + 1))
    gq = pl.BlockSpec((1, qw), lambda b, qi: (0, 0))
    gk = pl.BlockSpec((1, kw), lambda b, qi: (0, 0))
    out_o = pl.BlockSpec((tq, qw), lambda b, qi: (b * nq + qi, 0))
    o_shape = jax.ShapeDtypeStruct((n_batch * t, qw), BF16)
    if rope:
        ck, cv = cache
        past = ck.shape[1]
        c_spec = pl.BlockSpec((None, past, kw), lambda b, qi: (b, 0, 0))
        cq, sq, ckk, skk = rope_tabs
        in_specs = [q_spec, k_spec, v_spec, c_spec, c_spec, gq, gk,
                    pl.BlockSpec((tq, qw), lambda b, qi: (qi, 0)), pl.BlockSpec((tq, qw), lambda b, qi: (qi, 0)),
                    pl.BlockSpec((t, kw), lambda b, qi: (0, 0)), pl.BlockSpec((t, kw), lambda b, qi: (0, 0))]
        args = [qkv, qkv, qkv, ck, cv, q_gain, k_gain, cq, sq, ckk, skk]
        out_specs, out_shape = out_o, o_shape
    else:
        in_specs = [q_spec, k_spec, v_spec, gq, gk]
        args = [qkv, qkv, qkv, q_gain, k_gain]
        out_specs = [out_o, pl.BlockSpec((t, kw), lambda b, qi: (b, 0))]
        out_shape = [o_shape, jax.ShapeDtypeStruct((n_batch * t, kw), F32)]
    return pl.pallas_call(
        functools.partial(_attn_body, rope, tq),
        grid=(n_batch, nq),
        in_specs=in_specs,
        out_specs=out_specs,
        out_shape=out_shape,
        compiler_params=_params(2),
        name="gqa_latent" if rope else "gqa_context",
    )(*args)


def _hgrn_body(zero_init, t, *refs):
    if zero_init:
        (q_r, ff_r, fb_r, v_r, g_r, lb_r, gain_r, ob_o, send_o,
         oi_scr, u_scr, dec_scr, qt_scr, sst_scr, on_scr) = refs
        s0_r = None
    else:
        (q_r, ff_r, fb_r, v_r, g_r, lb_r, gain_r, s0_r, ob_o, send_o,
         oi_scr, u_scr, dec_scr, qt_scr, sst_scr, on_scr) = refs
    nt = t // LANES
    hd = HEAD_DIM
    pw = 2 * hd
    cpt = LANES // GLA_CHUNK
    lane = lax.broadcasted_iota(jnp.int32, (1, LANES), 1)
    pos = lane & (GLA_CHUNK - 1)
    ri = lax.broadcasted_iota(jnp.int32, (LANES, LANES), 0)
    ci = lax.broadcasted_iota(jnp.int32, (LANES, LANES), 1)
    same = (ri >> 4) == (ci >> 4)
    dmat = ci - ri

    def ind(cond):
        return jnp.where(cond, 1.0, 0.0).astype(BF16)

    m_fwd = jnp.concatenate([ind(same & (ri <= ci)), ind(same & (ri > ci))], axis=1)
    m_bwd = jnp.concatenate([ind(same & (ri >= ci)), ind(same & (ri < ci))], axis=1)
    sel_r = lax.broadcasted_iota(jnp.int32, (cpt, LANES), 0)
    sel_c = lax.broadcasted_iota(jnp.int32, (cpt, LANES), 1)
    sel = ind((sel_c >> 4) == sel_r)
    bd_r = lax.broadcasted_iota(jnp.int32, (cpt * pw, LANES), 0)
    bd_c = lax.broadcasted_iota(jnp.int32, (cpt * pw, LANES), 1)
    bd = (bd_r >> 7) == (bd_c >> 4)
    top = ((bd_r >> 6) & 1) == 0
    bd_all = ind(bd)
    bd_top = ind(bd & top)
    bd_bot = ind(bd & jnp.logical_not(top))
    u_lane = lax.broadcasted_iota(jnp.int32, (1, cpt * pw), 1)
    u_top = ((u_lane >> 6) & 1) == 0
    gain2 = jnp.concatenate([gain_r[...], gain_r[...]], axis=0)

    def colsum(a):
        return jnp.sum(a, axis=0, keepdims=True)

    def dot(a, b):
        return jnp.dot(a, b, preferred_element_type=F32)

    def pair_step(pr, carry):
        rows = pl.ds(pl.multiple_of(pr * pw, pw), pw)

        def phase_a(ti, c):
            ls = pl.ds(pl.multiple_of(ti * LANES, LANES), LANES)
            q = q_r[rows, ls]
            v = v_r[rows, ls]
            vb = v.astype(BF16)
            ats = [None, None]
            for bwd in (False, True):
                d_idx = int(bwd)
                raw_r = fb_r if bwd else ff_r
                lbv = lb_r[d_idx, rows, :]
                f = lbv + (1.0 - lbv) * jax.nn.sigmoid(raw_r[rows, ls])
                pm = (GLA_CHUNK - 1 - pos) if bwd else pos
                g_prev = [colsum(q[:hd]), colsum(q[hd:])]
                dprod = f
                for dl in range(GLA_CHUNK):
                    if dl > 0:
                        dprod = dprod * pltpu.roll(f, (LANES - dl) if bwd else dl, 1)
                    qd = q * dprod
                    g_cur = [colsum(qd[:hd]), colsum(qd[hd:])]
                    hit = dmat == (-dl if bwd else dl)
                    for k in range(2):
                        a = jnp.where(pm >= dl, g_prev[k] - g_cur[k], 0.0)
                        if dl > 0:
                            ats[k] = jnp.where(hit, a, ats[k])
                        elif bwd:
                            ats[k] = ats[k] + jnp.where(hit, a, 0.0)
                        else:
                            ats[k] = jnp.where(hit, a, 0.0)
                    g_prev = g_cur
                hi, mid, lo = _split3(jnp.log(f))
                m_cs = m_bwd if bwd else m_fwd
                cs = dot(hi, m_cs) + dot(mid, m_cs) + dot(lo, m_cs)
                qt_scr[d_idx, :, ls] = q * jnp.exp(cs[:, :LANES])
                kt = (1.0 - f) * jnp.exp(cs[:, LANES:])
                dsum = (lax.dot_general(sel, hi, _NT, preferred_element_type=F32)
                        + lax.dot_general(sel, mid, _NT, preferred_element_type=F32)
                        + lax.dot_general(sel, lo, _NT, preferred_element_type=F32))
                dec_scr[d_idx, ti] = jnp.exp(dsum)
                kbd = jnp.concatenate([kt.astype(BF16)] * cpt, axis=0) * bd_all
                uo = lax.dot_general(vb, kbd, _NT, preferred_element_type=F32)
                u_scr[d_idx, ti] = jnp.where(u_top, uo[:hd], uo[hd:])
            oi_scr[:, ls] = jnp.concatenate([dot(vb[:hd], ats[0].astype(BF16)), dot(vb[hd:], ats[1].astype(BF16))],
                                            axis=0)
            return c

        lax.fori_loop(0, nt, phase_a, 0)

        def phase_b(ti, st):
            sf, sb = st
            u = u_scr[0, ti]
            dec = dec_scr[0, ti]
            pieces = [None] * cpt
            for c in range(cpt):
                pieces[c] = sf
                sf = dec[c:c + 1, :] * sf + u[:, c * pw:(c + 1) * pw]
            sst_scr[0, ti] = jnp.concatenate(pieces, axis=1).astype(BF16)
            tb = nt - 1 - ti
            u = u_scr[1, tb]
            dec = dec_scr[1, tb]
            for c in range(cpt - 1, -1, -1):
                pieces[c] = sb
                sb = dec[c:c + 1, :] * sb + u[:, c * pw:(c + 1) * pw]
            sst_scr[1, tb] = jnp.concatenate(pieces, axis=1).astype(BF16)
            return sf, sb

        if zero_init:
            s0 = (jnp.zeros((hd, pw), F32), jnp.zeros((hd, pw), F32))
        else:
            s0 = tuple(jnp.concatenate([s0_r[k, 2 * pr], s0_r[k, 2 * pr + 1]], axis=1) for k in range(2))
        s_end = lax.fori_loop(0, nt, phase_b, s0)
        for k in range(2):
            send_o[k, 2 * pr] = s_end[k][:, :hd]
            send_o[k, 2 * pr + 1] = s_end[k][:, hd:]

        def phase_c(ti, c):
            ls = pl.ds(pl.multiple_of(ti * LANES, LANES), LANES)
            o = oi_scr[:, ls]
            for k in range(2):
                t8 = jnp.concatenate([qt_scr[k, :, ls].astype(BF16)] * cpt, axis=0)
                qb = jnp.concatenate([t8 * bd_top, t8 * bd_bot], axis=1)
                oo = dot(sst_scr[k, ti], qb)
                o = o + jnp.concatenate([oo[:, :LANES], oo[:, LANES:]], axis=0)
            o0, o1 = o[:hd], o[hd:]
            on = jnp.concatenate([o0 * lax.rsqrt(jnp.mean(o0 * o0, axis=0, keepdims=True) + EPS),
                                  o1 * lax.rsqrt(jnp.mean(o1 * o1, axis=0, keepdims=True) + EPS)], axis=0)
            on_scr[rows, ls] = on * gain2 * _silu(g_r[rows, ls])
            return c

        lax.fori_loop(0, nt, phase_c, 0)
        return carry

    lax.fori_loop(0, B_HEADS // 2, pair_step, 0)
    ob_o[...] = on_scr[...].T.astype(BF16)


def _hgrn(hgt, col0, n_batch, t, lb_b, gain_b, s0t):
    bw = B_HEADS * HEAD_DIM
    hd, pw = HEAD_DIM, 2 * HEAD_DIM
    nt = t // LANES
    cpt = LANES // GLA_CHUNK
    zero_init = s0t is None
    specs = [pl.BlockSpec((bw, t), functools.partial(lambda r, b: (r, col0 // t + b), r)) for r in range(5)]
    in_specs = specs + [_full_spec(lb_b.shape), _full_spec(gain_b.shape)]
    args = [hgt] * 5 + [lb_b, gain_b]
    st_spec = pl.BlockSpec((None, 2, B_HEADS, HEAD_DIM, HEAD_DIM), lambda b: (b, 0, 0, 0, 0))
    if not zero_init:
        in_specs.append(st_spec)
        args.append(s0t)
    scratch = [pltpu.VMEM((pw, t), F32), pltpu.VMEM((2, nt, hd, cpt * pw), F32), pltpu.VMEM((2, nt, cpt, pw), F32),
               pltpu.VMEM((2, pw, t), F32), pltpu.VMEM((2, nt, hd, cpt * pw), BF16), pltpu.VMEM((bw, t), F32)]
    return pl.pallas_call(
        functools.partial(_hgrn_body, zero_init, t),
        grid=(n_batch,),
        in_specs=in_specs,
        out_specs=[pl.BlockSpec((t, bw), lambda b: (b, 0)), st_spec],
        out_shape=[jax.ShapeDtypeStruct((n_batch * t, bw), BF16),
                   jax.ShapeDtypeStruct((n_batch, 2, B_HEADS, HEAD_DIM, HEAD_DIM), F32)],
        scratch_shapes=scratch,
        compiler_params=_params(1),
        name="hgrn2_context" if zero_init else "hgrn2_latent",
    )(*args)


def _route(logits):
    lane_i = lax.broadcasted_iota(jnp.int32, logits.shape, 1)
    lane = lane_i.astype(F32)
    neg = -jnp.inf
    big = 1e9
    gl = jnp.where(lane_i < MOE_GROUPS, logits, neg)
    gmax = jnp.max(gl, axis=-1, keepdims=True)
    g_val = 1.0 / jnp.sum(jnp.exp(gl - gmax), axis=-1, keepdims=True)
    g_idx = jnp.min(jnp.where(gl == gmax, lane, big), axis=-1, keepdims=True)
    n_e = MOE_GROUPS * EXPERTS_PER_GROUP
    lane_group = ((lane_i - MOE_GROUPS) >> 2).astype(F32)
    in_group = (lane_i >= MOE_GROUPS) & (lane_i < MOE_GROUPS + n_e) & (lane_group == g_idx)
    el = jnp.where(in_group, logits, neg)
    m1 = jnp.max(el, axis=-1, keepdims=True)
    i1 = jnp.min(jnp.where(el == m1, lane, big), axis=-1, keepdims=True)
    el2 = jnp.where(lane == i1, neg, el)
    m2 = jnp.max(el2, axis=-1, keepdims=True)
    i2 = jnp.min(jnp.where(el2 == m2, lane, big), axis=-1, keepdims=True)
    tt = jnp.exp(m2 - m1)
    w1 = g_val / (1.0 + tt)
    w2 = g_val * tt / (1.0 + tt)
    base = MOE_GROUPS + EXPERTS_PER_GROUP * g_idx
    route = jnp.where(lane == i1 - base, w1, 0.0) + jnp.where(lane == i2 - base, w2, 0.0)
    return jnp.where(lane_i == ROUTE_GID_LANE, g_idx, route)


def _proj_res_body(geom, lhs_n, n_x, has_y, d, *refs):
    lhs = []
    for n_a in lhs_n:
        lhs.append(geom.tok_load(refs[:n_a]))
        refs = refs[n_a:]
    ws, refs = refs[:len(lhs_n)], refs[len(lhs_n):]
    xs, refs = refs[:n_x], refs[n_x:]
    x = geom.tok_load(xs)
    if has_y:
        y, g2, refs = refs[0], refs[1], refs[2:]
        x = x + g2[...] * y[...]
    g1, gn2, sh2, sc2, wrh, wrl, br, xnew_o, xr_o, route_o = refs[:10]
    wbs = refs[10:]

    @pl.when(pl.program_id(0) == 0)
    def _():
        for w, wb in zip(ws, wbs):
            wb[...] = w[...].astype(BF16)

    acc = jnp.dot(lhs[0], wbs[0][...], preferred_element_type=F32)
    for a, wb in zip(lhs[1:], wbs[1:]):
        acc = acc + jnp.dot(a, wb[...], preferred_element_type=F32)
    xn = x + g1[...] * acc
    xnew_o[...] = xn
    hm = _modulate(xn, gn2[...], sh2[...], sc2[...])
    hi = hm.astype(BF16)
    lo = (hm - hi.astype(F32)).astype(BF16)
    logits = (jnp.dot(hi, wrh[...], preferred_element_type=F32) + jnp.dot(lo, wrh[...], preferred_element_type=F32)
              + jnp.dot(hi, wrl[...], preferred_element_type=F32)) + br[...]
    route = _route(logits)
    xr_o[:, :d] = hm
    xr_o[:, d:] = route
    route_o[...] = route


def _proj_res(geom, mod, layer, lhs, ws, xa, y, gn2, wrh, wrl, br, name):
    tm = TOKEN_TILE
    d = geom.d
    has_y = y is not None
    tok = pl.BlockSpec((tm, d), lambda i: (i, 0))
    in_specs, args = [], []
    for a in lhs:
        s, ar = geom.tok_specs(a)
        in_specs += s
        args += ar
    in_specs += [_resident(w.shape) for w in ws]
    args += list(ws)
    s, ar = geom.tok_specs(xa)
    in_specs += s
    args += ar
    if has_y:
        in_specs += [tok, geom.mod_spec(layer - 1, 5)]
        args += [y, mod]
    in_specs += [geom.mod_spec(layer, 2), _row_spec(d), geom.mod_spec(layer, 3), geom.mod_spec(layer, 4),
                 _full_spec(wrh.shape), _full_spec(wrl.shape), _full_spec(br.shape)]
    args += [mod, gn2, mod, mod, wrh, wrl, br]
    return pl.pallas_call(
        functools.partial(_proj_res_body, geom, tuple(_n_tok(a) for a in lhs), _n_tok(xa), has_y, d),
        grid=(geom.n // tm,),
        in_specs=in_specs,
        out_specs=[tok, pl.BlockSpec((tm, d + ROUTE_W), lambda i: (i, 0)), pl.BlockSpec((tm, ROUTE_W), lambda i: (i, 0))],
        out_shape=[jax.ShapeDtypeStruct((geom.n, d), F32), jax.ShapeDtypeStruct((geom.n, d + ROUTE_W), F32),
                   jax.ShapeDtypeStruct((geom.n, ROUTE_W), F32)],
        scratch_shapes=[pltpu.VMEM(w.shape, BF16) for w in ws],
        compiler_params=_params(1),
        name=name,
    )(*args)


def _dispatch(route, n, tile):
    n_tiles = n // tile + MOE_GROUPS
    n_slots = n_tiles * tile
    gid = route[:, ROUTE_GID_LANE].astype(jnp.int32)
    onehot = (gid[:, None] == jnp.arange(MOE_GROUPS, dtype=jnp.int32)[None, :]).astype(jnp.int32)
    csum = jnp.cumsum(onehot, axis=0)
    counts = csum[-1]
    rank = jnp.sum(csum * onehot, axis=1) - 1
    tiles_g = (counts + tile - 1) // tile
    tile_end = jnp.cumsum(tiles_g)
    tile_start = tile_end - tiles_g
    dest = (tile_start * tile)[gid] + rank
    tok = jnp.arange(n, dtype=jnp.int32)
    src = jnp.zeros((n_slots,), jnp.int32).at[dest].set(tok)
    slot = jnp.arange(-tile, n_slots, dtype=jnp.int32)
    spare = n + (((slot // tile) & 1) * tile) + (slot % tile)
    dst = spare.at[dest + tile].set(tok)
    t_idx = jnp.arange(n_tiles, dtype=jnp.int32)
    tile_gid = jnp.minimum(jnp.sum((t_idx[:, None] >= tile_end[None, :]).astype(jnp.int32), axis=1), MOE_GROUPS - 1)
    return tile_gid, src, dst, n_tiles


def _moe_body(tile, d, n_rows, tg_ref, src_ref, dst_ref, xr_hbm, w1_ref, w3_ref, w2_ref, y_hbm,
              xbuf, ybuf, w1b, w3b, w2b, isem, osem):
    t = pl.program_id(0)
    last = pl.num_programs(0) - 1
    slot = t % 2
    other = 1 - slot

    def row_in(tile_idx, s, i):
        row = src_ref[tile_idx * tile + i]
        return pltpu.make_async_copy(xr_hbm.at[pl.ds(row, 1), :], xbuf.at[s, pl.ds(i, 1), :], isem.at[s])

    def row_out(ext_tile_idx, s, i):
        row = dst_ref[ext_tile_idx * tile + i]
        return pltpu.make_async_copy(ybuf.at[s, pl.ds(i, 1), :], y_hbm.at[pl.ds(row, 1), :], osem.at[s])

    @pl.when(t == 0)
    def _():
        ybuf[...] = jnp.zeros(ybuf.shape, F32)
        for i in range(tile):
            row_in(0, 0, i).start()
        for i in range(tile):
            pltpu.make_async_copy(ybuf.at[0, pl.ds(i, 1), :], y_hbm.at[pl.ds(n_rows + i, 1), :], osem.at[0]).start()

    g = tg_ref[t]
    g_prev = tg_ref[jnp.maximum(t - 1, 0)]

    @pl.when((t == 0) | (g != g_prev))
    def _():
        w1b[...] = w1_ref[...].astype(BF16)
        w3b[...] = w3_ref[...].astype(BF16)
        w2b[...] = w2_ref[...].astype(BF16)

    nxt = jnp.minimum(t + 1, last)
    n_parts = 4 * EXPERTS_PER_GROUP
    per = tile // n_parts

    def step(slot, other):
        for i in range(tile):
            row_in(0, slot, i).wait()
        part = [0]

        def issue_part():
            k = part[0]
            part[0] += 1
            for i in range(k * per, (k + 1) * per):
                row_in(nxt, other, i).start()
                row_out(t, other, i).start()

        xt = xbuf[slot]
        xb = xt[:, :d].astype(BF16)
        cw = xt[:, d:]
        acc = jnp.zeros((tile, d), F32)
        for e in range(EXPERTS_PER_GROUP):
            a = jnp.dot(xb, w1b[e], preferred_element_type=F32)
            issue_part()
            b = jnp.dot(xb, w3b[e], preferred_element_type=F32)
            issue_part()
            hid = (_silu(a) * b * cw[:, e:e + 1]).astype(BF16)
            issue_part()
            acc = acc + jnp.dot(hid, w2b[e], preferred_element_type=F32)
            issue_part()
        for i in range(tile):
            row_out(0, slot, i).wait()
        ybuf[slot] = acc

        @pl.when(t == last)
        def _():
            for i in range(tile):
                row_in(0, other, i).wait()
            for i in range(tile):
                row_out(0, other, i).wait()
            for i in range(tile):
                row_out(t + 1, slot, i).start()
            for i in range(tile):
                row_out(0, slot, i).wait()

    @pl.when(slot == 0)
    def _():
        step(0, 1)

    @pl.when(slot == 1)
    def _():
        step(1, 0)


def _moe(xr, route, w1, w3, w2, n, d):
    tile = MOE_TILE
    tile_gid, src, dst, n_tiles = _dispatch(route, n, tile)
    wspec13 = pl.BlockSpec((None,) + w1.shape[1:], lambda t, tg, s, ds_: (tg[t], 0, 0, 0))
    wspec2 = pl.BlockSpec((None,) + w2.shape[1:], lambda t, tg, s, ds_: (tg[t], 0, 0, 0))
    grid_spec = pltpu.PrefetchScalarGridSpec(
        num_scalar_prefetch=3,
        grid=(n_tiles,),
        in_specs=[pl.BlockSpec(memory_space=pl.ANY), wspec13, wspec13, wspec2],
        out_specs=pl.BlockSpec(memory_space=pl.ANY),
        scratch_shapes=[pltpu.VMEM((2, tile, d + ROUTE_W), F32), pltpu.VMEM((2, tile, d), F32),
                        pltpu.VMEM(w1.shape[1:], BF16), pltpu.VMEM(w3.shape[1:], BF16), pltpu.VMEM(w2.shape[1:], BF16),
                        pltpu.SemaphoreType.DMA((2,)), pltpu.SemaphoreType.DMA((2,))],
    )
    return pl.pallas_call(
        functools.partial(_moe_body, tile, d, n),
        grid_spec=grid_spec,
        out_shape=jax.ShapeDtypeStruct((n + 2 * tile, d), F32),
        compiler_params=_params(1),
        name="moe_ffn",
    )(tile_gid, src, dst, xr, w1, w3, w2)


def _glu_body(xa, y, g2, gn, sh, sc, w, u_o, wb):
    @pl.when(pl.program_id(0) == 0)
    def _():
        wb[...] = w[...].astype(BF16)

    x = xa[...] + g2[...] * y[...]
    h = _modulate(x, gn[...], sh[...], sc[...]).astype(BF16)
    ab = jnp.dot(h, wb[...], preferred_element_type=F32)
    c = ab.shape[1] // 2
    u_o[...] = ab[:, :c] * jax.nn.sigmoid(ab[:, c:])


def _glu(geom, mod, layer, xa, y, gn, w):
    tm = TOKEN_TILE
    d = geom.d
    tok = pl.BlockSpec((tm, d), lambda i: (i, 0))
    c = w.shape[1] // 2
    return pl.pallas_call(
        _glu_body,
        grid=(geom.n // tm,),
        in_specs=[tok, tok, geom.mod_spec(layer - 1, 5), _row_spec(d), geom.mod_spec(layer, 0),
                  geom.mod_spec(layer, 1), _resident(w.shape)],
        out_specs=pl.BlockSpec((tm, c), lambda i: (i, 0)),
        out_shape=jax.ShapeDtypeStruct((geom.n, c), F32),
        scratch_shapes=[pltpu.VMEM(w.shape, BF16)],
        compiler_params=_params(1),
        name="conv_glu",
    )(xa, y, mod, gn, mod, mod, w)


def _dwconv_body(t, u_r, dw_r, b_r, g_r, be_r, o_r, pad_scr, cv_scr):
    c = u_r.shape[1]
    halo = 16
    pad_scr[0:halo, :] = jnp.zeros((halo, c), F32)
    pad_scr[t + halo:t + 2 * halo, :] = jnp.zeros((halo, c), F32)
    pad_scr[halo:t + halo, :] = u_r[...]
    rb_n = CONV_ROWS
    n_lb = c // LANES
    first = halo - CONV_WIDTH // 2

    def blk(lb, carry):
        ls = pl.ds(pl.multiple_of(lb * LANES, LANES), LANES)
        taps = [dw_r[k:k + 1, ls] for k in range(CONV_WIDTH)]
        bias = b_r[:, ls]
        for r0 in range(0, t, rb_n):
            win = pad_scr[r0:r0 + rb_n + 2 * halo, ls]
            acc = jnp.zeros((rb_n, LANES), F32)
            for k in range(CONV_WIDTH):
                acc = acc + win[first + k:first + k + rb_n, :] * taps[k]
            cv_scr[r0:r0 + rb_n, ls] = acc + bias
        return carry

    lax.fori_loop(0, n_lb, blk, 0)

    def ln(rb, carry):
        rs = pl.ds(pl.multiple_of(rb * rb_n, rb_n), rb_n)
        x = cv_scr[rs, :]
        mu = jnp.mean(x, axis=-1, keepdims=True)
        xc = x - mu
        var = jnp.mean(xc * xc, axis=-1, keepdims=True)
        yv = xc * lax.rsqrt(var + EPS) * g_r[...] + be_r[...]
        o_r[rs, :] = _silu(yv).astype(BF16)
        return carry

    lax.fori_loop(0, t // rb_n, ln, 0)


def _dwconv(u, row0, n_batch, t, dw, b, g, be):
    c = u.shape[1]
    return pl.pallas_call(
        functools.partial(_dwconv_body, t),
        grid=(n_batch,),
        in_specs=[pl.BlockSpec((t, c), lambda bi: (row0 // t + bi, 0)), _full_spec(dw.shape),
                  _row_spec(c), _row_spec(c), _row_spec(c)],
        out_specs=pl.BlockSpec((t, c), lambda bi: (bi, 0)),
        out_shape=jax.ShapeDtypeStruct((n_batch * t, c), BF16),
        scratch_shapes=[pltpu.VMEM((t + 32, c), F32), pltpu.VMEM((t, c), F32)],
        compiler_params=_params(1),
        name="conv_dw",
    )(u, dw, b, g, be)


def _final_body(ctx_tiles, xa, y, g2, gf, oc, os_):
    x = xa[...] + g2[...] * y[...]
    ms = jnp.mean(x * x, axis=-1, keepdims=True)
    val = x * lax.rsqrt(ms + EPS) * gf[...]
    i = pl.program_id(0)

    @pl.when(i < ctx_tiles)
    def _():
        oc[...] = val

    @pl.when(i >= ctx_tiles)
    def _():
        os_[...] = val


def _final(geom, mod, depth, xa, y, gf):
    tm = TOKEN_TILE
    d = geom.d
    ct = geom.ctx_tiles
    tok = pl.BlockSpec((tm, d), lambda i: (i, 0))
    return pl.pallas_call(
        functools.partial(_final_body, ct),
        grid=(geom.n // tm,),
        in_specs=[tok, tok, geom.mod_spec(depth - 1, 5), _row_spec(d)],
        out_specs=[pl.BlockSpec((tm, d), lambda i: (jnp.minimum(i, ct - 1), 0)),
                   pl.BlockSpec((tm, d), lambda i: (jnp.maximum(i - ct, 0), 0))],
        out_shape=[jax.ShapeDtypeStruct((geom.n_ctx, d), F32), jax.ShapeDtypeStruct((geom.n_smp, d), F32)],
        compiler_params=_params(1),
        name="final_norm",
    )(xa, y, mod, gf)


def _rope_tables(n_tokens):
    rows = n_tokens // GRID_W
    row = jnp.repeat(jnp.arange(rows, dtype=F32), GRID_W)
    col = jnp.tile(jnp.arange(GRID_W, dtype=F32), rows)
    half = HEAD_DIM // 2
    inv = ROPE_THETA ** (-jnp.arange(0, half, 2, dtype=F32) / half)
    ang_r = row[:, None] * inv[None, :]
    ang_c = col[:, None] * inv[None, :]
    cr, sr, cc, sc = jnp.cos(ang_r), jnp.sin(ang_r), jnp.cos(ang_c), jnp.sin(ang_c)
    cos = jnp.concatenate([cr, cr, cc, cc], axis=1)
    sin = jnp.concatenate([-sr, sr, -sc, sc], axis=1)
    return (jnp.tile(cos, (1, A_HEADS)), jnp.tile(sin, (1, A_HEADS)),
            jnp.tile(cos, (1, A_KV_HEADS)), jnp.tile(sin, (1, A_KV_HEADS)))


def _router_weights(w_group, b_group, w_expert, b_expert):
    d = w_group.shape[0]
    n_used = MOE_GROUPS + MOE_GROUPS * EXPERTS_PER_GROUP
    w = jnp.concatenate([w_group, w_expert, jnp.zeros((d, ROUTE_W - n_used), F32)], axis=1)
    b = jnp.concatenate([b_group, b_expert, jnp.zeros((ROUTE_W - n_used,), F32)])[None, :]
    hi = w.astype(BF16)
    lo = (w - hi.astype(F32)).astype(BF16)
    return hi, lo, b


def kernel(x_prompt, x_sample, cache_k, cache_v, state_hgrn, c, c_ctx, ada_w, ada_b, norm1_g, norm2_g, mix_w_in, mix_w_out, attn_q_gain, attn_k_gain, hgrn_lower_bound, hgrn_out_gain, conv_pw1, conv_dw, conv_dw_b, conv_ln_g, conv_ln_b, conv_pw2, moe_w_group, moe_b_group, moe_w_expert, moe_b_expert, moe_w1, moe_w3, moe_w2, final_g):
    batch, seq, d = x_prompt.shape
    dec_batch, dec_seq, _ = x_sample.shape
    depth = ada_w.shape[0]
    past = cache_k.shape[2]
    n_ctx, n_smp = batch * seq, dec_batch * dec_seq
    n = n_ctx + n_smp
    geom = _Geom(n_ctx, n_smp, dec_seq, d)
    qw, kw, bw = A_HEADS * HEAD_DIM, A_KV_HEADS * HEAD_DIM, B_HEADS * HEAD_DIM
    n_nat = qw + 2 * kw

    cond8 = jnp.concatenate([c_ctx[None, :], c, jnp.zeros((8 - 1 - dec_batch, d), F32)], axis=0)
    mod = _ada(cond8, ada_w, ada_b).reshape(depth, 8, 6, 1, d)

    lb_soft = jax.nn.softmax(hgrn_lower_bound.astype(F32), axis=0)
    lower_bounds = jnp.cumsum(lb_soft, axis=0) - lb_soft[0]
    rope_tabs = _rope_tables(dec_seq)

    xa = (x_prompt.reshape(n_ctx, d), x_sample.reshape(n_smp, d))
    y = None
    f = moe_w1.shape[-1]
    new_k, new_v, new_s = [], [], []
    for l in range(depth):
        wrh, wrl, br = _router_weights(moe_w_group[l], moe_b_group[l], moe_w_expert[l], moe_b_expert[l])
        gn1 = norm1_g[l][None, :]
        gn2 = norm2_g[l][None, :]
        if l % 2 == 0:
            e = l // 2
            qkv, hgt = _inproj(geom, mod, l, xa, y, gn1, mix_w_in[e], n_nat)
            qg = jnp.tile(attn_q_gain[e], A_HEADS)[None, :]
            kg = jnp.tile(attn_k_gain[e], A_KV_HEADS)[None, :]
            oa_c, khat = _attention(qkv, 0, batch, seq, qg, kg)
            ck = cache_k[:, e].reshape(dec_batch, past, kw)
            cv = cache_v[:, e].reshape(dec_batch, past, kw)
            oa_s = _attention(qkv, n_ctx, dec_batch, dec_seq, qg, kg, cache=(ck, cv), rope_tabs=rope_tabs)
            lb_b = jnp.broadcast_to(lower_bounds[e][:, :, None], (2, bw, LANES))
            gain_b = jnp.broadcast_to(hgrn_out_gain[e][:, None], (HEAD_DIM, LANES))
            ob_c, s_ctx = _hgrn(hgt, 0, batch, seq, lb_b, gain_b, None)
            s0t = jnp.swapaxes(state_hgrn[:, e], -1, -2)
            ob_s, _ = _hgrn(hgt, n_ctx, dec_batch, dec_seq, lb_b, gain_b, s0t)
            w_out = mix_w_out[e]
            xnew, xr, route = _proj_res(geom, mod, l, [(oa_c, oa_s), (ob_c, ob_s)], [w_out[:qw], w_out[qw:]], xa, y,
                                        gn2, wrh, wrl, br, "ab_outproj")
            new_k.append(khat.reshape(batch, seq, A_KV_HEADS, HEAD_DIM))
            new_v.append(qkv[:n_ctx, qw + kw:].reshape(batch, seq, A_KV_HEADS, HEAD_DIM))
            new_s.append(jnp.swapaxes(s_ctx, -1, -2))
        else:
            o = l // 2
            u = _glu(geom, mod, l, xa, y, gn1, conv_pw1[o])
            cargs = (conv_dw[o], conv_dw_b[o][None, :], conv_ln_g[o][None, :], conv_ln_b[o][None, :])
            c_c = _dwconv(u, 0, batch, seq, *cargs)
            c_s = _dwconv(u, n_ctx, dec_batch, dec_seq, *cargs)
            xnew, xr, route = _proj_res(geom, mod, l, [(c_c, c_s)], [conv_pw2[o]], xa, y, gn2, wrh, wrl, br,
                                        "conv_outproj")
        w1 = moe_w1[l].reshape(MOE_GROUPS, EXPERTS_PER_GROUP, d, f)
        w3 = moe_w3[l].reshape(MOE_GROUPS, EXPERTS_PER_GROUP, d, f)
        w2 = moe_w2[l].reshape(MOE_GROUPS, EXPERTS_PER_GROUP, f, d)
        y = _moe(xr, route, w1, w3, w2, n, d)
        xa = xnew
    y_prompt, y_sample = _final(geom, mod, depth, xa, y, final_g[None, :])
    return (y_prompt.reshape(batch, seq, d), y_sample.reshape(dec_batch, dec_seq, d),
            jnp.stack(new_k, axis=1), jnp.stack(new_v, axis=1), jnp.stack(new_s, axis=1))
```

```python
import functools

import jax
import jax.numpy as jnp
from jax import lax
from jax.experimental import pallas as pl
from jax.experimental.pallas import tpu as pltpu

F32 = jnp.float32
BF16 = jnp.bfloat16

HEAD_DIM = 64
A_HEADS = 8
A_KV_HEADS = 2
A_GROUP = A_HEADS // A_KV_HEADS
B_HEADS = 8
GRID_W = 64
ROPE_THETA = 10000.0
GLA_CHUNK = 16
CONV_WIDTH = 31
MOE_GROUPS = 4
EXPERTS_PER_GROUP = 4
EPS = 1e-6

LANES = 128
SUBLANES = 8
VMEM_LIMIT = 56 * 1024 * 1024
TOKEN_TILE = 512
MOE_TILE = 256
ATTN_Q_TILE = 256
ROUTE_W = LANES
ROUTE_GID_LANE = 8
CONV_ROWS = 64

_NT = (((1,), (1,)), ((), ()))


def _params(n_axes):
    return pltpu.CompilerParams(dimension_semantics=("arbitrary",) * n_axes, vmem_limit_bytes=VMEM_LIMIT)


def _modulate(x, g, sh, sc):
    ms = jnp.mean(x * x, axis=-1, keepdims=True)
    return (x * lax.rsqrt(ms + EPS) * g) * (1.0 + sc) + sh


def _silu(x):
    return x * jax.nn.sigmoid(x)


def _tile_rows_load(ref, n_rows):
    return jnp.concatenate([ref[pl.ds(s, n_rows, stride=SUBLANES), :] for s in range(SUBLANES)], axis=1)


def _tile_rows_store(ref, val):
    n_rows = val.shape[0]
    for s in range(SUBLANES):
        ref[pl.ds(s, n_rows, stride=SUBLANES), :] = val[:, s * LANES:(s + 1) * LANES]


ROUTED_SUB = 2 * SUBLANES


def _pack_routed(hm, route):
    pieces = [hm[:, s * LANES:(s + 1) * LANES] for s in range(SUBLANES)] + [route]
    return pieces + [jnp.zeros(route.shape, F32)] * (ROUTED_SUB - len(pieces))


def _unpack_routed(ref, n_rows):
    chunks = [ref[pl.ds(s, n_rows, stride=ROUTED_SUB), :] for s in range(SUBLANES)]
    route = ref[pl.ds(SUBLANES, n_rows, stride=ROUTED_SUB), :]
    return jnp.concatenate(chunks, axis=1).astype(BF16), route


def _split3(x):
    hi = x.astype(BF16)
    r1 = x - hi.astype(F32)
    mid = r1.astype(BF16)
    lo = (r1 - mid.astype(F32)).astype(BF16)
    return hi, mid, lo


def _ada_body(c_ref, w_ref, b_ref, o_ref):
    s = _silu(c_ref[...])
    o_ref[...] = jnp.dot(s.astype(BF16), w_ref[...].astype(BF16), preferred_element_type=F32) + b_ref[...]


def _ada(cond8, ada_w, ada_b):
    depth, d, d6 = ada_w.shape
    tn = d6 // 4
    return pl.pallas_call(
        _ada_body,
        grid=(depth, d6 // tn),
        in_specs=[
            pl.BlockSpec((8, d), lambda l, j: (0, 0)),
            pl.BlockSpec((None, d, tn), lambda l, j: (l, 0, j)),
            pl.BlockSpec((None, 1, tn), lambda l, j: (l, 0, j)),
        ],
        out_specs=pl.BlockSpec((None, 8, tn), lambda l, j: (l, 0, j)),
        out_shape=jax.ShapeDtypeStruct((depth, 8, d6), F32),
        compiler_params=_params(2),
        name="adaln",
    )(cond8, ada_w, ada_b.reshape(depth, 1, d6))


class _Geom:
    def __init__(self, n_ctx, n_smp, smp_len, d):
        self.n_ctx, self.n_smp, self.smp_len, self.d = n_ctx, n_smp, smp_len, d
        self.n = n_ctx + n_smp
        self.ctx_tiles = n_ctx // TOKEN_TILE

    def mod_spec(self, layer, j):
        ctx_tiles = self.ctx_tiles
        per_smp = self.smp_len // TOKEN_TILE

        def idx(i):
            row = jnp.where(i < ctx_tiles, 0, 1 + (i - ctx_tiles) // per_smp)
            return (layer, row, j, 0, 0)

        return pl.BlockSpec((None, None, None, 1, self.d), idx)

    def tok_specs(self, a):
        tm = TOKEN_TILE
        if isinstance(a, tuple):
            ct = self.ctx_tiles
            w = a[0].shape[1]
            return ([pl.BlockSpec((tm, w), lambda i: (jnp.minimum(i, ct - 1), 0)),
                     pl.BlockSpec((tm, w), lambda i: (jnp.maximum(i - ct, 0), 0))], list(a))
        return [pl.BlockSpec((tm, a.shape[1]), lambda i: (i, 0))], [a]

    def tok_load(self, refs):
        if len(refs) == 1:
            return refs[0][...]
        return jnp.where(pl.program_id(0) < self.ctx_tiles, refs[0][...], refs[1][...])


def _n_tok(a):
    return 2 if isinstance(a, tuple) else 1


def _row_spec(d):
    return pl.BlockSpec((1, d), lambda i: (0, 0))


def _resident(shape):
    nd = len(shape)
    return pl.BlockSpec(shape, lambda *_: (0,) * nd, pipeline_mode=pl.Buffered(1))


def _resident_slab(shape, lead, row_block=None):
    rows = shape[1] if row_block is None else row_block[1]
    rb = 0 if row_block is None else row_block[0]
    return pl.BlockSpec((None, rows) + tuple(shape[2:]), lambda *_: (lead, rb) + (0,) * (len(shape) - 2),
                        pipeline_mode=pl.Buffered(1))


def _full_spec(shape):
    nd = len(shape)
    return pl.BlockSpec(shape, lambda *_: (0,) * nd)


_Y_SPEC = pl.BlockSpec((TOKEN_TILE * SUBLANES, LANES), lambda i: (i, 0))


def _inproj_body(geom, n_x, has_y, n_nat, *refs):
    xs, refs = refs[:n_x], refs[n_x:]
    x = geom.tok_load(xs)
    if has_y:
        y, g2, refs = refs[0], refs[1], refs[2:]
        x = x + g2[...] * _tile_rows_load(y, TOKEN_TILE)
    gn, sh, sc, w, qkv_o, hgt_o, wn_s, wt_s = refs
    step = 512
    n_t = wt_s.shape[0]

    @pl.when(pl.program_id(0) == 0)
    def _():
        wn_s[...] = w[:, :n_nat].astype(BF16)
        for r in range(0, n_t, step):
            wt_s[r:r + step, :] = w[:, n_nat + r:n_nat + r + step].T.astype(BF16)

    h = _modulate(x, gn[...], sh[...], sc[...]).astype(BF16)
    qkv_o[...] = jnp.dot(h, wn_s[...], preferred_element_type=F32)
    for r in range(0, n_t, step):
        hgt_o[r:r + step, :] = lax.dot_general(wt_s[r:r + step, :], h, _NT, preferred_element_type=F32)


def _inproj(geom, mod, layer, xa, y, gn, w_in_all, e, n_nat):
    tm = TOKEN_TILE
    d = geom.d
    n_t = w_in_all.shape[2] - n_nat
    has_y = y is not None
    in_specs, args = geom.tok_specs(xa)
    if has_y:
        in_specs += [_Y_SPEC, geom.mod_spec(layer - 1, 5)]
        args += [y, mod]
    in_specs += [_row_spec(d), geom.mod_spec(layer, 0), geom.mod_spec(layer, 1), _resident_slab(w_in_all.shape, e)]
    args += [gn, mod, mod, w_in_all]
    return pl.pallas_call(
        functools.partial(_inproj_body, geom, _n_tok(xa), has_y, n_nat),
        grid=(geom.n // tm,),
        in_specs=in_specs,
        out_specs=[pl.BlockSpec((tm, n_nat), lambda i: (i, 0)), pl.BlockSpec((n_t, tm), lambda i: (0, i))],
        out_shape=[jax.ShapeDtypeStruct((geom.n, n_nat), F32), jax.ShapeDtypeStruct((n_t, geom.n), F32)],
        scratch_shapes=[pltpu.VMEM((d, n_nat), BF16), pltpu.VMEM((n_t, d), BF16)],
        compiler_params=_params(1),
        name="ab_inproj",
    )(*args)


def _head_norm(x, gain_row, n_heads):
    outs = []
    for h in range(n_heads):
        xh = x[:, h * HEAD_DIM:(h + 1) * HEAD_DIM]
        ms = jnp.mean(xh * xh, axis=-1, keepdims=True)
        outs.append(xh * lax.rsqrt(ms + EPS))
    return jnp.concatenate(outs, axis=1) * gain_row


def _rope(x, cos, sin_signed):
    w = x.shape[1]
    lane = lax.broadcasted_iota(jnp.int32, (1, w), 1)
    first_half = (lane & 16) == 0
    partner = jnp.where(first_half, pltpu.roll(x, w - 16, 1), pltpu.roll(x, 16, 1))
    return x * cos + partner * sin_signed


def _attn_body(rope, tq, *refs):
    if rope:
        q_r, k_r, v_r, ck_r, cv_r, qg_r, kg_r, cq_r, sq_r, ckk_r, skk_r, o_r = refs
    else:
        q_r, k_r, v_r, qg_r, kg_r, o_r, kh_r = refs
    qn = _head_norm(q_r[...], qg_r[...], A_HEADS)
    kn = _head_norm(k_r[...], kg_r[...], A_KV_HEADS)
    if rope:
        qn = _rope(qn, cq_r[...], sq_r[...])
        kn = _rope(kn, ckk_r[...], skk_r[...])
        keys = jnp.concatenate([ck_r[...], kn], axis=0)
        vals = jnp.concatenate([cv_r[...], v_r[...]], axis=0)
    else:
        kh_r[...] = kn
        keys, vals = kn, v_r[...]
    qn = qn * (HEAD_DIM ** -0.5)
    ones = jnp.ones((keys.shape[0], HEAD_DIM), BF16)
    outs = [None] * A_HEADS
    for j in range(A_KV_HEADS):
        kj = keys[:, j * HEAD_DIM:(j + 1) * HEAD_DIM].astype(BF16)
        vj = jnp.concatenate([vals[:, j * HEAD_DIM:(j + 1) * HEAD_DIM].astype(BF16), ones], axis=1)
        heads = [A_GROUP * j + g for g in range(A_GROUP)]
        q4 = jnp.concatenate([qn[:, h * HEAD_DIM:(h + 1) * HEAD_DIM] for h in heads], axis=0).astype(BF16)
        s = lax.dot_general(q4, kj, _NT, preferred_element_type=F32)
        m = jnp.max(s, axis=-1, keepdims=True)
        p = jnp.exp(s - m).astype(BF16)
        ol = jnp.dot(p, vj, preferred_element_type=F32)
        o = ol[:, :HEAD_DIM] / ol[:, HEAD_DIM:HEAD_DIM + 1]
        for g, h in enumerate(heads):
            outs[h] = o[g * tq:(g + 1) * tq, :]
    o_r[...] = jnp.concatenate(outs, axis=1).astype(BF16)


def _attention(qkv, row0, n_batch, t, q_gain, k_gain, cache=None, rope_tabs=None):
    tq = ATTN_Q_TILE
    nq = t // tq
    qw, kw = A_HEADS * HEAD_DIM, A_KV_HEADS * HEAD_DIM
    rope = cache is not None
    q_spec = pl.BlockSpec((tq, qw), lambda b, qi: (row0 // tq + b * nq + qi, 0))
    k_spec = pl.BlockSpec((t, kw), lambda b, qi: (row0 // t + b, qw // kw))
    v_spec = pl.BlockSpec((t, kw), lambda b, qi: (row0 // t + b, qw // kw + 1))
    gq = pl.BlockSpec((1, qw), lambda b, qi: (0, 0))
    gk = pl.BlockSpec((1, kw), lambda b, qi: (0, 0))
    out_o = pl.BlockSpec((tq, qw), lambda b, qi: (b * nq + qi, 0))
    o_shape = jax.ShapeDtypeStruct((n_batch * t, qw), BF16)
    if rope:
        ck, cv, e = cache
        past = ck.shape[2]
        c_spec = pl.BlockSpec((None, None, past, kw), lambda b, qi: (b, e, 0, 0))
        cq, sq, ckk, skk = rope_tabs
        in_specs = [q_spec, k_spec, v_spec, c_spec, c_spec, gq, gk,
                    pl.BlockSpec((tq, qw), lambda b, qi: (qi, 0)), pl.BlockSpec((tq, qw), lambda b, qi: (qi, 0)),
                    pl.BlockSpec((t, kw), lambda b, qi: (0, 0)), pl.BlockSpec((t, kw), lambda b, qi: (0, 0))]
        args = [qkv, qkv, qkv, ck, cv, q_gain, k_gain, cq, sq, ckk, skk]
        out_specs, out_shape = out_o, o_shape
    else:
        in_specs = [q_spec, k_spec, v_spec, gq, gk]
        args = [qkv, qkv, qkv, q_gain, k_gain]
        out_specs = [out_o, pl.BlockSpec((t, kw), lambda b, qi: (b, 0))]
        out_shape = [o_shape, jax.ShapeDtypeStruct((n_batch * t, kw), F32)]
    return pl.pallas_call(
        functools.partial(_attn_body, rope, tq),
        grid=(n_batch, nq),
        in_specs=in_specs,
        out_specs=out_specs,
        out_shape=out_shape,
        compiler_params=_params(2),
        name="gqa_latent" if rope else "gqa_context",
    )(*args)


def _hgrn_body(zero_init, t, *refs):
    if zero_init:
        (q_r, ff_r, fb_r, v_r, g_r, lb_r, gain_r, ob_o, send_o,
         oi_scr, u_scr, dec_scr, qt_scr, sst_scr, on_scr) = refs
        s0_r = None
    else:
        (q_r, ff_r, fb_r, v_r, g_r, lb_r, gain_r, s0_r, ob_o, send_o,
         oi_scr, u_scr, dec_scr, qt_scr, sst_scr, on_scr) = refs
    nt = t // LANES
    hd = HEAD_DIM
    pw = 2 * hd
    cpt = LANES // GLA_CHUNK
    lane = lax.broadcasted_iota(jnp.int32, (1, LANES), 1)
    pos = lane & (GLA_CHUNK - 1)
    ri = lax.broadcasted_iota(jnp.int32, (LANES, LANES), 0)
    ci = lax.broadcasted_iota(jnp.int32, (LANES, LANES), 1)
    same = (ri >> 4) == (ci >> 4)
    same_f = jnp.where(same, 1.0, 0.0)

    def ind(cond):
        return jnp.where(cond, 1.0, 0.0).astype(BF16)

    m_fwd = jnp.concatenate([ind(same & (ri <= ci)), ind(same & (ri > ci))], axis=1)
    m_bwd = jnp.concatenate([ind(same & (ri >= ci)), ind(same & (ri < ci))], axis=1)
    eye = ind(ri == ci)
    sel_r = lax.broadcasted_iota(jnp.int32, (cpt, LANES), 0)
    sel_c = lax.broadcasted_iota(jnp.int32, (cpt, LANES), 1)
    sel = ind((sel_c >> 4) == sel_r)
    bd_r = lax.broadcasted_iota(jnp.int32, (cpt * pw, LANES), 0)
    bd_c = lax.broadcasted_iota(jnp.int32, (cpt * pw, LANES), 1)
    bd = (bd_r >> 7) == (bd_c >> 4)
    top = ((bd_r >> 6) & 1) == 0
    bd_all = ind(bd)
    bd_top = ind(bd & top)
    bd_bot = ind(bd & jnp.logical_not(top))
    u_lane = lax.broadcasted_iota(jnp.int32, (1, cpt * pw), 1)
    u_top = ((u_lane >> 6) & 1) == 0
    gain2 = jnp.concatenate([gain_r[...], gain_r[...]], axis=0)

    def colsum(a):
        return jnp.sum(a, axis=0, keepdims=True)

    def dot(a, b):
        return jnp.dot(a, b, preferred_element_type=F32)

    def pair_step(pr, carry):
        rows = pl.ds(pl.multiple_of(pr * pw, pw), pw)

        def phase_a(ti, c):
            ls = pl.ds(pl.multiple_of(ti * LANES, LANES), LANES)
            q = q_r[rows, ls]
            v = v_r[rows, ls]
            vb = v.astype(BF16)
            skews = [None, None]
            for bwd in (False, True):
                d_idx = int(bwd)
                raw_r = fb_r if bwd else ff_r
                lbv = lb_r[d_idx, rows, :]
                f = lbv + (1.0 - lbv) * jax.nn.sigmoid(raw_r[rows, ls])
                pm = (GLA_CHUNK - 1 - pos) if bwd else pos
                g_prev = [colsum(q[:hd]), colsum(q[hd:])]
                qd = q
                a_rows = [[], []]
                for dl in range(GLA_CHUNK):
                    qd = qd * (pltpu.roll(f, (LANES - dl) if bwd else dl, 1) if dl > 0 else f)
                    g_cur = [colsum(qd[:hd]), colsum(qd[hd:])]
                    for k in range(2):
                        a_rows[k].append(jnp.where(pm >= dl, g_prev[k] - g_cur[k], 0.0))
                    g_prev = g_cur
                for k in range(2):
                    order = a_rows[k] if bwd else [a_rows[k][0]] + a_rows[k][:0:-1]
                    stack = jnp.concatenate(order * cpt, axis=0).astype(BF16)
                    tall = lax.dot_general(eye, stack, _NT, preferred_element_type=F32)
                    sk = pltpu.roll(tall, 0, 1, stride=1, stride_axis=0)
                    skews[k] = sk if skews[k] is None else skews[k] + sk
                hi, mid, lo = _split3(jnp.log(f))
                m_cs = m_bwd if bwd else m_fwd
                cs = dot(hi, m_cs) + dot(mid, m_cs) + dot(lo, m_cs)
                qt_scr[d_idx, :, ls] = q * jnp.exp(cs[:, :LANES])
                kt = (1.0 - f) * jnp.exp(cs[:, LANES:])
                dsum = (lax.dot_general(sel, hi, _NT, preferred_element_type=F32)
                        + lax.dot_general(sel, mid, _NT, preferred_element_type=F32)
                        + lax.dot_general(sel, lo, _NT, preferred_element_type=F32))
                dec_scr[d_idx, ti] = jnp.exp(dsum)
                kbd = jnp.concatenate([kt.astype(BF16)] * cpt, axis=0) * bd_all
                uo = lax.dot_general(vb, kbd, _NT, preferred_element_type=F32)
                u_scr[d_idx, ti] = jnp.where(u_top, uo[:hd], uo[hd:])
            amats = [(sk * same_f).astype(BF16) for sk in skews]
            oi_scr[:, ls] = jnp.concatenate(
                [lax.dot_general(vb[:hd], amats[0], _NT, preferred_element_type=F32),
                 lax.dot_general(vb[hd:], amats[1], _NT, preferred_element_type=F32)], axis=0)
            return c

        lax.fori_loop(0, nt, phase_a, 0, unroll=2)

        def phase_b(ti, st):
            sf, sb = st
            u = u_scr[0, ti]
            dec = dec_scr[0, ti]
            pieces = [None] * cpt
            for c in range(cpt):
                pieces[c] = sf
                sf = dec[c:c + 1, :] * sf + u[:, c * pw:(c + 1) * pw]
            sst_scr[0, ti] = jnp.concatenate(pieces, axis=1).astype(BF16)
            tb = nt - 1 - ti
            u = u_scr[1, tb]
            dec = dec_scr[1, tb]
            for c in range(cpt - 1, -1, -1):
                pieces[c] = sb
                sb = dec[c:c + 1, :] * sb + u[:, c * pw:(c + 1) * pw]
            sst_scr[1, tb] = jnp.concatenate(pieces, axis=1).astype(BF16)
            return sf, sb

        if zero_init:
            s0 = (jnp.zeros((hd, pw), F32), jnp.zeros((hd, pw), F32))
        else:
            s0 = tuple(jnp.concatenate([s0_r[k, 2 * pr], s0_r[k, 2 * pr + 1]], axis=1) for k in range(2))
        s_end = lax.fori_loop(0, nt, phase_b, s0)
        for k in range(2):
            send_o[k, 2 * pr] = s_end[k][:, :hd]
            send_o[k, 2 * pr + 1] = s_end[k][:, hd:]

        def phase_c(ti, c):
            ls = pl.ds(pl.multiple_of(ti * LANES, LANES), LANES)
            o = oi_scr[:, ls]
            for k in range(2):
                t8 = jnp.concatenate([qt_scr[k, :, ls].astype(BF16)] * cpt, axis=0)
                qb = jnp.concatenate([t8 * bd_top, t8 * bd_bot], axis=1)
                oo = dot(sst_scr[k, ti], qb)
                o = o + jnp.concatenate([oo[:, :LANES], oo[:, LANES:]], axis=0)
            o0, o1 = o[:hd], o[hd:]
            on = jnp.concatenate([o0 * lax.rsqrt(jnp.mean(o0 * o0, axis=0, keepdims=True) + EPS),
                                  o1 * lax.rsqrt(jnp.mean(o1 * o1, axis=0, keepdims=True) + EPS)], axis=0)
            on_scr[rows, ls] = on * gain2 * _silu(g_r[rows, ls])
            return c

        lax.fori_loop(0, nt, phase_c, 0)
        return carry

    lax.fori_loop(0, B_HEADS // 2, pair_step, 0)
    ob_o[...] = on_scr[...].T.astype(BF16)


def _hgrn(hgt, col0, n_batch, t, lb_b, gain_b, s0t):
    bw = B_HEADS * HEAD_DIM
    hd, pw = HEAD_DIM, 2 * HEAD_DIM
    nt = t // LANES
    cpt = LANES // GLA_CHUNK
    zero_init = s0t is None
    specs = [pl.BlockSpec((bw, t), functools.partial(lambda r, b: (r, col0 // t + b), r)) for r in range(5)]
    in_specs = specs + [_full_spec(lb_b.shape), _full_spec(gain_b.shape)]
    args = [hgt] * 5 + [lb_b, gain_b]
    st_spec = pl.BlockSpec((None, 2, B_HEADS, HEAD_DIM, HEAD_DIM), lambda b: (b, 0, 0, 0, 0))
    if not zero_init:
        in_specs.append(st_spec)
        args.append(s0t)
    scratch = [pltpu.VMEM((pw, t), F32), pltpu.VMEM((2, nt, hd, cpt * pw), F32), pltpu.VMEM((2, nt, cpt, pw), F32),
               pltpu.VMEM((2, pw, t), F32), pltpu.VMEM((2, nt, hd, cpt * pw), BF16), pltpu.VMEM((bw, t), F32)]
    return pl.pallas_call(
        functools.partial(_hgrn_body, zero_init, t),
        grid=(n_batch,),
        in_specs=in_specs,
        out_specs=[pl.BlockSpec((t, bw), lambda b: (b, 0)), st_spec],
        out_shape=[jax.ShapeDtypeStruct((n_batch * t, bw), BF16),
                   jax.ShapeDtypeStruct((n_batch, 2, B_HEADS, HEAD_DIM, HEAD_DIM), F32)],
        scratch_shapes=scratch,
        compiler_params=_params(1),
        name="hgrn2_context" if zero_init else "hgrn2_latent",
    )(*args)


def _route(logits):
    lane_i = lax.broadcasted_iota(jnp.int32, logits.shape, 1)
    lane = lane_i.astype(F32)
    neg = -jnp.inf
    big = 1e9
    gl = jnp.where(lane_i < MOE_GROUPS, logits, neg)
    gmax = jnp.max(gl, axis=-1, keepdims=True)
    g_val = 1.0 / jnp.sum(jnp.exp(gl - gmax), axis=-1, keepdims=True)
    g_idx = jnp.min(jnp.where(gl == gmax, lane, big), axis=-1, keepdims=True)
    n_e = MOE_GROUPS * EXPERTS_PER_GROUP
    lane_group = ((lane_i - MOE_GROUPS) >> 2).astype(F32)
    in_group = (lane_i >= MOE_GROUPS) & (lane_i < MOE_GROUPS + n_e) & (lane_group == g_idx)
    el = jnp.where(in_group, logits, neg)
    m1 = jnp.max(el, axis=-1, keepdims=True)
    i1 = jnp.min(jnp.where(el == m1, lane, big), axis=-1, keepdims=True)
    el2 = jnp.where(lane == i1, neg, el)
    m2 = jnp.max(el2, axis=-1, keepdims=True)
    i2 = jnp.min(jnp.where(el2 == m2, lane, big), axis=-1, keepdims=True)
    tt = jnp.exp(m2 - m1)
    w1 = g_val / (1.0 + tt)
    w2 = g_val * tt / (1.0 + tt)
    base = MOE_GROUPS + EXPERTS_PER_GROUP * g_idx
    route = jnp.where(lane == i1 - base, w1, 0.0) + jnp.where(lane == i2 - base, w2, 0.0)
    return jnp.where(lane_i == ROUTE_GID_LANE, g_idx, route)


def _proj_res_body(geom, lhs_n, n_x, has_y, d, *refs):
    lhs = []
    for n_a in lhs_n:
        lhs.append(geom.tok_load(refs[:n_a]))
        refs = refs[n_a:]
    ws, refs = refs[:len(lhs_n)], refs[len(lhs_n):]
    xs, refs = refs[:n_x], refs[n_x:]
    x = geom.tok_load(xs)
    if has_y:
        y, g2, refs = refs[0], refs[1], refs[2:]
        x = x + g2[...] * _tile_rows_load(y, TOKEN_TILE)
    g1, gn2, sh2, sc2, wrh, wrl, br, xnew_o, xr_o, route_o = refs[:10]
    wbs = refs[10:]

    @pl.when(pl.program_id(0) == 0)
    def _():
        for w, wb in zip(ws, wbs):
            wb[...] = w[...].astype(BF16)

    acc = jnp.dot(lhs[0], wbs[0][...], preferred_element_type=F32)
    for a, wb in zip(lhs[1:], wbs[1:]):
        acc = acc + jnp.dot(a, wb[...], preferred_element_type=F32)
    xn = x + g1[...] * acc
    xnew_o[...] = xn
    hm = _modulate(xn, gn2[...], sh2[...], sc2[...])
    hi = hm.astype(BF16)
    lo = (hm - hi.astype(F32)).astype(BF16)
    logits = (jnp.dot(hi, wrh[...], preferred_element_type=F32) + jnp.dot(lo, wrh[...], preferred_element_type=F32)
              + jnp.dot(hi, wrl[...], preferred_element_type=F32)) + br[...]
    route = _route(logits)
    for s, piece in enumerate(_pack_routed(hm, route)):
        xr_o[pl.ds(s, TOKEN_TILE, stride=ROUTED_SUB), :] = piece
    route_o[...] = route


def _proj_res(geom, mod, layer, lhs, ws, xa, y, gn2, wrh, wrl, br, name):
    tm = TOKEN_TILE
    d = geom.d
    has_y = y is not None
    tok = pl.BlockSpec((tm, d), lambda i: (i, 0))
    in_specs, args = [], []
    for a in lhs:
        s, ar = geom.tok_specs(a)
        in_specs += s
        args += ar
    w_shapes = []
    for w_all, lead, row_block in ws:
        in_specs.append(_resident_slab(w_all.shape, lead, row_block))
        args.append(w_all)
        w_shapes.append((w_all.shape[1] if row_block is None else row_block[1],) + tuple(w_all.shape[2:]))
    s, ar = geom.tok_specs(xa)
    in_specs += s
    args += ar
    if has_y:
        in_specs += [_Y_SPEC, geom.mod_spec(layer - 1, 5)]
        args += [y, mod]
    in_specs += [geom.mod_spec(layer, 2), _row_spec(d), geom.mod_spec(layer, 3), geom.mod_spec(layer, 4),
                 _full_spec(wrh.shape), _full_spec(wrl.shape), _full_spec(br.shape)]
    args += [mod, gn2, mod, mod, wrh, wrl, br]
    return pl.pallas_call(
        functools.partial(_proj_res_body, geom, tuple(_n_tok(a) for a in lhs), _n_tok(xa), has_y, d),
        grid=(geom.n // tm,),
        in_specs=in_specs,
        out_specs=[tok, pl.BlockSpec((tm * ROUTED_SUB, LANES), lambda i: (i, 0)),
                   pl.BlockSpec((tm, ROUTE_W), lambda i: (i, 0))],
        out_shape=[jax.ShapeDtypeStruct((geom.n, d), F32), jax.ShapeDtypeStruct((geom.n * ROUTED_SUB, LANES), F32),
                   jax.ShapeDtypeStruct((geom.n, ROUTE_W), F32)],
        scratch_shapes=[pltpu.VMEM(shp, BF16) for shp in w_shapes],
        compiler_params=_params(1),
        name=name,
    )(*args)


def _dispatch(route, n, tile):
    n_tiles = n // tile + MOE_GROUPS
    n_slots = n_tiles * tile
    gid = route[:, ROUTE_GID_LANE].astype(jnp.int32)
    onehot = (gid[:, None] == jnp.arange(MOE_GROUPS, dtype=jnp.int32)[None, :]).astype(jnp.int32)
    csum = jnp.cumsum(onehot, axis=0)
    counts = csum[-1]
    rank = jnp.sum(csum * onehot, axis=1) - 1
    tiles_g = (counts + tile - 1) // tile
    tile_end = jnp.cumsum(tiles_g)
    tile_start = tile_end - tiles_g
    dest = (tile_start * tile)[gid] + rank
    tok = jnp.arange(n, dtype=jnp.int32)
    src = jnp.zeros((n_slots,), jnp.int32).at[dest].set(tok)
    t_idx = jnp.arange(n_tiles, dtype=jnp.int32)
    tile_gid = jnp.minimum(jnp.sum((t_idx[:, None] >= tile_end[None, :]).astype(jnp.int32), axis=1), MOE_GROUPS - 1)
    tile_rows = jnp.clip(counts[tile_gid] - (t_idx - tile_start[tile_gid]) * tile, 0, tile)
    tile_rows = jnp.where(t_idx < tile_end[-1], tile_rows, 0)
    in_tile = jnp.arange(tile, dtype=jnp.int32)
    valid = (in_tile[None, :] < tile_rows[:, None]).reshape(n_slots)
    spare = (n + (t_idx[:, None] & 1) * tile + in_tile[None, :]).reshape(n_slots)
    dst = jnp.concatenate([n + tile + in_tile, jnp.where(valid, src, spare)])
    return tile_gid, src, dst, n_tiles


def _moe_body(tile, d, n_rows, tg_ref, src_ref, dst_ref, xr_hbm, w1_ref, w3_ref, w2_ref, y_hbm,
              xbuf, ybuf, w1b, w3b, w2b, isem, osem):
    t = pl.program_id(0)
    last = pl.num_programs(0) - 1
    slot = t % 2
    other = 1 - slot

    sub = SUBLANES
    rsub = ROUTED_SUB

    def hbm_rows(ref, row, n_sub):
        start = row * n_sub if isinstance(row, int) else pl.multiple_of(row * n_sub, n_sub)
        return ref.at[pl.ds(start, n_sub), :]

    def row_in(tile_idx, s, i):
        row = src_ref[tile_idx * tile + i]
        return pltpu.make_async_copy(hbm_rows(xr_hbm, row, rsub), xbuf.at[s, pl.ds(i * rsub, rsub), :], isem.at[s])

    def row_out(ext_tile_idx, s, i):
        row = dst_ref[ext_tile_idx * tile + i]
        return pltpu.make_async_copy(ybuf.at[s, pl.ds(i * sub, sub), :], hbm_rows(y_hbm, row, sub), osem.at[s])

    @pl.when(t == 0)
    def _():
        ybuf[...] = jnp.zeros(ybuf.shape, F32)
        for i in range(tile):
            row_in(0, 0, i).start(priority=i % 2)
        for i in range(tile):
            pltpu.make_async_copy(ybuf.at[0, pl.ds(i * sub, sub), :], hbm_rows(y_hbm, n_rows + i, sub),
                                  osem.at[0]).start(priority=i % 2)

    g = tg_ref[t]
    g_prev = tg_ref[jnp.maximum(t - 1, 0)]

    @pl.when((t == 0) | (g != g_prev))
    def _():
        w1b[...] = w1_ref[...].astype(BF16)
        w3b[...] = w3_ref[...].astype(BF16)
        w2b[...] = w2_ref[...].astype(BF16)

    nxt = jnp.minimum(t + 1, last)
    n_parts = 4 * EXPERTS_PER_GROUP
    per = tile // n_parts

    def step(slot, other):
        for i in range(tile):
            row_in(0, slot, i).wait()
        part = [0]

        def issue_part():
            k = part[0]
            part[0] += 1
            for i in range(k * per, (k + 1) * per):
                row_in(nxt, other, i).start(priority=i % 2)
                row_out(t, other, i).start(priority=i % 2)

        xb, cw = _unpack_routed(xbuf.at[slot], tile)
        acc = jnp.zeros((tile, d), F32)
        for e in range(EXPERTS_PER_GROUP):
            a = jnp.dot(xb, w1b[e], preferred_element_type=F32)
            issue_part()
            b = jnp.dot(xb, w3b[e], preferred_element_type=F32)
            issue_part()
            hid = (_silu(a) * b * cw[:, e:e + 1]).astype(BF16)
            issue_part()
            acc = acc + jnp.dot(hid, w2b[e], preferred_element_type=F32)
            issue_part()
        for i in range(tile):
            row_out(0, slot, i).wait()
        _tile_rows_store(ybuf.at[slot], acc)

        @pl.when(t == last)
        def _():
            for i in range(tile):
                row_in(0, other, i).wait()
            for i in range(tile):
                row_out(0, other, i).wait()
            for i in range(tile):
                row_out(t + 1, slot, i).start(priority=i % 2)
            for i in range(tile):
                row_out(0, slot, i).wait()

    @pl.when(slot == 0)
    def _():
        step(0, 1)

    @pl.when(slot == 1)
    def _():
        step(1, 0)


def _moe(xr, route, w1, w3, w2, layer, n, d):
    tile = MOE_TILE
    assert d == SUBLANES * LANES
    tile_gid, src, dst, n_tiles = _dispatch(route, n, tile)
    g0 = layer * MOE_GROUPS
    wspec13 = pl.BlockSpec((None,) + w1.shape[1:], lambda t, tg, s, ds_: (g0 + tg[t], 0, 0, 0))
    wspec2 = pl.BlockSpec((None,) + w2.shape[1:], lambda t, tg, s, ds_: (g0 + tg[t], 0, 0, 0))
    grid_spec = pltpu.PrefetchScalarGridSpec(
        num_scalar_prefetch=3,
        grid=(n_tiles,),
        in_specs=[pl.BlockSpec(memory_space=pl.ANY), wspec13, wspec13, wspec2],
        out_specs=pl.BlockSpec(memory_space=pl.ANY),
        scratch_shapes=[pltpu.VMEM((2, tile * ROUTED_SUB, LANES), F32), pltpu.VMEM((2, tile * SUBLANES, LANES), F32),
                        pltpu.VMEM(w1.shape[1:], BF16), pltpu.VMEM(w3.shape[1:], BF16), pltpu.VMEM(w2.shape[1:], BF16),
                        pltpu.SemaphoreType.DMA((2,)), pltpu.SemaphoreType.DMA((2,))],
    )
    return pl.pallas_call(
        functools.partial(_moe_body, tile, d, n),
        grid_spec=grid_spec,
        out_shape=jax.ShapeDtypeStruct(((n + 2 * tile) * SUBLANES, LANES), F32),
        compiler_params=_params(1),
        name="moe_ffn",
    )(tile_gid, src, dst, xr, w1, w3, w2)


def _glu_body(xa, y, g2, gn, sh, sc, w, u_o, wb):
    @pl.when(pl.program_id(0) == 0)
    def _():
        wb[...] = w[...].astype(BF16)

    x = xa[...] + g2[...] * _tile_rows_load(y, TOKEN_TILE)
    h = _modulate(x, gn[...], sh[...], sc[...]).astype(BF16)
    ab = jnp.dot(h, wb[...], preferred_element_type=F32)
    c = ab.shape[1] // 2
    u_o[...] = ab[:, :c] * jax.nn.sigmoid(ab[:, c:])


def _glu(geom, mod, layer, xa, y, gn, w_all, o):
    tm = TOKEN_TILE
    d = geom.d
    tok = pl.BlockSpec((tm, d), lambda i: (i, 0))
    c = w_all.shape[2] // 2
    return pl.pallas_call(
        _glu_body,
        grid=(geom.n // tm,),
        in_specs=[tok, _Y_SPEC, geom.mod_spec(layer - 1, 5), _row_spec(d), geom.mod_spec(layer, 0),
                  geom.mod_spec(layer, 1), _resident_slab(w_all.shape, o)],
        out_specs=pl.BlockSpec((tm, c), lambda i: (i, 0)),
        out_shape=jax.ShapeDtypeStruct((geom.n, c), F32),
        scratch_shapes=[pltpu.VMEM(w_all.shape[1:], BF16)],
        compiler_params=_params(1),
        name="conv_glu",
    )(xa, y, mod, gn, mod, mod, w_all)


def _dwconv_body(t, u_r, dw_r, b_r, g_r, be_r, o_r, pad_scr, cv_scr, sh_scr):
    c = u_r.shape[1]
    halo = 16
    sh_len = sh_scr.shape[1]
    pad_scr[0:halo, :] = jnp.zeros((halo, c), F32)
    pad_scr[t + halo:t + 2 * halo, :] = jnp.zeros((halo, c), F32)
    pad_scr[halo:t + halo, :] = u_r[...]
    rb_n = CONV_ROWS
    n_lb = c // LANES
    first = halo - CONV_WIDTH // 2

    def blk(lb, carry):
        ls = pl.ds(pl.multiple_of(lb * LANES, LANES), LANES)
        bias = b_r[:, ls]
        for r0 in range(0, t, rb_n):
            win = pad_scr[r0:r0 + rb_n + 2 * halo, ls]
            for sft in range(1, SUBLANES):
                sh_scr[sft] = win[sft:sft + sh_len, :]
            acc = jnp.zeros((rb_n, LANES), F32)
            for k in range(CONV_WIDTH):
                sft = (first + k) % SUBLANES
                a = first + k - sft
                src = pad_scr[r0 + a:r0 + a + rb_n, ls] if sft == 0 else sh_scr[sft, a:a + rb_n, :]
                acc = acc + src * dw_r[k:k + 1, ls]
            cv_scr[r0:r0 + rb_n, ls] = acc + bias
        return carry

    lax.fori_loop(0, n_lb, blk, 0)

    def ln(rb, carry):
        rs = pl.ds(pl.multiple_of(rb * rb_n, rb_n), rb_n)
        x = cv_scr[rs, :]
        mu = jnp.mean(x, axis=-1, keepdims=True)
        xc = x - mu
        var = jnp.mean(xc * xc, axis=-1, keepdims=True)
        yv = xc * lax.rsqrt(var + EPS) * g_r[...] + be_r[...]
        o_r[rs, :] = _silu(yv).astype(BF16)
        return carry

    lax.fori_loop(0, t // rb_n, ln, 0)


def _dwconv(u, row0, n_batch, t, dw, b, g, be):
    c = u.shape[1]
    return pl.pallas_call(
        functools.partial(_dwconv_body, t),
        grid=(n_batch,),
        in_specs=[pl.BlockSpec((t, c), lambda bi: (row0 // t + bi, 0)), _full_spec(dw.shape),
                  _row_spec(c), _row_spec(c), _row_spec(c)],
        out_specs=pl.BlockSpec((t, c), lambda bi: (bi, 0)),
        out_shape=jax.ShapeDtypeStruct((n_batch * t, c), BF16),
        scratch_shapes=[pltpu.VMEM((t + 32, c), F32), pltpu.VMEM((t, c), F32),
                        pltpu.VMEM((SUBLANES, CONV_ROWS + (CONV_WIDTH // SUBLANES) * SUBLANES, LANES), F32)],
        compiler_params=_params(1),
        name="conv_dw",
    )(u, dw, b, g, be)


def _final_body(ctx_tiles, xa, y, g2, gf, oc, os_):
    x = xa[...] + g2[...] * _tile_rows_load(y, TOKEN_TILE)
    ms = jnp.mean(x * x, axis=-1, keepdims=True)
    val = x * lax.rsqrt(ms + EPS) * gf[...]
    i = pl.program_id(0)

    @pl.when(i < ctx_tiles)
    def _():
        oc[...] = val

    @pl.when(i >= ctx_tiles)
    def _():
        os_[...] = val


def _final(geom, mod, depth, xa, y, gf):
    tm = TOKEN_TILE
    d = geom.d
    ct = geom.ctx_tiles
    tok = pl.BlockSpec((tm, d), lambda i: (i, 0))
    return pl.pallas_call(
        functools.partial(_final_body, ct),
        grid=(geom.n // tm,),
        in_specs=[tok, _Y_SPEC, geom.mod_spec(depth - 1, 5), _row_spec(d)],
        out_specs=[pl.BlockSpec((tm, d), lambda i: (jnp.minimum(i, ct - 1), 0)),
                   pl.BlockSpec((tm, d), lambda i: (jnp.maximum(i - ct, 0), 0))],
        out_shape=[jax.ShapeDtypeStruct((geom.n_ctx, d), F32), jax.ShapeDtypeStruct((geom.n_smp, d), F32)],
        compiler_params=_params(1),
        name="final_norm",
    )(xa, y, mod, gf)


def _rope_tables(n_tokens):
    rows = n_tokens // GRID_W
    row = jnp.repeat(jnp.arange(rows, dtype=F32), GRID_W)
    col = jnp.tile(jnp.arange(GRID_W, dtype=F32), rows)
    half = HEAD_DIM // 2
    inv = ROPE_THETA ** (-jnp.arange(0, half, 2, dtype=F32) / half)
    ang_r = row[:, None] * inv[None, :]
    ang_c = col[:, None] * inv[None, :]
    cr, sr, cc, sc = jnp.cos(ang_r), jnp.sin(ang_r), jnp.cos(ang_c), jnp.sin(ang_c)
    cos = jnp.concatenate([cr, cr, cc, cc], axis=1)
    sin = jnp.concatenate([-sr, sr, -sc, sc], axis=1)
    return (jnp.tile(cos, (1, A_HEADS)), jnp.tile(sin, (1, A_HEADS)),
            jnp.tile(cos, (1, A_KV_HEADS)), jnp.tile(sin, (1, A_KV_HEADS)))


def _router_weights(w_group, b_group, w_expert, b_expert):
    d = w_group.shape[0]
    n_used = MOE_GROUPS + MOE_GROUPS * EXPERTS_PER_GROUP
    w = jnp.concatenate([w_group, w_expert, jnp.zeros((d, ROUTE_W - n_used), F32)], axis=1)
    b = jnp.concatenate([b_group, b_expert, jnp.zeros((ROUTE_W - n_used,), F32)])[None, :]
    hi = w.astype(BF16)
    lo = (w - hi.astype(F32)).astype(BF16)
    return hi, lo, b


def kernel(x_prompt, x_sample, cache_k, cache_v, state_hgrn, c, c_ctx, ada_w, ada_b, norm1_g, norm2_g, mix_w_in, mix_w_out, attn_q_gain, attn_k_gain, hgrn_lower_bound, hgrn_out_gain, conv_pw1, conv_dw, conv_dw_b, conv_ln_g, conv_ln_b, conv_pw2, moe_w_group, moe_b_group, moe_w_expert, moe_b_expert, moe_w1, moe_w3, moe_w2, final_g):
    batch, seq, d = x_prompt.shape
    dec_batch, dec_seq, _ = x_sample.shape
    depth = ada_w.shape[0]
    past = cache_k.shape[2]
    n_ctx, n_smp = batch * seq, dec_batch * dec_seq
    n = n_ctx + n_smp
    geom = _Geom(n_ctx, n_smp, dec_seq, d)
    qw, kw, bw = A_HEADS * HEAD_DIM, A_KV_HEADS * HEAD_DIM, B_HEADS * HEAD_DIM
    n_nat = qw + 2 * kw

    cond8 = jnp.concatenate([c_ctx[None, :], c, jnp.zeros((8 - 1 - dec_batch, d), F32)], axis=0)
    mod = _ada(cond8, ada_w, ada_b).reshape(depth, 8, 6, 1, d)

    lb_soft = jax.nn.softmax(hgrn_lower_bound.astype(F32), axis=0)
    lower_bounds = jnp.cumsum(lb_soft, axis=0) - lb_soft[0]
    rope_tabs = _rope_tables(dec_seq)

    xa = (x_prompt.reshape(n_ctx, d), x_sample.reshape(n_smp, d))
    y = None
    f = moe_w1.shape[-1]
    w1 = moe_w1.reshape(depth * MOE_GROUPS, EXPERTS_PER_GROUP, d, f)
    w3 = moe_w3.reshape(depth * MOE_GROUPS, EXPERTS_PER_GROUP, d, f)
    w2 = moe_w2.reshape(depth * MOE_GROUPS, EXPERTS_PER_GROUP, f, d)
    n_ab = cache_k.shape[1]
    ck = cache_k.reshape(dec_batch, n_ab, past, kw)
    cv = cache_v.reshape(dec_batch, n_ab, past, kw)
    new_k, new_v, new_s = [], [], []
    for l in range(depth):
        wrh, wrl, br = _router_weights(moe_w_group[l], moe_b_group[l], moe_w_expert[l], moe_b_expert[l])
        gn1 = norm1_g[l][None, :]
        gn2 = norm2_g[l][None, :]
        if l % 2 == 0:
            e = l // 2
            qkv, hgt = _inproj(geom, mod, l, xa, y, gn1, mix_w_in, e, n_nat)
            qg = jnp.tile(attn_q_gain[e], A_HEADS)[None, :]
            kg = jnp.tile(attn_k_gain[e], A_KV_HEADS)[None, :]
            oa_c, khat = _attention(qkv, 0, batch, seq, qg, kg)
            oa_s = _attention(qkv, n_ctx, dec_batch, dec_seq, qg, kg, cache=(ck, cv, e), rope_tabs=rope_tabs)
            lb_b = jnp.broadcast_to(lower_bounds[e][:, :, None], (2, bw, LANES))
            gain_b = jnp.broadcast_to(hgrn_out_gain[e][:, None], (HEAD_DIM, LANES))
            ob_c, s_ctx = _hgrn(hgt, 0, batch, seq, lb_b, gain_b, None)
            s0t = jnp.swapaxes(state_hgrn[:, e], -1, -2)
            ob_s, _ = _hgrn(hgt, n_ctx, dec_batch, dec_seq, lb_b, gain_b, s0t)
            w_halves = [(mix_w_out, e, (0, qw)), (mix_w_out, e, (1, bw))]
            xnew, xr, route = _proj_res(geom, mod, l, [(oa_c, oa_s), (ob_c, ob_s)], w_halves, xa, y,
                                        gn2, wrh, wrl, br, "ab_outproj")
            new_k.append(khat.reshape(batch, seq, A_KV_HEADS, HEAD_DIM))
            new_v.append(qkv[:n_ctx, qw + kw:].reshape(batch, seq, A_KV_HEADS, HEAD_DIM))
            new_s.append(jnp.swapaxes(s_ctx, -1, -2))
        else:
            o = l // 2
            u = _glu(geom, mod, l, xa, y, gn1, conv_pw1, o)
            cargs = (conv_dw[o], conv_dw_b[o][None, :], conv_ln_g[o][None, :], conv_ln_b[o][None, :])
            c_c = _dwconv(u, 0, batch, seq, *cargs)
            c_s = _dwconv(u, n_ctx, dec_batch, dec_seq, *cargs)
            xnew, xr, route = _proj_res(geom, mod, l, [(c_c, c_s)], [(conv_pw2, o, None)], xa, y, gn2, wrh, wrl, br,
                                        "conv_outproj")
        y = _moe(xr, route, w1, w3, w2, l, n, d)
        xa = xnew
    y_prompt, y_sample = _final(geom, mod, depth, xa, y, final_g[None, :])
    return (y_prompt.reshape(batch, seq, d), y_sample.reshape(dec_batch, dec_seq, d),
            jnp.stack(new_k, axis=1), jnp.stack(new_v, axis=1), jnp.stack(new_s, axis=1))
```

```python
import functools

import jax
import jax.numpy as jnp
from jax import lax
from jax.experimental import pallas as pl
from jax.experimental.pallas import tpu as pltpu

F32 = jnp.float32
BF16 = jnp.bfloat16

HEAD_DIM = 64
A_HEADS = 8
A_KV_HEADS = 2
A_GROUP = A_HEADS // A_KV_HEADS
B_HEADS = 8
GRID_W = 64
ROPE_THETA = 10000.0
GLA_CHUNK = 16
CONV_WIDTH = 31
MOE_GROUPS = 4
EXPERTS_PER_GROUP = 4
EPS = 1e-6

LANES = 128
SUBLANES = 8
VMEM_LIMIT = 56 * 1024 * 1024
TOKEN_TILE = 512
MOE_TILE = 256
ATTN_Q_TILE = 256
ROUTE_W = LANES
ROUTE_GID_LANE = 8
CONV_ROWS = 64

_NT = (((1,), (1,)), ((), ()))


def _params(n_axes):
    return pltpu.CompilerParams(dimension_semantics=("arbitrary",) * n_axes, vmem_limit_bytes=VMEM_LIMIT)


def _modulate(x, g, sh, sc):
    ms = jnp.mean(x * x, axis=-1, keepdims=True)
    return (x * lax.rsqrt(ms + EPS) * g) * (1.0 + sc) + sh


def _silu(x):
    return x * jax.nn.sigmoid(x)


def _tile_rows_load(ref, n_rows):
    return jnp.concatenate([ref[pl.ds(s, n_rows, stride=SUBLANES), :] for s in range(SUBLANES)], axis=1)


def _tile_rows_store(ref, val):
    n_rows = val.shape[0]
    for s in range(SUBLANES):
        ref[pl.ds(s, n_rows, stride=SUBLANES), :] = val[:, s * LANES:(s + 1) * LANES]


ROUTED_SUB = 2 * SUBLANES


def _pack_routed(hm, route):
    pieces = [hm[:, s * LANES:(s + 1) * LANES] for s in range(SUBLANES)] + [route]
    return pieces + [jnp.zeros(route.shape, F32)] * (ROUTED_SUB - len(pieces))


def _unpack_routed(ref, n_rows):
    chunks = [ref[pl.ds(s, n_rows, stride=ROUTED_SUB), :] for s in range(SUBLANES)]
    route = ref[pl.ds(SUBLANES, n_rows, stride=ROUTED_SUB), :]
    return jnp.concatenate(chunks, axis=1).astype(BF16), route


def _split3(x):
    hi = x.astype(BF16)
    r1 = x - hi.astype(F32)
    mid = r1.astype(BF16)
    lo = (r1 - mid.astype(F32)).astype(BF16)
    return hi, mid, lo


def _ada_body(c_ref, w_ref, b_ref, o_ref):
    s = _silu(c_ref[...])
    o_ref[...] = jnp.dot(s.astype(BF16), w_ref[...].astype(BF16), preferred_element_type=F32) + b_ref[...]


def _ada(cond8, ada_w, ada_b):
    depth, d, d6 = ada_w.shape
    tn = d6 // 4
    return pl.pallas_call(
        _ada_body,
        grid=(depth, d6 // tn),
        in_specs=[
            pl.BlockSpec((8, d), lambda l, j: (0, 0)),
            pl.BlockSpec((None, d, tn), lambda l, j: (l, 0, j)),
            pl.BlockSpec((None, 1, tn), lambda l, j: (l, 0, j)),
        ],
        out_specs=pl.BlockSpec((None, 8, tn), lambda l, j: (l, 0, j)),
        out_shape=jax.ShapeDtypeStruct((depth, 8, d6), F32),
        compiler_params=_params(2),
        name="adaln",
    )(cond8, ada_w, ada_b.reshape(depth, 1, d6))


class _Geom:
    def __init__(self, n_ctx, n_smp, smp_len, d):
        self.n_ctx, self.n_smp, self.smp_len, self.d = n_ctx, n_smp, smp_len, d
        self.n = n_ctx + n_smp
        self.ctx_tiles = n_ctx // TOKEN_TILE

    def mod_spec(self, layer, j):
        ctx_tiles = self.ctx_tiles
        per_smp = self.smp_len // TOKEN_TILE

        def idx(i):
            row = jnp.where(i < ctx_tiles, 0, 1 + (i - ctx_tiles) // per_smp)
            return (layer, row, j, 0, 0)

        return pl.BlockSpec((None, None, None, 1, self.d), idx)

    def tok_specs(self, a):
        tm = TOKEN_TILE
        if isinstance(a, tuple):
            ct = self.ctx_tiles
            w = a[0].shape[1]
            return ([pl.BlockSpec((tm, w), lambda i: (jnp.minimum(i, ct - 1), 0)),
                     pl.BlockSpec((tm, w), lambda i: (jnp.maximum(i - ct, 0), 0))], list(a))
        return [pl.BlockSpec((tm, a.shape[1]), lambda i: (i, 0))], [a]

    def tok_load(self, refs):
        if len(refs) == 1:
            return refs[0][...]
        return jnp.where(pl.program_id(0) < self.ctx_tiles, refs[0][...], refs[1][...])


def _n_tok(a):
    return 2 if isinstance(a, tuple) else 1


def _row_spec(d):
    return pl.BlockSpec((1, d), lambda i: (0, 0))


def _resident(shape):
    nd = len(shape)
    return pl.BlockSpec(shape, lambda *_: (0,) * nd, pipeline_mode=pl.Buffered(1))


def _resident_slab(shape, lead, row_block=None):
    rows = shape[1] if row_block is None else row_block[1]
    rb = 0 if row_block is None else row_block[0]
    return pl.BlockSpec((None, rows) + tuple(shape[2:]), lambda *_: (lead, rb) + (0,) * (len(shape) - 2),
                        pipeline_mode=pl.Buffered(1))


def _full_spec(shape):
    nd = len(shape)
    return pl.BlockSpec(shape, lambda *_: (0,) * nd)


_Y_SPEC = pl.BlockSpec((TOKEN_TILE * SUBLANES, LANES), lambda i: (i, 0))


def _inproj_body(geom, n_x, has_y, n_nat, *refs):
    xs, refs = refs[:n_x], refs[n_x:]
    x = geom.tok_load(xs)
    if has_y:
        y, g2, refs = refs[0], refs[1], refs[2:]
        x = x + g2[...] * _tile_rows_load(y, TOKEN_TILE)
    gn, sh, sc, w, qkv_o, hgt_o, wn_s, wt_s = refs
    step = 512
    n_t = wt_s.shape[0]

    @pl.when(pl.program_id(0) == 0)
    def _():
        wn_s[...] = w[:, :n_nat].astype(BF16)
        for r in range(0, n_t, step):
            wt_s[r:r + step, :] = w[:, n_nat + r:n_nat + r + step].T.astype(BF16)

    h = _modulate(x, gn[...], sh[...], sc[...]).astype(BF16)
    qkv_o[...] = jnp.dot(h, wn_s[...], preferred_element_type=F32)
    for r in range(0, n_t, step):
        hgt_o[r:r + step, :] = lax.dot_general(wt_s[r:r + step, :], h, _NT, preferred_element_type=F32)


def _inproj(geom, mod, layer, xa, y, gn, w_in_all, e, n_nat):
    tm = TOKEN_TILE
    d = geom.d
    n_t = w_in_all.shape[2] - n_nat
    has_y = y is not None
    in_specs, args = geom.tok_specs(xa)
    if has_y:
        in_specs += [_Y_SPEC, geom.mod_spec(layer - 1, 5)]
        args += [y, mod]
    in_specs += [_row_spec(d), geom.mod_spec(layer, 0), geom.mod_spec(layer, 1), _resident_slab(w_in_all.shape, e)]
    args += [gn, mod, mod, w_in_all]
    return pl.pallas_call(
        functools.partial(_inproj_body, geom, _n_tok(xa), has_y, n_nat),
        grid=(geom.n // tm,),
        in_specs=in_specs,
        out_specs=[pl.BlockSpec((tm, n_nat), lambda i: (i, 0)), pl.BlockSpec((n_t, tm), lambda i: (0, i))],
        out_shape=[jax.ShapeDtypeStruct((geom.n, n_nat), F32), jax.ShapeDtypeStruct((n_t, geom.n), F32)],
        scratch_shapes=[pltpu.VMEM((d, n_nat), BF16), pltpu.VMEM((n_t, d), BF16)],
        compiler_params=_params(1),
        name="ab_inproj",
    )(*args)


def _head_norm(x, gain_row, n_heads):
    outs = []
    for h in range(n_heads):
        xh = x[:, h * HEAD_DIM:(h + 1) * HEAD_DIM]
        ms = jnp.mean(xh * xh, axis=-1, keepdims=True)
        outs.append(xh * lax.rsqrt(ms + EPS))
    return jnp.concatenate(outs, axis=1) * gain_row


def _rope(x, cos, sin_signed):
    w = x.shape[1]
    lane = lax.broadcasted_iota(jnp.int32, (1, w), 1)
    first_half = (lane & 16) == 0
    partner = jnp.where(first_half, pltpu.roll(x, w - 16, 1), pltpu.roll(x, 16, 1))
    return x * cos + partner * sin_signed


def _attn_body(rope, tq, *refs):
    if rope:
        q_r, k_r, v_r, ck_r, cv_r, qg_r, kg_r, cq_r, sq_r, ckk_r, skk_r, o_r = refs
    else:
        q_r, k_r, v_r, qg_r, kg_r, o_r, kh_r = refs
    qn = _head_norm(q_r[...], qg_r[...], A_HEADS)
    kn = _head_norm(k_r[...], kg_r[...], A_KV_HEADS)
    if rope:
        qn = _rope(qn, cq_r[...], sq_r[...])
        kn = _rope(kn, ckk_r[...], skk_r[...])
        keys = jnp.concatenate([ck_r[...], kn], axis=0)
        vals = jnp.concatenate([cv_r[...], v_r[...]], axis=0)
    else:
        kh_r[...] = kn
        keys, vals = kn, v_r[...]
    qn = qn * (HEAD_DIM ** -0.5)
    ones = jnp.ones((keys.shape[0], HEAD_DIM), BF16)
    outs = [None] * A_HEADS
    for j in range(A_KV_HEADS):
        kj = keys[:, j * HEAD_DIM:(j + 1) * HEAD_DIM].astype(BF16)
        vj = jnp.concatenate([vals[:, j * HEAD_DIM:(j + 1) * HEAD_DIM].astype(BF16), ones], axis=1)
        heads = [A_GROUP * j + g for g in range(A_GROUP)]
        q4 = jnp.concatenate([qn[:, h * HEAD_DIM:(h + 1) * HEAD_DIM] for h in heads], axis=0).astype(BF16)
        s = lax.dot_general(q4, kj, _NT, preferred_element_type=F32)
        m = jnp.max(s, axis=-1, keepdims=True)
        p = jnp.exp(s - m).astype(BF16)
        ol = jnp.dot(p, vj, preferred_element_type=F32)
        o = ol[:, :HEAD_DIM] / ol[:, HEAD_DIM:HEAD_DIM + 1]
        for g, h in enumerate(heads):
            outs[h] = o[g * tq:(g + 1) * tq, :]
    o_r[...] = jnp.concatenate(outs, axis=1).astype(BF16)


def _attention(qkv, row0, n_batch, t, q_gain, k_gain, cache=None, rope_tabs=None):
    tq = ATTN_Q_TILE
    nq = t // tq
    qw, kw = A_HEADS * HEAD_DIM, A_KV_HEADS * HEAD_DIM
    rope = cache is not None
    q_spec = pl.BlockSpec((tq, qw), lambda b, qi: (row0 // tq + b * nq + qi, 0))
    k_spec = pl.BlockSpec((t, kw), lambda b, qi: (row0 // t + b, qw // kw))
    v_spec = pl.BlockSpec((t, kw), lambda b, qi: (row0 // t + b, qw // kw + 1))
    gq = pl.BlockSpec((1, qw), lambda b, qi: (0, 0))
    gk = pl.BlockSpec((1, kw), lambda b, qi: (0, 0))
    out_o = pl.BlockSpec((tq, qw), lambda b, qi: (b * nq + qi, 0))
    o_shape = jax.ShapeDtypeStruct((n_batch * t, qw), BF16)
    if rope:
        ck, cv, e = cache
        past = ck.shape[2]
        c_spec = pl.BlockSpec((None, None, past, kw), lambda b, qi: (b, e, 0, 0))
        cq, sq, ckk, skk = rope_tabs
        in_specs = [q_spec, k_spec, v_spec, c_spec, c_spec, gq, gk,
                    pl.BlockSpec((tq, qw), lambda b, qi: (qi, 0)), pl.BlockSpec((tq, qw), lambda b, qi: (qi, 0)),
                    pl.BlockSpec((t, kw), lambda b, qi: (0, 0)), pl.BlockSpec((t, kw), lambda b, qi: (0, 0))]
        args = [qkv, qkv, qkv, ck, cv, q_gain, k_gain, cq, sq, ckk, skk]
        out_specs, out_shape = out_o, o_shape
    else:
        in_specs = [q_spec, k_spec, v_spec, gq, gk]
        args = [qkv, qkv, qkv, q_gain, k_gain]
        out_specs = [out_o, pl.BlockSpec((t, kw), lambda b, qi: (b, 0))]
        out_shape = [o_shape, jax.ShapeDtypeStruct((n_batch * t, kw), F32)]
    return pl.pallas_call(
        functools.partial(_attn_body, rope, tq),
        grid=(n_batch, nq),
        in_specs=in_specs,
        out_specs=out_specs,
        out_shape=out_shape,
        compiler_params=_params(2),
        name="gqa_latent" if rope else "gqa_context",
    )(*args)


def _hgrn_body(zero_init, t, *refs):
    if zero_init:
        (q_r, ff_r, fb_r, v_r, g_r, lb_r, gain_r, ob_o, send_o,
         oi_scr, u_scr, dec_scr, qt_scr, sst_scr, on_scr) = refs
        s0_r = None
    else:
        (q_r, ff_r, fb_r, v_r, g_r, lb_r, gain_r, s0_r, ob_o, send_o,
         oi_scr, u_scr, dec_scr, qt_scr, sst_scr, on_scr) = refs
    nt = t // LANES
    hd = HEAD_DIM
    pw = 2 * hd
    cpt = LANES // GLA_CHUNK
    lane = lax.broadcasted_iota(jnp.int32, (1, LANES), 1)
    pos = lane & (GLA_CHUNK - 1)
    ri = lax.broadcasted_iota(jnp.int32, (LANES, LANES), 0)
    ci = lax.broadcasted_iota(jnp.int32, (LANES, LANES), 1)
    same = (ri >> 4) == (ci >> 4)
    same_f = jnp.where(same, 1.0, 0.0)

    def ind(cond):
        return jnp.where(cond, 1.0, 0.0).astype(BF16)

    m_fwd = jnp.concatenate([ind(same & (ri <= ci)), ind(same & (ri > ci))], axis=1)
    m_bwd = jnp.concatenate([ind(same & (ri >= ci)), ind(same & (ri < ci))], axis=1)
    eye = ind(ri == ci)
    sel_r = lax.broadcasted_iota(jnp.int32, (cpt, LANES), 0)
    sel_c = lax.broadcasted_iota(jnp.int32, (cpt, LANES), 1)
    sel = ind((sel_c >> 4) == sel_r)
    bd_r = lax.broadcasted_iota(jnp.int32, (cpt * pw, LANES), 0)
    bd_c = lax.broadcasted_iota(jnp.int32, (cpt * pw, LANES), 1)
    bd = (bd_r >> 7) == (bd_c >> 4)
    top = ((bd_r >> 6) & 1) == 0
    bd_all = ind(bd)
    bd_top = ind(bd & top)
    bd_bot = ind(bd & jnp.logical_not(top))
    u_lane = lax.broadcasted_iota(jnp.int32, (1, cpt * pw), 1)
    u_top = ((u_lane >> 6) & 1) == 0
    gain2 = jnp.concatenate([gain_r[...], gain_r[...]], axis=0)

    def colsum(a):
        return jnp.sum(a, axis=0, keepdims=True)

    def dot(a, b):
        return jnp.dot(a, b, preferred_element_type=F32)

    def pair_step(pr, carry):
        rows = pl.ds(pl.multiple_of(pr * pw, pw), pw)

        def phase_a(ti, c):
            ls = pl.ds(pl.multiple_of(ti * LANES, LANES), LANES)
            q = q_r[rows, ls]
            v = v_r[rows, ls]
            vb = v.astype(BF16)
            skews = [None, None]
            for bwd in (False, True):
                d_idx = int(bwd)
                raw_r = fb_r if bwd else ff_r
                lbv = lb_r[d_idx, rows, :]
                f = lbv + (1.0 - lbv) * jax.nn.sigmoid(raw_r[rows, ls])
                pm = (GLA_CHUNK - 1 - pos) if bwd else pos
                g_prev = [colsum(q[:hd]), colsum(q[hd:])]
                qd = q
                a_rows = [[], []]
                for dl in range(GLA_CHUNK):
                    qd = qd * (pltpu.roll(f, (LANES - dl) if bwd else dl, 1) if dl > 0 else f)
                    g_cur = [colsum(qd[:hd]), colsum(qd[hd:])]
                    for k in range(2):
                        a_rows[k].append(jnp.where(pm >= dl, g_prev[k] - g_cur[k], 0.0))
                    g_prev = g_cur
                for k in range(2):
                    order = a_rows[k] if bwd else [a_rows[k][0]] + a_rows[k][:0:-1]
                    stack = jnp.concatenate(order * cpt, axis=0).astype(BF16)
                    tall = lax.dot_general(eye, stack, _NT, preferred_element_type=F32)
                    sk = pltpu.roll(tall, 0, 1, stride=1, stride_axis=0)
                    skews[k] = sk if skews[k] is None else skews[k] + sk
                hi, mid, lo = _split3(jnp.log(f))
                m_cs = m_bwd if bwd else m_fwd
                cs = dot(hi, m_cs) + dot(mid, m_cs) + dot(lo, m_cs)
                qt_scr[d_idx, :, ls] = q * jnp.exp(cs[:, :LANES])
                kt = (1.0 - f) * jnp.exp(cs[:, LANES:])
                dsum = (lax.dot_general(sel, hi, _NT, preferred_element_type=F32)
                        + lax.dot_general(sel, mid, _NT, preferred_element_type=F32)
                        + lax.dot_general(sel, lo, _NT, preferred_element_type=F32))
                dec_scr[d_idx, ti] = jnp.exp(dsum)
                kbd = jnp.concatenate([kt.astype(BF16)] * cpt, axis=0) * bd_all
                uo = lax.dot_general(vb, kbd, _NT, preferred_element_type=F32)
                u_scr[d_idx, ti] = jnp.where(u_top, uo[:hd], uo[hd:])
            amats = [(sk * same_f).astype(BF16) for sk in skews]
            oi_scr[:, ls] = jnp.concatenate(
                [lax.dot_general(vb[:hd], amats[0], _NT, preferred_element_type=F32),
                 lax.dot_general(vb[hd:], amats[1], _NT, preferred_element_type=F32)], axis=0)
            return c

        lax.fori_loop(0, nt, phase_a, 0, unroll=2)

        def phase_b(ti, st):
            sf, sb = st
            u = u_scr[0, ti]
            dec = dec_scr[0, ti]
            pieces = [None] * cpt
            for c in range(cpt):
                pieces[c] = sf
                sf = dec[c:c + 1, :] * sf + u[:, c * pw:(c + 1) * pw]
            sst_scr[0, ti] = jnp.concatenate(pieces, axis=1).astype(BF16)
            tb = nt - 1 - ti
            u = u_scr[1, tb]
            dec = dec_scr[1, tb]
            for c in range(cpt - 1, -1, -1):
                pieces[c] = sb
                sb = dec[c:c + 1, :] * sb + u[:, c * pw:(c + 1) * pw]
            sst_scr[1, tb] = jnp.concatenate(pieces, axis=1).astype(BF16)
            return sf, sb

        if zero_init:
            s0 = (jnp.zeros((hd, pw), F32), jnp.zeros((hd, pw), F32))
        else:
            s0 = tuple(jnp.concatenate([s0_r[k, 2 * pr], s0_r[k, 2 * pr + 1]], axis=1) for k in range(2))
        s_end = lax.fori_loop(0, nt, phase_b, s0)
        for k in range(2):
            send_o[k, 2 * pr] = s_end[k][:, :hd]
            send_o[k, 2 * pr + 1] = s_end[k][:, hd:]

        def phase_c(ti, c):
            ls = pl.ds(pl.multiple_of(ti * LANES, LANES), LANES)
            o = oi_scr[:, ls]
            for k in range(2):
                t8 = jnp.concatenate([qt_scr[k, :, ls].astype(BF16)] * cpt, axis=0)
                qb = jnp.concatenate([t8 * bd_top, t8 * bd_bot], axis=1)
                oo = dot(sst_scr[k, ti], qb)
                o = o + jnp.concatenate([oo[:, :LANES], oo[:, LANES:]], axis=0)
            o0, o1 = o[:hd], o[hd:]
            on = jnp.concatenate([o0 * lax.rsqrt(jnp.mean(o0 * o0, axis=0, keepdims=True) + EPS),
                                  o1 * lax.rsqrt(jnp.mean(o1 * o1, axis=0, keepdims=True) + EPS)], axis=0)
            on_scr[rows, ls] = on * gain2 * _silu(g_r[rows, ls])
            return c

        lax.fori_loop(0, nt, phase_c, 0, unroll=2)
        return carry

    lax.fori_loop(0, B_HEADS // 2, pair_step, 0)
    ob_o[...] = on_scr[...].T.astype(BF16)


def _hgrn(hgt, col0, n_batch, t, lb_b, gain_b, s0t):
    bw = B_HEADS * HEAD_DIM
    hd, pw = HEAD_DIM, 2 * HEAD_DIM
    nt = t // LANES
    cpt = LANES // GLA_CHUNK
    zero_init = s0t is None
    specs = [pl.BlockSpec((bw, t), functools.partial(lambda r, b: (r, col0 // t + b), r)) for r in range(5)]
    in_specs = specs + [_full_spec(lb_b.shape), _full_spec(gain_b.shape)]
    args = [hgt] * 5 + [lb_b, gain_b]
    st_spec = pl.BlockSpec((None, 2, B_HEADS, HEAD_DIM, HEAD_DIM), lambda b: (b, 0, 0, 0, 0))
    if not zero_init:
        in_specs.append(st_spec)
        args.append(s0t)
    scratch = [pltpu.VMEM((pw, t), F32), pltpu.VMEM((2, nt, hd, cpt * pw), F32), pltpu.VMEM((2, nt, cpt, pw), F32),
               pltpu.VMEM((2, pw, t), F32), pltpu.VMEM((2, nt, hd, cpt * pw), BF16), pltpu.VMEM((bw, t), F32)]
    return pl.pallas_call(
        functools.partial(_hgrn_body, zero_init, t),
        grid=(n_batch,),
        in_specs=in_specs,
        out_specs=[pl.BlockSpec((t, bw), lambda b: (b, 0)), st_spec],
        out_shape=[jax.ShapeDtypeStruct((n_batch * t, bw), BF16),
                   jax.ShapeDtypeStruct((n_batch, 2, B_HEADS, HEAD_DIM, HEAD_DIM), F32)],
        scratch_shapes=scratch,
        compiler_params=_params(1),
        name="hgrn2_context" if zero_init else "hgrn2_latent",
    )(*args)


def _route(logits):
    lane_i = lax.broadcasted_iota(jnp.int32, logits.shape, 1)
    lane = lane_i.astype(F32)
    neg = -jnp.inf
    big = 1e9
    gl = jnp.where(lane_i < MOE_GROUPS, logits, neg)
    gmax = jnp.max(gl, axis=-1, keepdims=True)
    g_val = 1.0 / jnp.sum(jnp.exp(gl - gmax), axis=-1, keepdims=True)
    g_idx = jnp.min(jnp.where(gl == gmax, lane, big), axis=-1, keepdims=True)
    n_e = MOE_GROUPS * EXPERTS_PER_GROUP
    lane_group = ((lane_i - MOE_GROUPS) >> 2).astype(F32)
    in_group = (lane_i >= MOE_GROUPS) & (lane_i < MOE_GROUPS + n_e) & (lane_group == g_idx)
    el = jnp.where(in_group, logits, neg)
    m1 = jnp.max(el, axis=-1, keepdims=True)
    i1 = jnp.min(jnp.where(el == m1, lane, big), axis=-1, keepdims=True)
    el2 = jnp.where(lane == i1, neg, el)
    m2 = jnp.max(el2, axis=-1, keepdims=True)
    i2 = jnp.min(jnp.where(el2 == m2, lane, big), axis=-1, keepdims=True)
    tt = jnp.exp(m2 - m1)
    w1 = g_val / (1.0 + tt)
    w2 = g_val * tt / (1.0 + tt)
    base = MOE_GROUPS + EXPERTS_PER_GROUP * g_idx
    route = jnp.where(lane == i1 - base, w1, 0.0) + jnp.where(lane == i2 - base, w2, 0.0)
    return jnp.where(lane_i == ROUTE_GID_LANE, g_idx, route)


def _proj_res_body(geom, lhs_n, n_x, has_y, d, *refs):
    lhs = []
    for n_a in lhs_n:
        lhs.append(geom.tok_load(refs[:n_a]))
        refs = refs[n_a:]
    ws, refs = refs[:len(lhs_n)], refs[len(lhs_n):]
    xs, refs = refs[:n_x], refs[n_x:]
    x = geom.tok_load(xs)
    if has_y:
        y, g2, refs = refs[0], refs[1], refs[2:]
        x = x + g2[...] * _tile_rows_load(y, TOKEN_TILE)
    g1, gn2, sh2, sc2, wrh, wrl, br, xnew_o, xr_o, route_o = refs[:10]
    wbs = refs[10:]

    @pl.when(pl.program_id(0) == 0)
    def _():
        for w, wb in zip(ws, wbs):
            wb[...] = w[...].astype(BF16)

    acc = jnp.dot(lhs[0], wbs[0][...], preferred_element_type=F32)
    for a, wb in zip(lhs[1:], wbs[1:]):
        acc = acc + jnp.dot(a, wb[...], preferred_element_type=F32)
    xn = x + g1[...] * acc
    xnew_o[...] = xn
    hm = _modulate(xn, gn2[...], sh2[...], sc2[...])
    hi = hm.astype(BF16)
    lo = (hm - hi.astype(F32)).astype(BF16)
    logits = (jnp.dot(hi, wrh[...], preferred_element_type=F32) + jnp.dot(lo, wrh[...], preferred_element_type=F32)
              + jnp.dot(hi, wrl[...], preferred_element_type=F32)) + br[...]
    route = _route(logits)
    for s, piece in enumerate(_pack_routed(hm, route)):
        xr_o[pl.ds(s, TOKEN_TILE, stride=ROUTED_SUB), :] = piece
    route_o[...] = route


def _proj_res(geom, mod, layer, lhs, ws, xa, y, gn2, wrh, wrl, br, name):
    tm = TOKEN_TILE
    d = geom.d
    has_y = y is not None
    tok = pl.BlockSpec((tm, d), lambda i: (i, 0))
    in_specs, args = [], []
    for a in lhs:
        s, ar = geom.tok_specs(a)
        in_specs += s
        args += ar
    w_shapes = []
    for w_all, lead, row_block in ws:
        in_specs.append(_resident_slab(w_all.shape, lead, row_block))
        args.append(w_all)
        w_shapes.append((w_all.shape[1] if row_block is None else row_block[1],) + tuple(w_all.shape[2:]))
    s, ar = geom.tok_specs(xa)
    in_specs += s
    args += ar
    if has_y:
        in_specs += [_Y_SPEC, geom.mod_spec(layer - 1, 5)]
        args += [y, mod]
    in_specs += [geom.mod_spec(layer, 2), _row_spec(d), geom.mod_spec(layer, 3), geom.mod_spec(layer, 4),
                 _full_spec(wrh.shape), _full_spec(wrl.shape), _full_spec(br.shape)]
    args += [mod, gn2, mod, mod, wrh, wrl, br]
    return pl.pallas_call(
        functools.partial(_proj_res_body, geom, tuple(_n_tok(a) for a in lhs), _n_tok(xa), has_y, d),
        grid=(geom.n // tm,),
        in_specs=in_specs,
        out_specs=[tok, pl.BlockSpec((tm * ROUTED_SUB, LANES), lambda i: (i, 0)),
                   pl.BlockSpec((tm, ROUTE_W), lambda i: (i, 0))],
        out_shape=[jax.ShapeDtypeStruct((geom.n, d), F32), jax.ShapeDtypeStruct((geom.n * ROUTED_SUB, LANES), F32),
                   jax.ShapeDtypeStruct((geom.n, ROUTE_W), F32)],
        scratch_shapes=[pltpu.VMEM(shp, BF16) for shp in w_shapes],
        compiler_params=_params(1),
        name=name,
    )(*args)


def _dispatch(route, n, tile):
    n_tiles = n // tile + MOE_GROUPS
    n_slots = n_tiles * tile
    gid = route[:, ROUTE_GID_LANE].astype(jnp.int32)
    onehot = (gid[:, None] == jnp.arange(MOE_GROUPS, dtype=jnp.int32)[None, :]).astype(jnp.int32)
    csum = jnp.cumsum(onehot, axis=0)
    counts = csum[-1]
    rank = jnp.sum(csum * onehot, axis=1) - 1
    tiles_g = (counts + tile - 1) // tile
    tile_end = jnp.cumsum(tiles_g)
    tile_start = tile_end - tiles_g
    dest = (tile_start * tile)[gid] + rank
    tok = jnp.arange(n, dtype=jnp.int32)
    src = jnp.zeros((n_slots,), jnp.int32).at[dest].set(tok)
    t_idx = jnp.arange(n_tiles, dtype=jnp.int32)
    tile_gid = jnp.minimum(jnp.sum((t_idx[:, None] >= tile_end[None, :]).astype(jnp.int32), axis=1), MOE_GROUPS - 1)
    tile_rows = jnp.clip(counts[tile_gid] - (t_idx - tile_start[tile_gid]) * tile, 0, tile)
    tile_rows = jnp.where(t_idx < tile_end[-1], tile_rows, 0)
    in_tile = jnp.arange(tile, dtype=jnp.int32)
    valid = (in_tile[None, :] < tile_rows[:, None]).reshape(n_slots)
    spare = (n + (t_idx[:, None] & 1) * tile + in_tile[None, :]).reshape(n_slots)
    dst = jnp.concatenate([n + tile + in_tile, jnp.where(valid, src, spare)])
    return tile_gid, src, dst, tile_end[-1:].astype(jnp.int32), n_tiles


def _moe_body(tile, d, n_rows, tg_ref, src_ref, dst_ref, nu_ref, xr_hbm, w1_ref, w3_ref, w2_ref, y_hbm,
              xbuf, ybuf, w1b, w3b, w2b, isem, osem):
    t = pl.program_id(0)
    last = nu_ref[0] - 1
    slot = t % 2
    other = 1 - slot

    sub = SUBLANES
    rsub = ROUTED_SUB

    def hbm_rows(ref, row, n_sub):
        start = row * n_sub if isinstance(row, int) else pl.multiple_of(row * n_sub, n_sub)
        return ref.at[pl.ds(start, n_sub), :]

    def row_in(tile_idx, s, i):
        row = src_ref[tile_idx * tile + i]
        return pltpu.make_async_copy(hbm_rows(xr_hbm, row, rsub), xbuf.at[s, pl.ds(i * rsub, rsub), :], isem.at[s])

    def row_out(ext_tile_idx, s, i):
        row = dst_ref[ext_tile_idx * tile + i]
        return pltpu.make_async_copy(ybuf.at[s, pl.ds(i * sub, sub), :], hbm_rows(y_hbm, row, sub), osem.at[s])

    @pl.when(t == 0)
    def _():
        ybuf[...] = jnp.zeros(ybuf.shape, F32)
        for i in range(tile):
            row_in(0, 0, i).start(priority=i % 2)
        for i in range(tile):
            pltpu.make_async_copy(ybuf.at[0, pl.ds(i * sub, sub), :], hbm_rows(y_hbm, n_rows + i, sub),
                                  osem.at[0]).start(priority=i % 2)

    g = tg_ref[t]
    g_prev = tg_ref[jnp.maximum(t - 1, 0)]

    @pl.when(((t == 0) | (g != g_prev)) & (t <= last))
    def _():
        w1b[...] = w1_ref[...].astype(BF16)
        w3b[...] = w3_ref[...].astype(BF16)
        w2b[...] = w2_ref[...].astype(BF16)

    nxt = jnp.minimum(t + 1, last)
    n_parts = 4 * EXPERTS_PER_GROUP
    per = tile // n_parts

    def step(slot, other):
        for i in range(tile):
            row_in(0, slot, i).wait()
        part = [0]

        def issue_part():
            k = part[0]
            part[0] += 1
            for i in range(k * per, (k + 1) * per):
                row_in(nxt, other, i).start(priority=i % 2)
                row_out(t, other, i).start(priority=i % 2)

        xb, cw = _unpack_routed(xbuf.at[slot], tile)
        acc = jnp.zeros((tile, d), F32)
        for e in range(EXPERTS_PER_GROUP):
            a = jnp.dot(xb, w1b[e], preferred_element_type=F32)
            issue_part()
            b = jnp.dot(xb, w3b[e], preferred_element_type=F32)
            issue_part()
            hid = (_silu(a) * b * cw[:, e:e + 1]).astype(BF16)
            issue_part()
            acc = acc + jnp.dot(hid, w2b[e], preferred_element_type=F32)
            issue_part()
        for i in range(tile):
            row_out(0, slot, i).wait()
        _tile_rows_store(ybuf.at[slot], acc)

        @pl.when(t == last)
        def _():
            for i in range(tile):
                row_in(0, other, i).wait()
            for i in range(tile):
                row_out(0, other, i).wait()
            for i in range(tile):
                row_out(t + 1, slot, i).start(priority=i % 2)
            for i in range(tile):
                row_out(0, slot, i).wait()

    @pl.when((slot == 0) & (t <= last))
    def _():
        step(0, 1)

    @pl.when((slot == 1) & (t <= last))
    def _():
        step(1, 0)


def _moe(xr, route, w1, w3, w2, layer, n, d):
    tile = MOE_TILE
    assert d == SUBLANES * LANES
    tile_gid, src, dst, n_used, n_tiles = _dispatch(route, n, tile)
    g0 = layer * MOE_GROUPS
    wspec13 = pl.BlockSpec((None,) + w1.shape[1:], lambda t, tg, s, ds_, nu: (g0 + tg[t], 0, 0, 0))
    wspec2 = pl.BlockSpec((None,) + w2.shape[1:], lambda t, tg, s, ds_, nu: (g0 + tg[t], 0, 0, 0))
    grid_spec = pltpu.PrefetchScalarGridSpec(
        num_scalar_prefetch=4,
        grid=(n_tiles,),
        in_specs=[pl.BlockSpec(memory_space=pl.ANY), wspec13, wspec13, wspec2],
        out_specs=pl.BlockSpec(memory_space=pl.ANY),
        scratch_shapes=[pltpu.VMEM((2, tile * ROUTED_SUB, LANES), F32), pltpu.VMEM((2, tile * SUBLANES, LANES), F32),
                        pltpu.VMEM(w1.shape[1:], BF16), pltpu.VMEM(w3.shape[1:], BF16), pltpu.VMEM(w2.shape[1:], BF16),
                        pltpu.SemaphoreType.DMA((2,)), pltpu.SemaphoreType.DMA((2,))],
    )
    return pl.pallas_call(
        functools.partial(_moe_body, tile, d, n),
        grid_spec=grid_spec,
        out_shape=jax.ShapeDtypeStruct(((n + 2 * tile) * SUBLANES, LANES), F32),
        compiler_params=_params(1),
        name="moe_ffn",
    )(tile_gid, src, dst, n_used, xr, w1, w3, w2)


def _glu_body(xa, y, g2, gn, sh, sc, w, u_o, wb):
    @pl.when(pl.program_id(0) == 0)
    def _():
        wb[...] = w[...].astype(BF16)

    x = xa[...] + g2[...] * _tile_rows_load(y, TOKEN_TILE)
    h = _modulate(x, gn[...], sh[...], sc[...]).astype(BF16)
    ab = jnp.dot(h, wb[...], preferred_element_type=F32)
    c = ab.shape[1] // 2
    u_o[...] = ab[:, :c] * jax.nn.sigmoid(ab[:, c:])


def _glu(geom, mod, layer, xa, y, gn, w_all, o):
    tm = TOKEN_TILE
    d = geom.d
    tok = pl.BlockSpec((tm, d), lambda i: (i, 0))
    c = w_all.shape[2] // 2
    return pl.pallas_call(
        _glu_body,
        grid=(geom.n // tm,),
        in_specs=[tok, _Y_SPEC, geom.mod_spec(layer - 1, 5), _row_spec(d), geom.mod_spec(layer, 0),
                  geom.mod_spec(layer, 1), _resident_slab(w_all.shape, o)],
        out_specs=pl.BlockSpec((tm, c), lambda i: (i, 0)),
        out_shape=jax.ShapeDtypeStruct((geom.n, c), F32),
        scratch_shapes=[pltpu.VMEM(w_all.shape[1:], BF16)],
        compiler_params=_params(1),
        name="conv_glu",
    )(xa, y, mod, gn, mod, mod, w_all)


def _dwconv_body(t, u_r, dw_r, b_r, g_r, be_r, o_r, pad_scr, cv_scr, sh_scr):
    c = u_r.shape[1]
    halo = 16
    sh_len = sh_scr.shape[1]
    pad_scr[0:halo, :] = jnp.zeros((halo, c), F32)
    pad_scr[t + halo:t + 2 * halo, :] = jnp.zeros((halo, c), F32)
    pad_scr[halo:t + halo, :] = u_r[...]
    rb_n = CONV_ROWS
    n_lb = c // LANES
    first = halo - CONV_WIDTH // 2

    def blk(lb, carry):
        ls = pl.ds(pl.multiple_of(lb * LANES, LANES), LANES)
        bias = b_r[:, ls]
        for r0 in range(0, t, rb_n):
            win = pad_scr[r0:r0 + rb_n + 2 * halo, ls]
            for sft in range(1, SUBLANES):
                sh_scr[sft] = win[sft:sft + sh_len, :]
            acc = jnp.zeros((rb_n, LANES), F32)
            for k in range(CONV_WIDTH):
                sft = (first + k) % SUBLANES
                a = first + k - sft
                src = pad_scr[r0 + a:r0 + a + rb_n, ls] if sft == 0 else sh_scr[sft, a:a + rb_n, :]
                acc = acc + src * dw_r[k:k + 1, ls]
            cv_scr[r0:r0 + rb_n, ls] = acc + bias
        return carry

    lax.fori_loop(0, n_lb, blk, 0)

    def ln(rb, carry):
        rs = pl.ds(pl.multiple_of(rb * rb_n, rb_n), rb_n)
        x = cv_scr[rs, :]
        mu = jnp.mean(x, axis=-1, keepdims=True)
        xc = x - mu
        var = jnp.mean(xc * xc, axis=-1, keepdims=True)
        yv = xc * lax.rsqrt(var + EPS) * g_r[...] + be_r[...]
        o_r[rs, :] = _silu(yv).astype(BF16)
        return carry

    lax.fori_loop(0, t // rb_n, ln, 0)


def _dwconv(u, row0, n_batch, t, dw, b, g, be):
    c = u.shape[1]
    return pl.pallas_call(
        functools.partial(_dwconv_body, t),
        grid=(n_batch,),
        in_specs=[pl.BlockSpec((t, c), lambda bi: (row0 // t + bi, 0)), _full_spec(dw.shape),
                  _row_spec(c), _row_spec(c), _row_spec(c)],
        out_specs=pl.BlockSpec((t, c), lambda bi: (bi, 0)),
        out_shape=jax.ShapeDtypeStruct((n_batch * t, c), BF16),
        scratch_shapes=[pltpu.VMEM((t + 32, c), F32), pltpu.VMEM((t, c), F32),
                        pltpu.VMEM((SUBLANES, CONV_ROWS + (CONV_WIDTH // SUBLANES) * SUBLANES, LANES), F32)],
        compiler_params=_params(1),
        name="conv_dw",
    )(u, dw, b, g, be)


def _final_body(ctx_tiles, xa, y, g2, gf, oc, os_):
    x = xa[...] + g2[...] * _tile_rows_load(y, TOKEN_TILE)
    ms = jnp.mean(x * x, axis=-1, keepdims=True)
    val = x * lax.rsqrt(ms + EPS) * gf[...]
    i = pl.program_id(0)

    @pl.when(i < ctx_tiles)
    def _():
        oc[...] = val

    @pl.when(i >= ctx_tiles)
    def _():
        os_[...] = val


def _final(geom, mod, depth, xa, y, gf):
    tm = TOKEN_TILE
    d = geom.d
    ct = geom.ctx_tiles
    tok = pl.BlockSpec((tm, d), lambda i: (i, 0))
    return pl.pallas_call(
        functools.partial(_final_body, ct),
        grid=(geom.n // tm,),
        in_specs=[tok, _Y_SPEC, geom.mod_spec(depth - 1, 5), _row_spec(d)],
        out_specs=[pl.BlockSpec((tm, d), lambda i: (jnp.minimum(i, ct - 1), 0)),
                   pl.BlockSpec((tm, d), lambda i: (jnp.maximum(i - ct, 0), 0))],
        out_shape=[jax.ShapeDtypeStruct((geom.n_ctx, d), F32), jax.ShapeDtypeStruct((geom.n_smp, d), F32)],
        compiler_params=_params(1),
        name="final_norm",
    )(xa, y, mod, gf)


def _rope_tables(n_tokens):
    rows = n_tokens // GRID_W
    row = jnp.repeat(jnp.arange(rows, dtype=F32), GRID_W)
    col = jnp.tile(jnp.arange(GRID_W, dtype=F32), rows)
    half = HEAD_DIM // 2
    inv = ROPE_THETA ** (-jnp.arange(0, half, 2, dtype=F32) / half)
    ang_r = row[:, None] * inv[None, :]
    ang_c = col[:, None] * inv[None, :]
    cr, sr, cc, sc = jnp.cos(ang_r), jnp.sin(ang_r), jnp.cos(ang_c), jnp.sin(ang_c)
    cos = jnp.concatenate([cr, cr, cc, cc], axis=1)
    sin = jnp.concatenate([-sr, sr, -sc, sc], axis=1)
    return (jnp.tile(cos, (1, A_HEADS)), jnp.tile(sin, (1, A_HEADS)),
            jnp.tile(cos, (1, A_KV_HEADS)), jnp.tile(sin, (1, A_KV_HEADS)))


def _router_weights(w_group, b_group, w_expert, b_expert):
    d = w_group.shape[0]
    n_used = MOE_GROUPS + MOE_GROUPS * EXPERTS_PER_GROUP
    w = jnp.concatenate([w_group, w_expert, jnp.zeros((d, ROUTE_W - n_used), F32)], axis=1)
    b = jnp.concatenate([b_group, b_expert, jnp.zeros((ROUTE_W - n_used,), F32)])[None, :]
    hi = w.astype(BF16)
    lo = (w - hi.astype(F32)).astype(BF16)
    return hi, lo, b


def kernel(x_prompt, x_sample, cache_k, cache_v, state_hgrn, c, c_ctx, ada_w, ada_b, norm1_g, norm2_g, mix_w_in, mix_w_out, attn_q_gain, attn_k_gain, hgrn_lower_bound, hgrn_out_gain, conv_pw1, conv_dw, conv_dw_b, conv_ln_g, conv_ln_b, conv_pw2, moe_w_group, moe_b_group, moe_w_expert, moe_b_expert, moe_w1, moe_w3, moe_w2, final_g):
    batch, seq, d = x_prompt.shape
    dec_batch, dec_seq, _ = x_sample.shape
    depth = ada_w.shape[0]
    past = cache_k.shape[2]
    n_ctx, n_smp = batch * seq, dec_batch * dec_seq
    n = n_ctx + n_smp
    geom = _Geom(n_ctx, n_smp, dec_seq, d)
    qw, kw, bw = A_HEADS * HEAD_DIM, A_KV_HEADS * HEAD_DIM, B_HEADS * HEAD_DIM
    n_nat = qw + 2 * kw

    cond8 = jnp.concatenate([c_ctx[None, :], c, jnp.zeros((8 - 1 - dec_batch, d), F32)], axis=0)
    mod = _ada(cond8, ada_w, ada_b).reshape(depth, 8, 6, 1, d)

    lb_soft = jax.nn.softmax(hgrn_lower_bound.astype(F32), axis=0)
    lower_bounds = jnp.cumsum(lb_soft, axis=0) - lb_soft[0]
    rope_tabs = _rope_tables(dec_seq)

    xa = (x_prompt.reshape(n_ctx, d), x_sample.reshape(n_smp, d))
    y = None
    f = moe_w1.shape[-1]
    w1 = moe_w1.reshape(depth * MOE_GROUPS, EXPERTS_PER_GROUP, d, f)
    w3 = moe_w3.reshape(depth * MOE_GROUPS, EXPERTS_PER_GROUP, d, f)
    w2 = moe_w2.reshape(depth * MOE_GROUPS, EXPERTS_PER_GROUP, f, d)
    n_ab = cache_k.shape[1]
    ck = cache_k.reshape(dec_batch, n_ab, past, kw)
    cv = cache_v.reshape(dec_batch, n_ab, past, kw)
    new_k, new_v, new_s = [], [], []
    for l in range(depth):
        wrh, wrl, br = _router_weights(moe_w_group[l], moe_b_group[l], moe_w_expert[l], moe_b_expert[l])
        gn1 = norm1_g[l][None, :]
        gn2 = norm2_g[l][None, :]
        if l % 2 == 0:
            e = l // 2
            qkv, hgt = _inproj(geom, mod, l, xa, y, gn1, mix_w_in, e, n_nat)
            qg = jnp.tile(attn_q_gain[e], A_HEADS)[None, :]
            kg = jnp.tile(attn_k_gain[e], A_KV_HEADS)[None, :]
            oa_c, khat = _attention(qkv, 0, batch, seq, qg, kg)
            oa_s = _attention(qkv, n_ctx, dec_batch, dec_seq, qg, kg, cache=(ck, cv, e), rope_tabs=rope_tabs)
            lb_b = jnp.broadcast_to(lower_bounds[e][:, :, None], (2, bw, LANES))
            gain_b = jnp.broadcast_to(hgrn_out_gain[e][:, None], (HEAD_DIM, LANES))
            ob_c, s_ctx = _hgrn(hgt, 0, batch, seq, lb_b, gain_b, None)
            s0t = jnp.swapaxes(state_hgrn[:, e], -1, -2)
            ob_s, _ = _hgrn(hgt, n_ctx, dec_batch, dec_seq, lb_b, gain_b, s0t)
            w_halves = [(mix_w_out, e, (0, qw)), (mix_w_out, e, (1, bw))]
            xnew, xr, route = _proj_res(geom, mod, l, [(oa_c, oa_s), (ob_c, ob_s)], w_halves, xa, y,
                                        gn2, wrh, wrl, br, "ab_outproj")
            new_k.append(khat.reshape(batch, seq, A_KV_HEADS, HEAD_DIM))
            new_v.append(qkv[:n_ctx, qw + kw:].reshape(batch, seq, A_KV_HEADS, HEAD_DIM))
            new_s.append(jnp.swapaxes(s_ctx, -1, -2))
        else:
            o = l // 2
            u = _glu(geom, mod, l, xa, y, gn1, conv_pw1, o)
            cargs = (conv_dw[o], conv_dw_b[o][None, :], conv_ln_g[o][None, :], conv_ln_b[o][None, :])
            c_c = _dwconv(u, 0, batch, seq, *cargs)
            c_s = _dwconv(u, n_ctx, dec_batch, dec_seq, *cargs)
            xnew, xr, route = _proj_res(geom, mod, l, [(c_c, c_s)], [(conv_pw2, o, None)], xa, y, gn2, wrh, wrl, br,
                                        "conv_outproj")
        y = _moe(xr, route, w1, w3, w2, l, n, d)
        xa = xnew
    y_prompt, y_sample = _final(geom, mod, depth, xa, y, final_g[None, :])
    return (y_prompt.reshape(batch, seq, d), y_sample.reshape(dec_batch, dec_seq, d),
            jnp.stack(new_k, axis=1), jnp.stack(new_v, axis=1), jnp.stack(new_s, axis=1))
```

```python
import functools

import jax
import jax.numpy as jnp
from jax import lax
from jax.experimental import pallas as pl
from jax.experimental.pallas import tpu as pltpu

F32 = jnp.float32
BF16 = jnp.bfloat16

HEAD_DIM = 64
A_HEADS = 8
A_KV_HEADS = 2
A_GROUP = A_HEADS // A_KV_HEADS
B_HEADS = 8
GRID_W = 64
ROPE_THETA = 10000.0
GLA_CHUNK = 16
CONV_WIDTH = 31
MOE_GROUPS = 4
EXPERTS_PER_GROUP = 4
EPS = 1e-6

LANES = 128
SUBLANES = 8
VMEM_LIMIT = 56 * 1024 * 1024
TOKEN_TILE = 512
MOE_TILE = 256
ATTN_Q_TILE = 256
ROUTE_W = LANES
ROUTE_GID_LANE = 8
CONV_ROWS = 64

_NT = (((1,), (1,)), ((), ()))


def _params(n_axes):
    return pltpu.CompilerParams(dimension_semantics=("arbitrary",) * n_axes, vmem_limit_bytes=VMEM_LIMIT)


def _modulate(x, g, sh, sc):
    ms = jnp.mean(x * x, axis=-1, keepdims=True)
    return (x * lax.rsqrt(ms + EPS) * g) * (1.0 + sc) + sh


def _silu(x):
    return x * jax.nn.sigmoid(x)


def _tile_rows_load(ref, n_rows):
    return jnp.concatenate([ref[pl.ds(s, n_rows, stride=SUBLANES), :] for s in range(SUBLANES)], axis=1)


def _tile_rows_store(ref, val):
    n_rows = val.shape[0]
    for s in range(SUBLANES):
        ref[pl.ds(s, n_rows, stride=SUBLANES), :] = val[:, s * LANES:(s + 1) * LANES]


ROUTED_SUB = 2 * SUBLANES


def _pack_routed(hm, route):
    pieces = [hm[:, s * LANES:(s + 1) * LANES] for s in range(SUBLANES)] + [route]
    return pieces + [jnp.zeros(route.shape, F32)] * (ROUTED_SUB - len(pieces))


def _unpack_routed(ref, n_rows):
    chunks = [ref[pl.ds(s, n_rows, stride=ROUTED_SUB), :] for s in range(SUBLANES)]
    route = ref[pl.ds(SUBLANES, n_rows, stride=ROUTED_SUB), :]
    return jnp.concatenate(chunks, axis=1).astype(BF16), route


def _split3(x):
    hi = x.astype(BF16)
    r1 = x - hi.astype(F32)
    mid = r1.astype(BF16)
    lo = (r1 - mid.astype(F32)).astype(BF16)
    return hi, mid, lo


def _ada_body(c_ref, w_ref, b_ref, o_ref):
    s = _silu(c_ref[...])
    o_ref[...] = jnp.dot(s.astype(BF16), w_ref[...].astype(BF16), preferred_element_type=F32) + b_ref[...]


def _ada(cond8, ada_w, ada_b):
    depth, d, d6 = ada_w.shape
    tn = d6 // 4
    return pl.pallas_call(
        _ada_body,
        grid=(depth, d6 // tn),
        in_specs=[
            pl.BlockSpec((8, d), lambda l, j: (0, 0)),
            pl.BlockSpec((None, d, tn), lambda l, j: (l, 0, j)),
            pl.BlockSpec((None, 1, tn), lambda l, j: (l, 0, j)),
        ],
        out_specs=pl.BlockSpec((None, 8, tn), lambda l, j: (l, 0, j)),
        out_shape=jax.ShapeDtypeStruct((depth, 8, d6), F32),
        compiler_params=_params(2),
        name="adaln",
    )(cond8, ada_w, ada_b.reshape(depth, 1, d6))


class _Geom:
    def __init__(self, n_ctx, n_smp, smp_len, d):
        self.n_ctx, self.n_smp, self.smp_len, self.d = n_ctx, n_smp, smp_len, d
        self.n = n_ctx + n_smp
        self.ctx_tiles = n_ctx // TOKEN_TILE

    def mod_spec(self, layer, j):
        ctx_tiles = self.ctx_tiles
        per_smp = self.smp_len // TOKEN_TILE

        def idx(i):
            row = jnp.where(i < ctx_tiles, 0, 1 + (i - ctx_tiles) // per_smp)
            return (layer, row, j, 0, 0)

        return pl.BlockSpec((None, None, None, 1, self.d), idx)

    def tok_specs(self, a):
        tm = TOKEN_TILE
        if isinstance(a, tuple):
            ct = self.ctx_tiles
            w = a[0].shape[1]
            return ([pl.BlockSpec((tm, w), lambda i: (jnp.minimum(i, ct - 1), 0)),
                     pl.BlockSpec((tm, w), lambda i: (jnp.maximum(i - ct, 0), 0))], list(a))
        return [pl.BlockSpec((tm, a.shape[1]), lambda i: (i, 0))], [a]

    def tok_load(self, refs):
        if len(refs) == 1:
            return refs[0][...]
        return jnp.where(pl.program_id(0) < self.ctx_tiles, refs[0][...], refs[1][...])


def _n_tok(a):
    return 2 if isinstance(a, tuple) else 1


def _row_spec(d):
    return pl.BlockSpec((1, d), lambda i: (0, 0))


def _layer_spec(shape, lead):
    return pl.BlockSpec((None,) + tuple(shape[1:]), lambda *_: (lead,) + (0,) * (len(shape) - 1))


def _resident_slab(shape, lead, row_block=None):
    rows = shape[1] if row_block is None else row_block[1]
    rb = 0 if row_block is None else row_block[0]
    return pl.BlockSpec((None, rows) + tuple(shape[2:]), lambda *_: (lead, rb) + (0,) * (len(shape) - 2),
                        pipeline_mode=pl.Buffered(1))


def _full_spec(shape):
    nd = len(shape)
    return pl.BlockSpec(shape, lambda *_: (0,) * nd)


_Y_SPEC = pl.BlockSpec((TOKEN_TILE * SUBLANES, LANES), lambda i: (i, 0))


def _inproj_body(geom, n_x, has_y, n_nat, *refs):
    xs, refs = refs[:n_x], refs[n_x:]
    x = geom.tok_load(xs)
    if has_y:
        y, g2, refs = refs[0], refs[1], refs[2:]
        x = x + g2[...] * _tile_rows_load(y, TOKEN_TILE)
    gn, sh, sc, w, qkv_o, hgt_o, wn_s, wt_s = refs
    step = 512
    n_t = wt_s.shape[0]

    @pl.when(pl.program_id(0) == 0)
    def _():
        wn_s[...] = w[:, :n_nat].astype(BF16)
        for r in range(0, n_t, step):
            wt_s[r:r + step, :] = w[:, n_nat + r:n_nat + r + step].T.astype(BF16)

    h = _modulate(x, gn[...], sh[...], sc[...]).astype(BF16)
    qkv_o[...] = jnp.dot(h, wn_s[...], preferred_element_type=F32)
    for r in range(0, n_t, step):
        hgt_o[r:r + step, :] = lax.dot_general(wt_s[r:r + step, :], h, _NT, preferred_element_type=F32)


def _inproj(geom, mod, layer, xa, y, gn, w_in_all, e, n_nat):
    tm = TOKEN_TILE
    d = geom.d
    n_t = w_in_all.shape[2] - n_nat
    has_y = y is not None
    in_specs, args = geom.tok_specs(xa)
    if has_y:
        in_specs += [_Y_SPEC, geom.mod_spec(layer - 1, 5)]
        args += [y, mod]
    in_specs += [_row_spec(d), geom.mod_spec(layer, 0), geom.mod_spec(layer, 1), _resident_slab(w_in_all.shape, e)]
    args += [gn, mod, mod, w_in_all]
    return pl.pallas_call(
        functools.partial(_inproj_body, geom, _n_tok(xa), has_y, n_nat),
        grid=(geom.n // tm,),
        in_specs=in_specs,
        out_specs=[pl.BlockSpec((tm, n_nat), lambda i: (i, 0)), pl.BlockSpec((n_t, tm), lambda i: (0, i))],
        out_shape=[jax.ShapeDtypeStruct((geom.n, n_nat), F32), jax.ShapeDtypeStruct((n_t, geom.n), F32)],
        scratch_shapes=[pltpu.VMEM((d, n_nat), BF16), pltpu.VMEM((n_t, d), BF16)],
        compiler_params=_params(1),
        name="ab_inproj",
    )(*args)


def _head_norm(x, gain_row, n_heads):
    outs = []
    for h in range(n_heads):
        xh = x[:, h * HEAD_DIM:(h + 1) * HEAD_DIM]
        ms = jnp.mean(xh * xh, axis=-1, keepdims=True)
        outs.append(xh * lax.rsqrt(ms + EPS))
    return jnp.concatenate(outs, axis=1) * gain_row


def _rope(x, cos, sin_signed):
    w = x.shape[1]
    lane = lax.broadcasted_iota(jnp.int32, (1, w), 1)
    first_half = (lane & 16) == 0
    partner = jnp.where(first_half, pltpu.roll(x, w - 16, 1), pltpu.roll(x, 16, 1))
    return x * cos + partner * sin_signed


def _attn_body(rope, tq, *refs):
    if rope:
        q_r, k_r, v_r, ck_r, cv_r, qg_r, kg_r, cq_r, sq_r, ckk_r, skk_r, o_r, keys_s, vals_s = refs
    else:
        q_r, k_r, v_r, qg_r, kg_r, o_r, kh_r, keys_s, vals_s = refs

    @pl.when(pl.program_id(1) == 0)
    def _():
        kn = _head_norm(k_r[...], kg_r[...], A_KV_HEADS)
        if rope:
            kn = _rope(kn, ckk_r[...], skk_r[...])
            keys = jnp.concatenate([ck_r[...], kn], axis=0)
            vals = jnp.concatenate([cv_r[...], v_r[...]], axis=0)
        else:
            kh_r[...] = kn
            keys, vals = kn, v_r[...]
        keys_s[...] = keys.astype(BF16)
        ones = jnp.ones((keys.shape[0], HEAD_DIM), BF16)
        for j in range(A_KV_HEADS):
            vals_s[j] = jnp.concatenate([vals[:, j * HEAD_DIM:(j + 1) * HEAD_DIM].astype(BF16), ones], axis=1)

    qn = _head_norm(q_r[...], qg_r[...], A_HEADS)
    if rope:
        qn = _rope(qn, cq_r[...], sq_r[...])
    qn = qn * (HEAD_DIM ** -0.5)
    keys = keys_s[...]
    outs = [None] * A_HEADS
    for j in range(A_KV_HEADS):
        kj = keys[:, j * HEAD_DIM:(j + 1) * HEAD_DIM]
        vj = vals_s[j]
        heads = [A_GROUP * j + g for g in range(A_GROUP)]
        q4 = jnp.concatenate([qn[:, h * HEAD_DIM:(h + 1) * HEAD_DIM] for h in heads], axis=0).astype(BF16)
        s = lax.dot_general(q4, kj, _NT, preferred_element_type=F32)
        m = jnp.max(s, axis=-1, keepdims=True)
        p = jnp.exp(s - m).astype(BF16)
        ol = jnp.dot(p, vj, preferred_element_type=F32)
        o = ol[:, :HEAD_DIM] / ol[:, HEAD_DIM:HEAD_DIM + 1]
        for g, h in enumerate(heads):
            outs[h] = o[g * tq:(g + 1) * tq, :]
    o_r[...] = jnp.concatenate(outs, axis=1).astype(BF16)


def _attention(qkv, row0, n_batch, t, q_gain, k_gain, cache=None, rope_tabs=None):
    tq = ATTN_Q_TILE
    nq = t // tq
    qw, kw = A_HEADS * HEAD_DIM, A_KV_HEADS * HEAD_DIM
    rope = cache is not None
    q_spec = pl.BlockSpec((tq, qw), lambda b, qi: (row0 // tq + b * nq + qi, 0))
    k_spec = pl.BlockSpec((t, kw), lambda b, qi: (row0 // t + b, qw // kw))
    v_spec = pl.BlockSpec((t, kw), lambda b, qi: (row0 // t + b, qw // kw + 1))
    gq = pl.BlockSpec((1, qw), lambda b, qi: (0, 0))
    gk = pl.BlockSpec((1, kw), lambda b, qi: (0, 0))
    out_o = pl.BlockSpec((tq, qw), lambda b, qi: (b * nq + qi, 0))
    o_shape = jax.ShapeDtypeStruct((n_batch * t, qw), BF16)
    if rope:
        ck, cv, e = cache
        past = ck.shape[2]
        c_spec = pl.BlockSpec((None, None, past, kw), lambda b, qi: (b, e, 0, 0))
        cq, sq, ckk, skk = rope_tabs
        in_specs = [q_spec, k_spec, v_spec, c_spec, c_spec, gq, gk,
                    pl.BlockSpec((tq, qw), lambda b, qi: (qi, 0)), pl.BlockSpec((tq, qw), lambda b, qi: (qi, 0)),
                    pl.BlockSpec((t, kw), lambda b, qi: (0, 0)), pl.BlockSpec((t, kw), lambda b, qi: (0, 0))]
        args = [qkv, qkv, qkv, ck, cv, q_gain, k_gain, cq, sq, ckk, skk]
        out_specs, out_shape = out_o, o_shape
    else:
        in_specs = [q_spec, k_spec, v_spec, gq, gk]
        args = [qkv, qkv, qkv, q_gain, k_gain]
        out_specs = [out_o, pl.BlockSpec((t, kw), lambda b, qi: (b, 0))]
        out_shape = [o_shape, jax.ShapeDtypeStruct((n_batch * t, kw), F32)]
        past = 0
    n_keys = past + t
    return pl.pallas_call(
        functools.partial(_attn_body, rope, tq),
        grid=(n_batch, nq),
        in_specs=in_specs,
        out_specs=out_specs,
        out_shape=out_shape,
        scratch_shapes=[pltpu.VMEM((n_keys, kw), BF16), pltpu.VMEM((A_KV_HEADS, n_keys, 2 * HEAD_DIM), BF16)],
        compiler_params=_params(2),
        name="gqa_latent" if rope else "gqa_context",
    )(*args)


def _hgrn_body(zero_init, t, *refs):
    if zero_init:
        (q_r, ff_r, fb_r, v_r, g_r, lb_r, gain_r, ob_o, send_o,
         oi_scr, u_scr, dec_scr, qt_scr, sst_scr, on_scr) = refs
        s0_r = None
    else:
        (q_r, ff_r, fb_r, v_r, g_r, lb_r, gain_r, s0_r, ob_o, send_o,
         oi_scr, u_scr, dec_scr, qt_scr, sst_scr, on_scr) = refs
    nt = t // LANES
    hd = HEAD_DIM
    pw = 2 * hd
    cpt = LANES // GLA_CHUNK
    lane = lax.broadcasted_iota(jnp.int32, (1, LANES), 1)
    pos = lane & (GLA_CHUNK - 1)
    ri = lax.broadcasted_iota(jnp.int32, (LANES, LANES), 0)
    ci = lax.broadcasted_iota(jnp.int32, (LANES, LANES), 1)
    same = (ri >> 4) == (ci >> 4)
    same_f = jnp.where(same, 1.0, 0.0)

    def ind(cond):
        return jnp.where(cond, 1.0, 0.0).astype(BF16)

    m_fwd = jnp.concatenate([ind(same & (ri <= ci)), ind(same & (ri > ci))], axis=1)
    m_bwd = jnp.concatenate([ind(same & (ri >= ci)), ind(same & (ri < ci))], axis=1)
    eye = ind(ri == ci)
    sel_r = lax.broadcasted_iota(jnp.int32, (cpt, LANES), 0)
    sel_c = lax.broadcasted_iota(jnp.int32, (cpt, LANES), 1)
    sel = ind((sel_c >> 4) == sel_r)
    bd_r = lax.broadcasted_iota(jnp.int32, (cpt * pw, LANES), 0)
    bd_c = lax.broadcasted_iota(jnp.int32, (cpt * pw, LANES), 1)
    bd = (bd_r >> 7) == (bd_c >> 4)
    top = ((bd_r >> 6) & 1) == 0
    bd_all = ind(bd)
    bd_top = ind(bd & top)
    bd_bot = ind(bd & jnp.logical_not(top))
    u_lane = lax.broadcasted_iota(jnp.int32, (1, cpt * pw), 1)
    u_top = ((u_lane >> 6) & 1) == 0
    gain2 = jnp.concatenate([gain_r[...], gain_r[...]], axis=0)

    def colsum(a):
        return jnp.sum(a, axis=0, keepdims=True)

    def dot(a, b):
        return jnp.dot(a, b, preferred_element_type=F32)

    def pair_step(pr, carry):
        rows = pl.ds(pl.multiple_of(pr * pw, pw), pw)

        def phase_a(ti, c):
            ls = pl.ds(pl.multiple_of(ti * LANES, LANES), LANES)
            q = q_r[rows, ls]
            v = v_r[rows, ls]
            vb = v.astype(BF16)
            skews = [None, None]
            for bwd in (False, True):
                d_idx = int(bwd)
                raw_r = fb_r if bwd else ff_r
                lbv = lb_r[d_idx, rows, :]
                f = lbv + (1.0 - lbv) * jax.nn.sigmoid(raw_r[rows, ls])
                pm = (GLA_CHUNK - 1 - pos) if bwd else pos
                g_prev = [colsum(q[:hd]), colsum(q[hd:])]
                qd = q
                a_rows = [[], []]
                for dl in range(GLA_CHUNK):
                    qd = qd * (pltpu.roll(f, (LANES - dl) if bwd else dl, 1) if dl > 0 else f)
                    g_cur = [colsum(qd[:hd]), colsum(qd[hd:])]
                    for k in range(2):
                        a_rows[k].append(jnp.where(pm >= dl, g_prev[k] - g_cur[k], 0.0))
                    g_prev = g_cur
                for k in range(2):
                    order = a_rows[k] if bwd else [a_rows[k][0]] + a_rows[k][:0:-1]
                    stack = jnp.concatenate(order * cpt, axis=0).astype(BF16)
                    tall = lax.dot_general(eye, stack, _NT, preferred_element_type=F32)
                    sk = pltpu.roll(tall, 0, 1, stride=1, stride_axis=0)
                    skews[k] = sk if skews[k] is None else skews[k] + sk
                hi, mid, lo = _split3(jnp.log(f))
                m_cs = m_bwd if bwd else m_fwd
                cs = dot(hi, m_cs) + dot(mid, m_cs) + dot(lo, m_cs)
                qt_scr[d_idx, :, ls] = q * jnp.exp(cs[:, :LANES])
                kt = (1.0 - f) * jnp.exp(cs[:, LANES:])
                dsum = (lax.dot_general(sel, hi, _NT, preferred_element_type=F32)
                        + lax.dot_general(sel, mid, _NT, preferred_element_type=F32)
                        + lax.dot_general(sel, lo, _NT, preferred_element_type=F32))
                dec_scr[d_idx, ti] = jnp.exp(dsum)
                kbd = jnp.concatenate([kt.astype(BF16)] * cpt, axis=0) * bd_all
                uo = lax.dot_general(vb, kbd, _NT, preferred_element_type=F32)
                u_scr[d_idx, ti] = jnp.where(u_top, uo[:hd], uo[hd:])
            amats = [(sk * same_f).astype(BF16) for sk in skews]
            oi_scr[:, ls] = jnp.concatenate(
                [lax.dot_general(vb[:hd], amats[0], _NT, preferred_element_type=F32),
                 lax.dot_general(vb[hd:], amats[1], _NT, preferred_element_type=F32)], axis=0)
            return c

        lax.fori_loop(0, nt, phase_a, 0, unroll=2)

        def phase_b(ti, st):
            sf, sb = st
            u = u_scr[0, ti]
            dec = dec_scr[0, ti]
            pieces = [None] * cpt
            for c in range(cpt):
                pieces[c] = sf
                sf = dec[c:c + 1, :] * sf + u[:, c * pw:(c + 1) * pw]
            sst_scr[0, ti] = jnp.concatenate(pieces, axis=1).astype(BF16)
            tb = nt - 1 - ti
            u = u_scr[1, tb]
            dec = dec_scr[1, tb]
            for c in range(cpt - 1, -1, -1):
                pieces[c] = sb
                sb = dec[c:c + 1, :] * sb + u[:, c * pw:(c + 1) * pw]
            sst_scr[1, tb] = jnp.concatenate(pieces, axis=1).astype(BF16)
            return sf, sb

        if zero_init:
            s0 = (jnp.zeros((hd, pw), F32), jnp.zeros((hd, pw), F32))
        else:
            s0 = tuple(jnp.concatenate([s0_r[k, 2 * pr], s0_r[k, 2 * pr + 1]], axis=1) for k in range(2))
        s_end = lax.fori_loop(0, nt, phase_b, s0)
        for k in range(2):
            send_o[k, 2 * pr] = s_end[k][:, :hd]
            send_o[k, 2 * pr + 1] = s_end[k][:, hd:]

        def phase_c(ti, c):
            ls = pl.ds(pl.multiple_of(ti * LANES, LANES), LANES)
            o = oi_scr[:, ls]
            for k in range(2):
                t8 = jnp.concatenate([qt_scr[k, :, ls].astype(BF16)] * cpt, axis=0)
                qb = jnp.concatenate([t8 * bd_top, t8 * bd_bot], axis=1)
                oo = dot(sst_scr[k, ti], qb)
                o = o + jnp.concatenate([oo[:, :LANES], oo[:, LANES:]], axis=0)
            o0, o1 = o[:hd], o[hd:]
            on = jnp.concatenate([o0 * lax.rsqrt(jnp.mean(o0 * o0, axis=0, keepdims=True) + EPS),
                                  o1 * lax.rsqrt(jnp.mean(o1 * o1, axis=0, keepdims=True) + EPS)], axis=0)
            on_scr[rows, ls] = on * gain2 * _silu(g_r[rows, ls])
            return c

        lax.fori_loop(0, nt, phase_c, 0, unroll=2)
        return carry

    lax.fori_loop(0, B_HEADS // 2, pair_step, 0)
    ob_o[...] = on_scr[...].T.astype(BF16)


def _hgrn(hgt, col0, n_batch, t, lb_b, gain_b, s0t):
    bw = B_HEADS * HEAD_DIM
    hd, pw = HEAD_DIM, 2 * HEAD_DIM
    nt = t // LANES
    cpt = LANES // GLA_CHUNK
    zero_init = s0t is None
    specs = [pl.BlockSpec((bw, t), functools.partial(lambda r, b: (r, col0 // t + b), r)) for r in range(5)]
    in_specs = specs + [_full_spec(lb_b.shape), _full_spec(gain_b.shape)]
    args = [hgt] * 5 + [lb_b, gain_b]
    st_spec = pl.BlockSpec((None, 2, B_HEADS, HEAD_DIM, HEAD_DIM), lambda b: (b, 0, 0, 0, 0))
    if not zero_init:
        in_specs.append(st_spec)
        args.append(s0t)
    scratch = [pltpu.VMEM((pw, t), F32), pltpu.VMEM((2, nt, hd, cpt * pw), F32), pltpu.VMEM((2, nt, cpt, pw), F32),
               pltpu.VMEM((2, pw, t), F32), pltpu.VMEM((2, nt, hd, cpt * pw), BF16), pltpu.VMEM((bw, t), F32)]
    return pl.pallas_call(
        functools.partial(_hgrn_body, zero_init, t),
        grid=(n_batch,),
        in_specs=in_specs,
        out_specs=[pl.BlockSpec((t, bw), lambda b: (b, 0)), st_spec],
        out_shape=[jax.ShapeDtypeStruct((n_batch * t, bw), BF16),
                   jax.ShapeDtypeStruct((n_batch, 2, B_HEADS, HEAD_DIM, HEAD_DIM), F32)],
        scratch_shapes=scratch,
        compiler_params=_params(1),
        name="hgrn2_context" if zero_init else "hgrn2_latent",
    )(*args)


def _route(logits):
    lane_i = lax.broadcasted_iota(jnp.int32, logits.shape, 1)
    lane = lane_i.astype(F32)
    neg = -jnp.inf
    big = 1e9
    gl = jnp.where(lane_i < MOE_GROUPS, logits, neg)
    gmax = jnp.max(gl, axis=-1, keepdims=True)
    g_val = 1.0 / jnp.sum(jnp.exp(gl - gmax), axis=-1, keepdims=True)
    g_idx = jnp.min(jnp.where(gl == gmax, lane, big), axis=-1, keepdims=True)
    n_e = MOE_GROUPS * EXPERTS_PER_GROUP
    lane_group = ((lane_i - MOE_GROUPS) >> 2).astype(F32)
    in_group = (lane_i >= MOE_GROUPS) & (lane_i < MOE_GROUPS + n_e) & (lane_group == g_idx)
    el = jnp.where(in_group, logits, neg)
    m1 = jnp.max(el, axis=-1, keepdims=True)
    i1 = jnp.min(jnp.where(el == m1, lane, big), axis=-1, keepdims=True)
    el2 = jnp.where(lane == i1, neg, el)
    m2 = jnp.max(el2, axis=-1, keepdims=True)
    i2 = jnp.min(jnp.where(el2 == m2, lane, big), axis=-1, keepdims=True)
    tt = jnp.exp(m2 - m1)
    w1 = g_val / (1.0 + tt)
    w2 = g_val * tt / (1.0 + tt)
    base = MOE_GROUPS + EXPERTS_PER_GROUP * g_idx
    route = jnp.where(lane == i1 - base, w1, 0.0) + jnp.where(lane == i2 - base, w2, 0.0)
    return jnp.where(lane_i == ROUTE_GID_LANE, g_idx, route)


def _proj_res_body(geom, lhs_n, n_x, has_y, *refs):
    lhs = []
    for n_a in lhs_n:
        lhs.append(geom.tok_load(refs[:n_a]))
        refs = refs[n_a:]
    ws, refs = refs[:len(lhs_n)], refs[len(lhs_n):]
    xs, refs = refs[:n_x], refs[n_x:]
    x = geom.tok_load(xs)
    if has_y:
        y, g2, refs = refs[0], refs[1], refs[2:]
        x = x + g2[...] * _tile_rows_load(y, TOKEN_TILE)
    g1, gn2, sh2, sc2, wrh, wrl, br, xnew_o, xr_o, route_o = refs[:10]
    wbs = refs[10:]

    @pl.when(pl.program_id(0) == 0)
    def _():
        for w, wb in zip(ws, wbs):
            wb[...] = w[...].astype(BF16)

    acc = jnp.dot(lhs[0], wbs[0][...], preferred_element_type=F32)
    for a, wb in zip(lhs[1:], wbs[1:]):
        acc = acc + jnp.dot(a, wb[...], preferred_element_type=F32)
    xn = x + g1[...] * acc
    xnew_o[...] = xn
    hm = _modulate(xn, gn2[...], sh2[...], sc2[...])
    hi = hm.astype(BF16)
    lo = (hm - hi.astype(F32)).astype(BF16)
    logits = (jnp.dot(hi, wrh[...], preferred_element_type=F32) + jnp.dot(lo, wrh[...], preferred_element_type=F32)
              + jnp.dot(hi, wrl[...], preferred_element_type=F32)) + br[...]
    route = _route(logits)
    for s, piece in enumerate(_pack_routed(hm, route)):
        xr_o[pl.ds(s, TOKEN_TILE, stride=ROUTED_SUB), :] = piece
    route_o[...] = route


def _proj_res(geom, mod, layer, lhs, ws, xa, y, gn2, wrh, wrl, br, name):
    tm = TOKEN_TILE
    d = geom.d
    has_y = y is not None
    tok = pl.BlockSpec((tm, d), lambda i: (i, 0))
    in_specs, args = [], []
    for a in lhs:
        s, ar = geom.tok_specs(a)
        in_specs += s
        args += ar
    w_shapes = []
    for w_all, lead, row_block in ws:
        in_specs.append(_resident_slab(w_all.shape, lead, row_block))
        args.append(w_all)
        w_shapes.append((w_all.shape[1] if row_block is None else row_block[1],) + tuple(w_all.shape[2:]))
    s, ar = geom.tok_specs(xa)
    in_specs += s
    args += ar
    if has_y:
        in_specs += [_Y_SPEC, geom.mod_spec(layer - 1, 5)]
        args += [y, mod]
    in_specs += [geom.mod_spec(layer, 2), _layer_spec(gn2.shape, layer), geom.mod_spec(layer, 3),
                 geom.mod_spec(layer, 4), _layer_spec(wrh.shape, layer), _layer_spec(wrl.shape, layer),
                 _layer_spec(br.shape, layer)]
    args += [mod, gn2, mod, mod, wrh, wrl, br]
    return pl.pallas_call(
        functools.partial(_proj_res_body, geom, tuple(_n_tok(a) for a in lhs), _n_tok(xa), has_y),
        grid=(geom.n // tm,),
        in_specs=in_specs,
        out_specs=[tok, pl.BlockSpec((tm * ROUTED_SUB, LANES), lambda i: (i, 0)),
                   pl.BlockSpec((tm, ROUTE_W), lambda i: (i, 0))],
        out_shape=[jax.ShapeDtypeStruct((geom.n, d), F32), jax.ShapeDtypeStruct((geom.n * ROUTED_SUB, LANES), F32),
                   jax.ShapeDtypeStruct((geom.n, ROUTE_W), F32)],
        scratch_shapes=[pltpu.VMEM(shp, BF16) for shp in w_shapes],
        compiler_params=_params(1),
        name=name,
    )(*args)


def _dispatch(route, n, tile):
    n_tiles = n // tile + MOE_GROUPS
    n_slots = n_tiles * tile
    gid = route[:, ROUTE_GID_LANE].astype(jnp.int32)
    onehot = (gid[:, None] == jnp.arange(MOE_GROUPS, dtype=jnp.int32)[None, :]).astype(jnp.int32)
    csum = jnp.cumsum(onehot, axis=0)
    counts = csum[-1]
    rank = jnp.sum(csum * onehot, axis=1) - 1
    tiles_g = (counts + tile - 1) // tile
    tile_end = jnp.cumsum(tiles_g)
    tile_start = tile_end - tiles_g
    dest = (tile_start * tile)[gid] + rank
    tok = jnp.arange(n, dtype=jnp.int32)
    src = jnp.zeros((n_slots,), jnp.int32).at[dest].set(tok)
    t_idx = jnp.arange(n_tiles, dtype=jnp.int32)
    tile_gid = jnp.minimum(jnp.sum((t_idx[:, None] >= tile_end[None, :]).astype(jnp.int32), axis=1), MOE_GROUPS - 1)
    tile_rows = jnp.clip(counts[tile_gid] - (t_idx - tile_start[tile_gid]) * tile, 0, tile)
    tile_rows = jnp.where(t_idx < tile_end[-1], tile_rows, 0)
    in_tile = jnp.arange(tile, dtype=jnp.int32)
    valid = (in_tile[None, :] < tile_rows[:, None]).reshape(n_slots)
    spare = (n + (t_idx[:, None] & 1) * tile + in_tile[None, :]).reshape(n_slots)
    dst = jnp.concatenate([n + tile + in_tile, jnp.where(valid, src, spare)])
    return tile_gid, src, dst, tile_end[-1:].astype(jnp.int32), n_tiles


def _moe_body(tile, d, n_rows, tg_ref, src_ref, dst_ref, nu_ref, xr_hbm, w1_ref, w3_ref, w2_ref, y_hbm,
              xbuf, ybuf, w1b, w3b, w2b, isem, osem):
    t = pl.program_id(0)
    last = nu_ref[0] - 1
    slot = t % 2
    other = 1 - slot

    sub = SUBLANES
    rsub = ROUTED_SUB

    def hbm_rows(ref, row, n_sub):
        start = row * n_sub if isinstance(row, int) else pl.multiple_of(row * n_sub, n_sub)
        return ref.at[pl.ds(start, n_sub), :]

    def row_in(tile_idx, s, i):
        row = src_ref[tile_idx * tile + i]
        return pltpu.make_async_copy(hbm_rows(xr_hbm, row, rsub), xbuf.at[s, pl.ds(i * rsub, rsub), :], isem.at[s])

    def row_out(ext_tile_idx, s, i):
        row = dst_ref[ext_tile_idx * tile + i]
        return pltpu.make_async_copy(ybuf.at[s, pl.ds(i * sub, sub), :], hbm_rows(y_hbm, row, sub), osem.at[s])

    @pl.when(t == 0)
    def _():
        ybuf[...] = jnp.zeros(ybuf.shape, F32)
        for i in range(tile):
            row_in(0, 0, i).start(priority=i % 2)
        for i in range(tile):
            pltpu.make_async_copy(ybuf.at[0, pl.ds(i * sub, sub), :], hbm_rows(y_hbm, n_rows + i, sub),
                                  osem.at[0]).start(priority=i % 2)

    g = tg_ref[t]
    g_prev = tg_ref[jnp.maximum(t - 1, 0)]

    @pl.when(((t == 0) | (g != g_prev)) & (t <= last))
    def _():
        w1b[...] = w1_ref[...].astype(BF16)
        w3b[...] = w3_ref[...].astype(BF16)
        w2b[...] = w2_ref[...].astype(BF16)

    nxt = jnp.minimum(t + 1, last)
    n_parts = 4 * EXPERTS_PER_GROUP
    per = tile // n_parts

    def step(slot, other):
        for i in range(tile):
            row_in(0, slot, i).wait()
        part = [0]

        def issue_part():
            k = part[0]
            part[0] += 1
            for i in range(k * per, (k + 1) * per):
                row_in(nxt, other, i).start(priority=i % 2)
                row_out(t, other, i).start(priority=i % 2)

        xb, cw = _unpack_routed(xbuf.at[slot], tile)
        acc = jnp.zeros((tile, d), F32)
        for e in range(EXPERTS_PER_GROUP):
            a = jnp.dot(xb, w1b[e], preferred_element_type=F32)
            issue_part()
            b = jnp.dot(xb, w3b[e], preferred_element_type=F32)
            issue_part()
            hid = (_silu(a) * b * cw[:, e:e + 1]).astype(BF16)
            issue_part()
            acc = acc + jnp.dot(hid, w2b[e], preferred_element_type=F32)
            issue_part()
        for i in range(tile):
            row_out(0, slot, i).wait()
        _tile_rows_store(ybuf.at[slot], acc)

        @pl.when(t == last)
        def _():
            for i in range(tile):
                row_in(0, other, i).wait()
            for i in range(tile):
                row_out(0, other, i).wait()
            for i in range(tile):
                row_out(t + 1, slot, i).start(priority=i % 2)
            for i in range(tile):
                row_out(0, slot, i).wait()

    @pl.when((slot == 0) & (t <= last))
    def _():
        step(0, 1)

    @pl.when((slot == 1) & (t <= last))
    def _():
        step(1, 0)


def _moe(xr, route, w1, w3, w2, layer, n, d):
    tile = MOE_TILE
    assert d == SUBLANES * LANES
    tile_gid, src, dst, n_used, n_tiles = _dispatch(route, n, tile)
    g0 = layer * MOE_GROUPS
    wspec13 = pl.BlockSpec((None,) + w1.shape[1:], lambda t, tg, s, ds_, nu: (g0 + tg[t], 0, 0, 0))
    wspec2 = pl.BlockSpec((None,) + w2.shape[1:], lambda t, tg, s, ds_, nu: (g0 + tg[t], 0, 0, 0))
    grid_spec = pltpu.PrefetchScalarGridSpec(
        num_scalar_prefetch=4,
        grid=(n_tiles,),
        in_specs=[pl.BlockSpec(memory_space=pl.ANY), wspec13, wspec13, wspec2],
        out_specs=pl.BlockSpec(memory_space=pl.ANY),
        scratch_shapes=[pltpu.VMEM((2, tile * ROUTED_SUB, LANES), F32), pltpu.VMEM((2, tile * SUBLANES, LANES), F32),
                        pltpu.VMEM(w1.shape[1:], BF16), pltpu.VMEM(w3.shape[1:], BF16), pltpu.VMEM(w2.shape[1:], BF16),
                        pltpu.SemaphoreType.DMA((2,)), pltpu.SemaphoreType.DMA((2,))],
    )
    return pl.pallas_call(
        functools.partial(_moe_body, tile, d, n),
        grid_spec=grid_spec,
        out_shape=jax.ShapeDtypeStruct(((n + 2 * tile) * SUBLANES, LANES), F32),
        compiler_params=_params(1),
        name="moe_ffn",
    )(tile_gid, src, dst, n_used, xr, w1, w3, w2)


def _glu_body(xa, y, g2, gn, sh, sc, w, u_o, wb):
    @pl.when(pl.program_id(0) == 0)
    def _():
        wb[...] = w[...].astype(BF16)

    x = xa[...] + g2[...] * _tile_rows_load(y, TOKEN_TILE)
    h = _modulate(x, gn[...], sh[...], sc[...]).astype(BF16)
    ab = jnp.dot(h, wb[...], preferred_element_type=F32)
    c = ab.shape[1] // 2
    u_o[...] = ab[:, :c] * jax.nn.sigmoid(ab[:, c:])


def _glu(geom, mod, layer, xa, y, gn, w_all, o):
    tm = TOKEN_TILE
    d = geom.d
    tok = pl.BlockSpec((tm, d), lambda i: (i, 0))
    c = w_all.shape[2] // 2
    return pl.pallas_call(
        _glu_body,
        grid=(geom.n // tm,),
        in_specs=[tok, _Y_SPEC, geom.mod_spec(layer - 1, 5), _row_spec(d), geom.mod_spec(layer, 0),
                  geom.mod_spec(layer, 1), _resident_slab(w_all.shape, o)],
        out_specs=pl.BlockSpec((tm, c), lambda i: (i, 0)),
        out_shape=jax.ShapeDtypeStruct((geom.n, c), F32),
        scratch_shapes=[pltpu.VMEM(w_all.shape[1:], BF16)],
        compiler_params=_params(1),
        name="conv_glu",
    )(xa, y, mod, gn, mod, mod, w_all)


def _dwconv_body(t, u_r, dw_r, b_r, g_r, be_r, o_r, pad_scr, cv_scr, sh_scr):
    c = u_r.shape[1]
    halo = 16
    sh_len = sh_scr.shape[1]
    pad_scr[0:halo, :] = jnp.zeros((halo, c), F32)
    pad_scr[t + halo:t + 2 * halo, :] = jnp.zeros((halo, c), F32)
    pad_scr[halo:t + halo, :] = u_r[...]
    rb_n = CONV_ROWS
    n_lb = c // LANES
    first = halo - CONV_WIDTH // 2

    def blk(lb, carry):
        ls = pl.ds(pl.multiple_of(lb * LANES, LANES), LANES)
        bias = b_r[:, ls]
        for r0 in range(0, t, rb_n):
            win = pad_scr[r0:r0 + rb_n + 2 * halo, ls]
            for sft in range(1, SUBLANES):
                sh_scr[sft] = win[sft:sft + sh_len, :]
            acc = jnp.zeros((rb_n, LANES), F32)
            for k in range(CONV_WIDTH):
                sft = (first + k) % SUBLANES
                a = first + k - sft
                src = pad_scr[r0 + a:r0 + a + rb_n, ls] if sft == 0 else sh_scr[sft, a:a + rb_n, :]
                acc = acc + src * dw_r[k:k + 1, ls]
            cv_scr[r0:r0 + rb_n, ls] = acc + bias
        return carry

    lax.fori_loop(0, n_lb, blk, 0)

    def ln(rb, carry):
        rs = pl.ds(pl.multiple_of(rb * rb_n, rb_n), rb_n)
        x = cv_scr[rs, :]
        mu = jnp.mean(x, axis=-1, keepdims=True)
        xc = x - mu
        var = jnp.mean(xc * xc, axis=-1, keepdims=True)
        yv = xc * lax.rsqrt(var + EPS) * g_r[...] + be_r[...]
        o_r[rs, :] = _silu(yv).astype(BF16)
        return carry

    lax.fori_loop(0, t // rb_n, ln, 0, unroll=2)


def _dwconv(u, row0, n_batch, t, dw, b, g, be):
    c = u.shape[1]
    return pl.pallas_call(
        functools.partial(_dwconv_body, t),
        grid=(n_batch,),
        in_specs=[pl.BlockSpec((t, c), lambda bi: (row0 // t + bi, 0)), _full_spec(dw.shape),
                  _row_spec(c), _row_spec(c), _row_spec(c)],
        out_specs=pl.BlockSpec((t, c), lambda bi: (bi, 0)),
        out_shape=jax.ShapeDtypeStruct((n_batch * t, c), BF16),
        scratch_shapes=[pltpu.VMEM((t + 32, c), F32), pltpu.VMEM((t, c), F32),
                        pltpu.VMEM((SUBLANES, CONV_ROWS + (CONV_WIDTH // SUBLANES) * SUBLANES, LANES), F32)],
        compiler_params=_params(1),
        name="conv_dw",
    )(u, dw, b, g, be)


def _final_body(ctx_tiles, xa, y, g2, gf, oc, os_):
    x = xa[...] + g2[...] * _tile_rows_load(y, TOKEN_TILE)
    ms = jnp.mean(x * x, axis=-1, keepdims=True)
    val = x * lax.rsqrt(ms + EPS) * gf[...]
    i = pl.program_id(0)

    @pl.when(i < ctx_tiles)
    def _():
        oc[...] = val

    @pl.when(i >= ctx_tiles)
    def _():
        os_[...] = val


def _final(geom, mod, depth, xa, y, gf):
    tm = TOKEN_TILE
    d = geom.d
    ct = geom.ctx_tiles
    tok = pl.BlockSpec((tm, d), lambda i: (i, 0))
    return pl.pallas_call(
        functools.partial(_final_body, ct),
        grid=(geom.n // tm,),
        in_specs=[tok, _Y_SPEC, geom.mod_spec(depth - 1, 5), _row_spec(d)],
        out_specs=[pl.BlockSpec((tm, d), lambda i: (jnp.minimum(i, ct - 1), 0)),
                   pl.BlockSpec((tm, d), lambda i: (jnp.maximum(i - ct, 0), 0))],
        out_shape=[jax.ShapeDtypeStruct((geom.n_ctx, d), F32), jax.ShapeDtypeStruct((geom.n_smp, d), F32)],
        compiler_params=_params(1),
        name="final_norm",
    )(xa, y, mod, gf)


def _rope_tables(n_tokens):
    rows = n_tokens // GRID_W
    row = jnp.repeat(jnp.arange(rows, dtype=F32), GRID_W)
    col = jnp.tile(jnp.arange(GRID_W, dtype=F32), rows)
    half = HEAD_DIM // 2
    inv = ROPE_THETA ** (-jnp.arange(0, half, 2, dtype=F32) / half)
    ang_r = row[:, None] * inv[None, :]
    ang_c = col[:, None] * inv[None, :]
    cr, sr, cc, sc = jnp.cos(ang_r), jnp.sin(ang_r), jnp.cos(ang_c), jnp.sin(ang_c)
    cos = jnp.concatenate([cr, cr, cc, cc], axis=1)
    sin = jnp.concatenate([-sr, sr, -sc, sc], axis=1)
    return (jnp.tile(cos, (1, A_HEADS)), jnp.tile(sin, (1, A_HEADS)),
            jnp.tile(cos, (1, A_KV_HEADS)), jnp.tile(sin, (1, A_KV_HEADS)))


def _router_weights(w_group, b_group, w_expert, b_expert):
    depth, d, _ = w_group.shape
    n_used = MOE_GROUPS + MOE_GROUPS * EXPERTS_PER_GROUP
    w = jnp.concatenate([w_group, w_expert, jnp.zeros((depth, d, ROUTE_W - n_used), F32)], axis=2)
    b = jnp.concatenate([b_group, b_expert, jnp.zeros((depth, ROUTE_W - n_used), F32)], axis=1)[:, None, :]
    hi = w.astype(BF16)
    lo = (w - hi.astype(F32)).astype(BF16)
    return hi, lo, b


def kernel(x_prompt, x_sample, cache_k, cache_v, state_hgrn, c, c_ctx, ada_w, ada_b, norm1_g, norm2_g, mix_w_in, mix_w_out, attn_q_gain, attn_k_gain, hgrn_lower_bound, hgrn_out_gain, conv_pw1, conv_dw, conv_dw_b, conv_ln_g, conv_ln_b, conv_pw2, moe_w_group, moe_b_group, moe_w_expert, moe_b_expert, moe_w1, moe_w3, moe_w2, final_g):
    batch, seq, d = x_prompt.shape
    dec_batch, dec_seq, _ = x_sample.shape
    depth = ada_w.shape[0]
    past = cache_k.shape[2]
    n_ctx, n_smp = batch * seq, dec_batch * dec_seq
    n = n_ctx + n_smp
    geom = _Geom(n_ctx, n_smp, dec_seq, d)
    qw, kw, bw = A_HEADS * HEAD_DIM, A_KV_HEADS * HEAD_DIM, B_HEADS * HEAD_DIM
    n_nat = qw + 2 * kw

    cond8 = jnp.concatenate([c_ctx[None, :], c, jnp.zeros((8 - 1 - dec_batch, d), F32)], axis=0)
    mod = _ada(cond8, ada_w, ada_b).reshape(depth, 8, 6, 1, d)

    lb_soft = jax.nn.softmax(hgrn_lower_bound.astype(F32), axis=0)
    lower_bounds = jnp.cumsum(lb_soft, axis=0) - lb_soft[0]
    rope_tabs = _rope_tables(dec_seq)

    xa = (x_prompt.reshape(n_ctx, d), x_sample.reshape(n_smp, d))
    y = None
    f = moe_w1.shape[-1]
    w1 = moe_w1.reshape(depth * MOE_GROUPS, EXPERTS_PER_GROUP, d, f)
    w3 = moe_w3.reshape(depth * MOE_GROUPS, EXPERTS_PER_GROUP, d, f)
    w2 = moe_w2.reshape(depth * MOE_GROUPS, EXPERTS_PER_GROUP, f, d)
    n_ab = cache_k.shape[1]
    ck = cache_k.reshape(dec_batch, n_ab, past, kw)
    cv = cache_v.reshape(dec_batch, n_ab, past, kw)
    wrh, wrl, br = _router_weights(moe_w_group, moe_b_group, moe_w_expert, moe_b_expert)
    gn2 = norm2_g[:, None, :]
    new_k, new_v, new_s = [], [], []
    for l in range(depth):
        gn1 = norm1_g[l][None, :]
        if l % 2 == 0:
            e = l // 2
            qkv, hgt = _inproj(geom, mod, l, xa, y, gn1, mix_w_in, e, n_nat)
            qg = jnp.tile(attn_q_gain[e], A_HEADS)[None, :]
            kg = jnp.tile(attn_k_gain[e], A_KV_HEADS)[None, :]
            oa_c, khat = _attention(qkv, 0, batch, seq, qg, kg)
            oa_s = _attention(qkv, n_ctx, dec_batch, dec_seq, qg, kg, cache=(ck, cv, e), rope_tabs=rope_tabs)
            lb_b = jnp.broadcast_to(lower_bounds[e][:, :, None], (2, bw, LANES))
            gain_b = jnp.broadcast_to(hgrn_out_gain[e][:, None], (HEAD_DIM, LANES))
            ob_c, s_ctx = _hgrn(hgt, 0, batch, seq, lb_b, gain_b, None)
            s0t = jnp.swapaxes(state_hgrn[:, e], -1, -2)
            ob_s, _ = _hgrn(hgt, n_ctx, dec_batch, dec_seq, lb_b, gain_b, s0t)
            w_halves = [(mix_w_out, e, (0, qw)), (mix_w_out, e, (1, bw))]
            xnew, xr, route = _proj_res(geom, mod, l, [(oa_c, oa_s), (ob_c, ob_s)], w_halves, xa, y,
                                        gn2, wrh, wrl, br, "ab_outproj")
            new_k.append(khat.reshape(batch, seq, A_KV_HEADS, HEAD_DIM))
            new_v.append(qkv[:n_ctx, qw + kw:].reshape(batch, seq, A_KV_HEADS, HEAD_DIM))
            new_s.append(jnp.swapaxes(s_ctx, -1, -2))
        else:
            o = l // 2
            u = _glu(geom, mod, l, xa, y, gn1, conv_pw1, o)
            cargs = (conv_dw[o], conv_dw_b[o][None, :], conv_ln_g[o][None, :], conv_ln_b[o][None, :])
            c_c = _dwconv(u, 0, batch, seq, *cargs)
            c_s = _dwconv(u, n_ctx, dec_batch, dec_seq, *cargs)
            xnew, xr, route = _proj_res(geom, mod, l, [(c_c, c_s)], [(conv_pw2, o, None)], xa, y, gn2, wrh, wrl, br,
                                        "conv_outproj")
        y = _moe(xr, route, w1, w3, w2, l, n, d)
        xa = xnew
    y_prompt, y_sample = _final(geom, mod, depth, xa, y, final_g[None, :])
    return (y_prompt.reshape(batch, seq, d), y_sample.reshape(dec_batch, dec_seq, d),
            jnp.stack(new_k, axis=1), jnp.stack(new_v, axis=1), jnp.stack(new_s, axis=1))
```

```python
import functools

import jax
import jax.numpy as jnp
from jax import lax
from jax.experimental import pallas as pl
from jax.experimental.pallas import tpu as pltpu

F32 = jnp.float32
BF16 = jnp.bfloat16

HEAD_DIM = 64
A_HEADS = 8
A_KV_HEADS = 2
A_GROUP = A_HEADS // A_KV_HEADS
B_HEADS = 8
GRID_W = 64
ROPE_THETA = 10000.0
GLA_CHUNK = 16
CONV_WIDTH = 31
MOE_GROUPS = 4
EXPERTS_PER_GROUP = 4
EPS = 1e-6

LANES = 128
SUBLANES = 8
VMEM_LIMIT = 56 * 1024 * 1024
TOKEN_TILE = 512
MOE_TILE = 256
ATTN_Q_TILE = 256
ROUTE_W = LANES
ROUTE_GID_LANE = 8
CONV_ROWS = 64

_NT = (((1,), (1,)), ((), ()))


def _params(n_axes):
    return pltpu.CompilerParams(dimension_semantics=("arbitrary",) * n_axes, vmem_limit_bytes=VMEM_LIMIT)


def _modulate(x, g, sh, sc):
    ms = jnp.mean(x * x, axis=-1, keepdims=True)
    return (x * lax.rsqrt(ms + EPS) * g) * (1.0 + sc) + sh


def _silu(x):
    return x * jax.nn.sigmoid(x)


def _tile_rows_load(ref, n_rows):
    return jnp.concatenate([ref[pl.ds(s, n_rows, stride=SUBLANES), :] for s in range(SUBLANES)], axis=1)


def _tile_rows_store(ref, val):
    n_rows = val.shape[0]
    for s in range(SUBLANES):
        ref[pl.ds(s, n_rows, stride=SUBLANES), :] = val[:, s * LANES:(s + 1) * LANES]


ROUTED_SUB = 2 * SUBLANES


def _pack_routed(hm, route):
    pieces = [hm[:, s * LANES:(s + 1) * LANES] for s in range(SUBLANES)] + [route]
    return pieces + [jnp.zeros(route.shape, F32)] * (ROUTED_SUB - len(pieces))


def _unpack_routed(ref, n_rows):
    chunks = [ref[pl.ds(s, n_rows, stride=ROUTED_SUB), :] for s in range(SUBLANES)]
    route = ref[pl.ds(SUBLANES, n_rows, stride=ROUTED_SUB), :]
    return jnp.concatenate(chunks, axis=1).astype(BF16), route


def _split3(x):
    hi = x.astype(BF16)
    r1 = x - hi.astype(F32)
    mid = r1.astype(BF16)
    lo = (r1 - mid.astype(F32)).astype(BF16)
    return hi, mid, lo


def _ada_body(c_ref, w_ref, b_ref, o_ref):
    s = _silu(c_ref[...])
    o_ref[...] = jnp.dot(s.astype(BF16), w_ref[...].astype(BF16), preferred_element_type=F32) + b_ref[...]


def _ada(cond8, ada_w, ada_b):
    depth, d, d6 = ada_w.shape
    tn = d6 // 4
    return pl.pallas_call(
        _ada_body,
        grid=(depth, d6 // tn),
        in_specs=[
            pl.BlockSpec((8, d), lambda l, j: (0, 0)),
            pl.BlockSpec((None, d, tn), lambda l, j: (l, 0, j)),
            pl.BlockSpec((None, 1, tn), lambda l, j: (l, 0, j)),
        ],
        out_specs=pl.BlockSpec((None, 8, tn), lambda l, j: (l, 0, j)),
        out_shape=jax.ShapeDtypeStruct((depth, 8, d6), F32),
        compiler_params=_params(2),
        name="adaln",
    )(cond8, ada_w, ada_b.reshape(depth, 1, d6))


class _Geom:
    def __init__(self, n_ctx, n_smp, smp_len, d):
        self.n_ctx, self.n_smp, self.smp_len, self.d = n_ctx, n_smp, smp_len, d
        self.n = n_ctx + n_smp
        self.ctx_tiles = n_ctx // TOKEN_TILE

    def mod_spec(self, layer, j):
        ctx_tiles = self.ctx_tiles
        per_smp = self.smp_len // TOKEN_TILE

        def idx(i):
            row = jnp.where(i < ctx_tiles, 0, 1 + (i - ctx_tiles) // per_smp)
            return (layer, row, j, 0, 0)

        return pl.BlockSpec((None, None, None, 1, self.d), idx)

    def tok_specs(self, a):
        tm = TOKEN_TILE
        if isinstance(a, tuple):
            ct = self.ctx_tiles
            w = a[0].shape[1]
            return ([pl.BlockSpec((tm, w), lambda i: (jnp.minimum(i, ct - 1), 0)),
                     pl.BlockSpec((tm, w), lambda i: (jnp.maximum(i - ct, 0), 0))], list(a))
        return [pl.BlockSpec((tm, a.shape[1]), lambda i: (i, 0))], [a]

    def tok_load(self, refs):
        if len(refs) == 1:
            return refs[0][...]
        return jnp.where(pl.program_id(0) < self.ctx_tiles, refs[0][...], refs[1][...])


def _n_tok(a):
    return 2 if isinstance(a, tuple) else 1


def _row_spec(d):
    return pl.BlockSpec((1, d), lambda i: (0, 0))


def _layer_spec(shape, lead):
    return pl.BlockSpec((None,) + tuple(shape[1:]), lambda *_: (lead,) + (0,) * (len(shape) - 1))


def _resident_slab(shape, lead, row_block=None):
    rows = shape[1] if row_block is None else row_block[1]
    rb = 0 if row_block is None else row_block[0]
    return pl.BlockSpec((None, rows) + tuple(shape[2:]), lambda *_: (lead, rb) + (0,) * (len(shape) - 2),
                        pipeline_mode=pl.Buffered(1))


def _full_spec(shape):
    nd = len(shape)
    return pl.BlockSpec(shape, lambda *_: (0,) * nd)


_Y_SPEC = pl.BlockSpec((TOKEN_TILE * SUBLANES, LANES), lambda i: (i, 0))


def _inproj_body(geom, n_x, has_y, n_nat, *refs):
    xs, refs = refs[:n_x], refs[n_x:]
    x = geom.tok_load(xs)
    if has_y:
        y, g2, refs = refs[0], refs[1], refs[2:]
        x = x + g2[...] * _tile_rows_load(y, TOKEN_TILE)
    gn, sh, sc, w, qkv_o, hgt_o, wn_s, wt_s = refs
    step = 512
    n_t = wt_s.shape[0]

    @pl.when(pl.program_id(0) == 0)
    def _():
        wn_s[...] = w[:, :n_nat].astype(BF16)
        for r in range(0, n_t, step):
            wt_s[r:r + step, :] = w[:, n_nat + r:n_nat + r + step].T.astype(BF16)

    h = _modulate(x, gn[...], sh[...], sc[...]).astype(BF16)
    qkv_o[...] = jnp.dot(h, wn_s[...], preferred_element_type=F32)
    for r in range(0, n_t, step):
        hgt_o[r:r + step, :] = lax.dot_general(wt_s[r:r + step, :], h, _NT, preferred_element_type=F32)


def _inproj(geom, mod, layer, xa, y, gn, w_in_all, e, n_nat):
    tm = TOKEN_TILE
    d = geom.d
    n_t = w_in_all.shape[2] - n_nat
    has_y = y is not None
    in_specs, args = geom.tok_specs(xa)
    if has_y:
        in_specs += [_Y_SPEC, geom.mod_spec(layer - 1, 5)]
        args += [y, mod]
    in_specs += [_row_spec(d), geom.mod_spec(layer, 0), geom.mod_spec(layer, 1), _resident_slab(w_in_all.shape, e)]
    args += [gn, mod, mod, w_in_all]
    return pl.pallas_call(
        functools.partial(_inproj_body, geom, _n_tok(xa), has_y, n_nat),
        grid=(geom.n // tm,),
        in_specs=in_specs,
        out_specs=[pl.BlockSpec((tm, n_nat), lambda i: (i, 0)), pl.BlockSpec((n_t, tm), lambda i: (0, i))],
        out_shape=[jax.ShapeDtypeStruct((geom.n, n_nat), F32), jax.ShapeDtypeStruct((n_t, geom.n), F32)],
        scratch_shapes=[pltpu.VMEM((d, n_nat), BF16), pltpu.VMEM((n_t, d), BF16)],
        compiler_params=_params(1),
        name="ab_inproj",
    )(*args)


def _head_norm(x, gain_row, n_heads):
    outs = []
    for h in range(n_heads):
        xh = x[:, h * HEAD_DIM:(h + 1) * HEAD_DIM]
        ms = jnp.mean(xh * xh, axis=-1, keepdims=True)
        outs.append(xh * lax.rsqrt(ms + EPS))
    return jnp.concatenate(outs, axis=1) * gain_row


def _rope(x, cos, sin_signed):
    w = x.shape[1]
    lane = lax.broadcasted_iota(jnp.int32, (1, w), 1)
    first_half = (lane & 16) == 0
    partner = jnp.where(first_half, pltpu.roll(x, w - 16, 1), pltpu.roll(x, 16, 1))
    return x * cos + partner * sin_signed


def _attn_body(rope, tq, *refs):
    if rope:
        q_r, k_r, v_r, ck_r, cv_r, qg_r, kg_r, cq_r, sq_r, ckk_r, skk_r, o_r, keys_s, vals_s = refs
    else:
        q_r, k_r, v_r, qg_r, kg_r, o_r, kh_r, keys_s, vals_s = refs

    @pl.when(pl.program_id(1) == 0)
    def _():
        kn = _head_norm(k_r[...], kg_r[...], A_KV_HEADS)
        if rope:
            kn = _rope(kn, ckk_r[...], skk_r[...])
            keys = jnp.concatenate([ck_r[...], kn], axis=0)
            vals = jnp.concatenate([cv_r[...], v_r[...]], axis=0)
        else:
            kh_r[...] = kn
            keys, vals = kn, v_r[...]
        keys_s[...] = keys.astype(BF16)
        ones = jnp.ones((keys.shape[0], HEAD_DIM), BF16)
        for j in range(A_KV_HEADS):
            vals_s[j] = jnp.concatenate([vals[:, j * HEAD_DIM:(j + 1) * HEAD_DIM].astype(BF16), ones], axis=1)

    qn = _head_norm(q_r[...], qg_r[...], A_HEADS)
    if rope:
        qn = _rope(qn, cq_r[...], sq_r[...])
    qn = qn * (HEAD_DIM ** -0.5)
    keys = keys_s[...]
    outs = [None] * A_HEADS
    for j in range(A_KV_HEADS):
        kj = keys[:, j * HEAD_DIM:(j + 1) * HEAD_DIM]
        vj = vals_s[j]
        heads = [A_GROUP * j + g for g in range(A_GROUP)]
        q4 = jnp.concatenate([qn[:, h * HEAD_DIM:(h + 1) * HEAD_DIM] for h in heads], axis=0).astype(BF16)
        s = lax.dot_general(q4, kj, _NT, preferred_element_type=F32)
        m = jnp.max(s, axis=-1, keepdims=True)
        p = jnp.exp(s - m).astype(BF16)
        ol = jnp.dot(p, vj, preferred_element_type=F32)
        o = ol[:, :HEAD_DIM] / ol[:, HEAD_DIM:HEAD_DIM + 1]
        for g, h in enumerate(heads):
            outs[h] = o[g * tq:(g + 1) * tq, :]
    o_r[...] = jnp.concatenate(outs, axis=1).astype(BF16)


def _attention(qkv, row0, n_batch, t, q_gain, k_gain, cache=None, rope_tabs=None):
    tq = ATTN_Q_TILE
    nq = t // tq
    qw, kw = A_HEADS * HEAD_DIM, A_KV_HEADS * HEAD_DIM
    rope = cache is not None
    q_spec = pl.BlockSpec((tq, qw), lambda b, qi: (row0 // tq + b * nq + qi, 0))
    k_spec = pl.BlockSpec((t, kw), lambda b, qi: (row0 // t + b, qw // kw))
    v_spec = pl.BlockSpec((t, kw), lambda b, qi: (row0 // t + b, qw // kw + 1))
    gq = pl.BlockSpec((1, qw), lambda b, qi: (0, 0))
    gk = pl.BlockSpec((1, kw), lambda b, qi: (0, 0))
    out_o = pl.BlockSpec((tq, qw), lambda b, qi: (b * nq + qi, 0))
    o_shape = jax.ShapeDtypeStruct((n_batch * t, qw), BF16)
    if rope:
        ck, cv, e = cache
        past = ck.shape[2]
        c_spec = pl.BlockSpec((None, None, past, kw), lambda b, qi: (b, e, 0, 0))
        cq, sq, ckk, skk = rope_tabs
        in_specs = [q_spec, k_spec, v_spec, c_spec, c_spec, gq, gk,
                    pl.BlockSpec((tq, qw), lambda b, qi: (qi, 0)), pl.BlockSpec((tq, qw), lambda b, qi: (qi, 0)),
                    pl.BlockSpec((t, kw), lambda b, qi: (0, 0)), pl.BlockSpec((t, kw), lambda b, qi: (0, 0))]
        args = [qkv, qkv, qkv, ck, cv, q_gain, k_gain, cq, sq, ckk, skk]
        out_specs, out_shape = out_o, o_shape
    else:
        in_specs = [q_spec, k_spec, v_spec, gq, gk]
        args = [qkv, qkv, qkv, q_gain, k_gain]
        out_specs = [out_o, pl.BlockSpec((t, kw), lambda b, qi: (b, 0))]
        out_shape = [o_shape, jax.ShapeDtypeStruct((n_batch * t, kw), F32)]
        past = 0
    n_keys = past + t
    return pl.pallas_call(
        functools.partial(_attn_body, rope, tq),
        grid=(n_batch, nq),
        in_specs=in_specs,
        out_specs=out_specs,
        out_shape=out_shape,
        scratch_shapes=[pltpu.VMEM((n_keys, kw), BF16), pltpu.VMEM((A_KV_HEADS, n_keys, 2 * HEAD_DIM), BF16)],
        compiler_params=_params(2),
        name="gqa_latent" if rope else "gqa_context",
    )(*args)


def _hgrn_body(zero_init, t, *refs):
    if zero_init:
        (q_r, ff_r, fb_r, v_r, g_r, lb_r, gain_r, ob_o, send_o,
         oi_scr, u_scr, dec_scr, qt_scr, sst_scr, on_scr) = refs
        s0_r = None
    else:
        (q_r, ff_r, fb_r, v_r, g_r, lb_r, gain_r, s0_r, ob_o, send_o,
         oi_scr, u_scr, dec_scr, qt_scr, sst_scr, on_scr) = refs
    nt = t // LANES
    hd = HEAD_DIM
    pw = 2 * hd
    cpt = LANES // GLA_CHUNK
    lane = lax.broadcasted_iota(jnp.int32, (1, LANES), 1)
    pos = lane & (GLA_CHUNK - 1)
    ri = lax.broadcasted_iota(jnp.int32, (LANES, LANES), 0)
    ci = lax.broadcasted_iota(jnp.int32, (LANES, LANES), 1)
    same = (ri >> 4) == (ci >> 4)
    same_f = jnp.where(same, 1.0, 0.0)

    def ind(cond):
        return jnp.where(cond, 1.0, 0.0).astype(BF16)

    m_fwd = jnp.concatenate([ind(same & (ri <= ci)), ind(same & (ri > ci))], axis=1)
    m_bwd = jnp.concatenate([ind(same & (ri >= ci)), ind(same & (ri < ci))], axis=1)
    eye = ind(ri == ci)
    sel_r = lax.broadcasted_iota(jnp.int32, (cpt, LANES), 0)
    sel_c = lax.broadcasted_iota(jnp.int32, (cpt, LANES), 1)
    sel = ind((sel_c >> 4) == sel_r)
    bd_r = lax.broadcasted_iota(jnp.int32, (cpt * pw, LANES), 0)
    bd_c = lax.broadcasted_iota(jnp.int32, (cpt * pw, LANES), 1)
    bd = (bd_r >> 7) == (bd_c >> 4)
    top = ((bd_r >> 6) & 1) == 0
    bd_all = ind(bd)
    bd_top = ind(bd & top)
    bd_bot = ind(bd & jnp.logical_not(top))
    u_lane = lax.broadcasted_iota(jnp.int32, (1, cpt * pw), 1)
    u_top = ((u_lane >> 6) & 1) == 0
    gain2 = jnp.concatenate([gain_r[...], gain_r[...]], axis=0)

    def colsum(a):
        return jnp.sum(a, axis=0, keepdims=True)

    def dot(a, b):
        return jnp.dot(a, b, preferred_element_type=F32)

    def pair_step(pr, carry):
        rows = pl.ds(pl.multiple_of(pr * pw, pw), pw)

        def phase_a(ti, c):
            ls = pl.ds(pl.multiple_of(ti * LANES, LANES), LANES)
            q = q_r[rows, ls]
            v = v_r[rows, ls]
            vb = v.astype(BF16)
            skews = [None, None]
            for bwd in (False, True):
                d_idx = int(bwd)
                raw_r = fb_r if bwd else ff_r
                lbv = lb_r[d_idx, rows, :]
                f = lbv + (1.0 - lbv) * jax.nn.sigmoid(raw_r[rows, ls])
                pm = (GLA_CHUNK - 1 - pos) if bwd else pos
                g_prev = [colsum(q[:hd]), colsum(q[hd:])]
                qd = q
                a_rows = [[], []]
                for dl in range(GLA_CHUNK):
                    qd = qd * (pltpu.roll(f, (LANES - dl) if bwd else dl, 1) if dl > 0 else f)
                    g_cur = [colsum(qd[:hd]), colsum(qd[hd:])]
                    for k in range(2):
                        a_rows[k].append(jnp.where(pm >= dl, g_prev[k] - g_cur[k], 0.0))
                    g_prev = g_cur
                for k in range(2):
                    order = a_rows[k] if bwd else [a_rows[k][0]] + a_rows[k][:0:-1]
                    stack = jnp.concatenate(order * cpt, axis=0).astype(BF16)
                    tall = lax.dot_general(eye, stack, _NT, preferred_element_type=F32)
                    sk = pltpu.roll(tall, 0, 1, stride=1, stride_axis=0)
                    skews[k] = sk if skews[k] is None else skews[k] + sk
                hi, mid, lo = _split3(jnp.log(f))
                m_cs = m_bwd if bwd else m_fwd
                cs = dot(hi, m_cs) + dot(mid, m_cs) + dot(lo, m_cs)
                qt_scr[d_idx, :, ls] = q * jnp.exp(cs[:, :LANES])
                kt = (1.0 - f) * jnp.exp(cs[:, LANES:])
                dsum = (lax.dot_general(sel, hi, _NT, preferred_element_type=F32)
                        + lax.dot_general(sel, mid, _NT, preferred_element_type=F32)
                        + lax.dot_general(sel, lo, _NT, preferred_element_type=F32))
                dec_scr[d_idx, ti] = jnp.exp(dsum)
                kbd = jnp.concatenate([kt.astype(BF16)] * cpt, axis=0) * bd_all
                uo = lax.dot_general(vb, kbd, _NT, preferred_element_type=F32)
                u_scr[d_idx, ti] = jnp.where(u_top, uo[:hd], uo[hd:])
            amats = [(sk * same_f).astype(BF16) for sk in skews]
            oi_scr[:, ls] = jnp.concatenate(
                [lax.dot_general(vb[:hd], amats[0], _NT, preferred_element_type=F32),
                 lax.dot_general(vb[hd:], amats[1], _NT, preferred_element_type=F32)], axis=0)
            return c

        lax.fori_loop(0, nt, phase_a, 0, unroll=2)

        def phase_b(ti, st):
            sf, sb = st
            u = u_scr[0, ti]
            dec = dec_scr[0, ti]
            pieces = [None] * cpt
            for c in range(cpt):
                pieces[c] = sf
                sf = dec[c:c + 1, :] * sf + u[:, c * pw:(c + 1) * pw]
            sst_scr[0, ti] = jnp.concatenate(pieces, axis=1).astype(BF16)
            tb = nt - 1 - ti
            u = u_scr[1, tb]
            dec = dec_scr[1, tb]
            for c in range(cpt - 1, -1, -1):
                pieces[c] = sb
                sb = dec[c:c + 1, :] * sb + u[:, c * pw:(c + 1) * pw]
            sst_scr[1, tb] = jnp.concatenate(pieces, axis=1).astype(BF16)
            return sf, sb

        if zero_init:
            s0 = (jnp.zeros((hd, pw), F32), jnp.zeros((hd, pw), F32))
        else:
            s0 = tuple(jnp.concatenate([s0_r[k, 2 * pr], s0_r[k, 2 * pr + 1]], axis=1) for k in range(2))
        s_end = lax.fori_loop(0, nt, phase_b, s0)
        for k in range(2):
            send_o[k, 2 * pr] = s_end[k][:, :hd]
            send_o[k, 2 * pr + 1] = s_end[k][:, hd:]

        def phase_c(ti, c):
            ls = pl.ds(pl.multiple_of(ti * LANES, LANES), LANES)
            o = oi_scr[:, ls]
            for k in range(2):
                t8 = jnp.concatenate([qt_scr[k, :, ls].astype(BF16)] * cpt, axis=0)
                qb = jnp.concatenate([t8 * bd_top, t8 * bd_bot], axis=1)
                oo = dot(sst_scr[k, ti], qb)
                o = o + jnp.concatenate([oo[:, :LANES], oo[:, LANES:]], axis=0)
            o0, o1 = o[:hd], o[hd:]
            on = jnp.concatenate([o0 * lax.rsqrt(jnp.mean(o0 * o0, axis=0, keepdims=True) + EPS),
                                  o1 * lax.rsqrt(jnp.mean(o1 * o1, axis=0, keepdims=True) + EPS)], axis=0)
            on_scr[rows, ls] = on * gain2 * _silu(g_r[rows, ls])
            return c

        lax.fori_loop(0, nt, phase_c, 0, unroll=2)
        return carry

    lax.fori_loop(0, B_HEADS // 2, pair_step, 0)
    ob_o[...] = on_scr[...].T.astype(BF16)


def _hgrn(hgt, col0, n_batch, t, lb_b, gain_b, s0t):
    bw = B_HEADS * HEAD_DIM
    hd, pw = HEAD_DIM, 2 * HEAD_DIM
    nt = t // LANES
    cpt = LANES // GLA_CHUNK
    zero_init = s0t is None
    specs = [pl.BlockSpec((bw, t), functools.partial(lambda r, b: (r, col0 // t + b), r)) for r in range(5)]
    in_specs = specs + [_full_spec(lb_b.shape), _full_spec(gain_b.shape)]
    args = [hgt] * 5 + [lb_b, gain_b]
    st_spec = pl.BlockSpec((None, 2, B_HEADS, HEAD_DIM, HEAD_DIM), lambda b: (b, 0, 0, 0, 0))
    if not zero_init:
        in_specs.append(st_spec)
        args.append(s0t)
    scratch = [pltpu.VMEM((pw, t), F32), pltpu.VMEM((2, nt, hd, cpt * pw), F32), pltpu.VMEM((2, nt, cpt, pw), F32),
               pltpu.VMEM((2, pw, t), F32), pltpu.VMEM((2, nt, hd, cpt * pw), BF16), pltpu.VMEM((bw, t), F32)]
    return pl.pallas_call(
        functools.partial(_hgrn_body, zero_init, t),
        grid=(n_batch,),
        in_specs=in_specs,
        out_specs=[pl.BlockSpec((t, bw), lambda b: (b, 0)), st_spec],
        out_shape=[jax.ShapeDtypeStruct((n_batch * t, bw), BF16),
                   jax.ShapeDtypeStruct((n_batch, 2, B_HEADS, HEAD_DIM, HEAD_DIM), F32)],
        scratch_shapes=scratch,
        compiler_params=_params(1),
        name="hgrn2_context" if zero_init else "hgrn2_latent",
    )(*args)


def _route(logits):
    lane_i = lax.broadcasted_iota(jnp.int32, logits.shape, 1)
    lane = lane_i.astype(F32)
    neg = -jnp.inf
    big = 1e9
    gl = jnp.where(lane_i < MOE_GROUPS, logits, neg)
    gmax = jnp.max(gl, axis=-1, keepdims=True)
    g_val = 1.0 / jnp.sum(jnp.exp(gl - gmax), axis=-1, keepdims=True)
    g_idx = jnp.min(jnp.where(gl == gmax, lane, big), axis=-1, keepdims=True)
    n_e = MOE_GROUPS * EXPERTS_PER_GROUP
    lane_group = ((lane_i - MOE_GROUPS) >> 2).astype(F32)
    in_group = (lane_i >= MOE_GROUPS) & (lane_i < MOE_GROUPS + n_e) & (lane_group == g_idx)
    el = jnp.where(in_group, logits, neg)
    m1 = jnp.max(el, axis=-1, keepdims=True)
    i1 = jnp.min(jnp.where(el == m1, lane, big), axis=-1, keepdims=True)
    el2 = jnp.where(lane == i1, neg, el)
    m2 = jnp.max(el2, axis=-1, keepdims=True)
    i2 = jnp.min(jnp.where(el2 == m2, lane, big), axis=-1, keepdims=True)
    tt = jnp.exp(m2 - m1)
    w1 = g_val / (1.0 + tt)
    w2 = g_val * tt / (1.0 + tt)
    base = MOE_GROUPS + EXPERTS_PER_GROUP * g_idx
    route = jnp.where(lane == i1 - base, w1, 0.0) + jnp.where(lane == i2 - base, w2, 0.0)
    return jnp.where(lane_i == ROUTE_GID_LANE, g_idx, route)


def _proj_res_body(geom, lhs_n, n_x, has_y, *refs):
    lhs = []
    for n_a in lhs_n:
        lhs.append(geom.tok_load(refs[:n_a]))
        refs = refs[n_a:]
    ws, refs = refs[:len(lhs_n)], refs[len(lhs_n):]
    xs, refs = refs[:n_x], refs[n_x:]
    x = geom.tok_load(xs)
    if has_y:
        y, g2, refs = refs[0], refs[1], refs[2:]
        x = x + g2[...] * _tile_rows_load(y, TOKEN_TILE)
    g1, gn2, sh2, sc2, wrh, wrl, br, xnew_o, xr_o, route_o = refs[:10]
    wbs = refs[10:]

    @pl.when(pl.program_id(0) == 0)
    def _():
        for w, wb in zip(ws, wbs):
            wb[...] = w[...].astype(BF16)

    acc = jnp.dot(lhs[0], wbs[0][...], preferred_element_type=F32)
    for a, wb in zip(lhs[1:], wbs[1:]):
        acc = acc + jnp.dot(a, wb[...], preferred_element_type=F32)
    xn = x + g1[...] * acc
    xnew_o[...] = xn
    hm = _modulate(xn, gn2[...], sh2[...], sc2[...])
    hi = hm.astype(BF16)
    lo = (hm - hi.astype(F32)).astype(BF16)
    logits = (jnp.dot(hi, wrh[...], preferred_element_type=F32) + jnp.dot(lo, wrh[...], preferred_element_type=F32)
              + jnp.dot(hi, wrl[...], preferred_element_type=F32)) + br[...]
    route = _route(logits)
    for s, piece in enumerate(_pack_routed(hm, route)):
        xr_o[pl.ds(s, TOKEN_TILE, stride=ROUTED_SUB), :] = piece
    route_o[...] = route


def _proj_res(geom, mod, layer, lhs, ws, xa, y, gn2, wrh, wrl, br, name):
    tm = TOKEN_TILE
    d = geom.d
    has_y = y is not None
    tok = pl.BlockSpec((tm, d), lambda i: (i, 0))
    in_specs, args = [], []
    for a in lhs:
        s, ar = geom.tok_specs(a)
        in_specs += s
        args += ar
    w_shapes = []
    for w_all, lead, row_block in ws:
        in_specs.append(_resident_slab(w_all.shape, lead, row_block))
        args.append(w_all)
        w_shapes.append((w_all.shape[1] if row_block is None else row_block[1],) + tuple(w_all.shape[2:]))
    s, ar = geom.tok_specs(xa)
    in_specs += s
    args += ar
    if has_y:
        in_specs += [_Y_SPEC, geom.mod_spec(layer - 1, 5)]
        args += [y, mod]
    in_specs += [geom.mod_spec(layer, 2), _layer_spec(gn2.shape, layer), geom.mod_spec(layer, 3),
                 geom.mod_spec(layer, 4), _layer_spec(wrh.shape, layer), _layer_spec(wrl.shape, layer),
                 _layer_spec(br.shape, layer)]
    args += [mod, gn2, mod, mod, wrh, wrl, br]
    return pl.pallas_call(
        functools.partial(_proj_res_body, geom, tuple(_n_tok(a) for a in lhs), _n_tok(xa), has_y),
        grid=(geom.n // tm,),
        in_specs=in_specs,
        out_specs=[tok, pl.BlockSpec((tm * ROUTED_SUB, LANES), lambda i: (i, 0)),
                   pl.BlockSpec((tm, ROUTE_W), lambda i: (i, 0))],
        out_shape=[jax.ShapeDtypeStruct((geom.n, d), F32), jax.ShapeDtypeStruct((geom.n * ROUTED_SUB, LANES), F32),
                   jax.ShapeDtypeStruct((geom.n, ROUTE_W), F32)],
        scratch_shapes=[pltpu.VMEM(shp, BF16) for shp in w_shapes],
        compiler_params=_params(1),
        name=name,
    )(*args)


def _dispatch(route, n, tile):
    n_tiles = n // tile + MOE_GROUPS
    n_slots = n_tiles * tile
    gid = route[:, ROUTE_GID_LANE].astype(jnp.int32)
    onehot = (gid[:, None] == jnp.arange(MOE_GROUPS, dtype=jnp.int32)[None, :]).astype(jnp.int32)
    csum = jnp.cumsum(onehot, axis=0)
    counts = csum[-1]
    rank = jnp.sum(csum * onehot, axis=1) - 1
    tiles_g = (counts + tile - 1) // tile
    tile_end = jnp.cumsum(tiles_g)
    tile_start = tile_end - tiles_g
    dest = (tile_start * tile)[gid] + rank
    tok = jnp.arange(n, dtype=jnp.int32)
    src = jnp.zeros((n_slots,), jnp.int32).at[dest].set(tok)
    t_idx = jnp.arange(n_tiles, dtype=jnp.int32)
    tile_gid = jnp.minimum(jnp.sum((t_idx[:, None] >= tile_end[None, :]).astype(jnp.int32), axis=1), MOE_GROUPS - 1)
    tile_rows = jnp.clip(counts[tile_gid] - (t_idx - tile_start[tile_gid]) * tile, 0, tile)
    tile_rows = jnp.where(t_idx < tile_end[-1], tile_rows, 0)
    in_tile = jnp.arange(tile, dtype=jnp.int32)
    valid = (in_tile[None, :] < tile_rows[:, None]).reshape(n_slots)
    spare = (n + (t_idx[:, None] & 1) * tile + in_tile[None, :]).reshape(n_slots)
    dst = jnp.concatenate([n + tile + in_tile, jnp.where(valid, src, spare)])
    return tile_gid, src, dst, tile_end[-1:].astype(jnp.int32), n_tiles


def _moe_body(tile, d, n_rows, tg_ref, src_ref, dst_ref, nu_ref, xr_hbm, w1_ref, w3_ref, w2_ref, y_hbm,
              xbuf, ybuf, w1b, w3b, w2b, isem, osem):
    t = pl.program_id(0)
    last = nu_ref[0] - 1
    slot = t % 2
    other = 1 - slot

    sub = SUBLANES
    rsub = ROUTED_SUB

    def hbm_rows(ref, row, n_sub):
        start = row * n_sub if isinstance(row, int) else pl.multiple_of(row * n_sub, n_sub)
        return ref.at[pl.ds(start, n_sub), :]

    def row_in(tile_idx, s, i):
        row = src_ref[tile_idx * tile + i]
        return pltpu.make_async_copy(hbm_rows(xr_hbm, row, rsub), xbuf.at[s, pl.ds(i * rsub, rsub), :], isem.at[s])

    def row_out(ext_tile_idx, s, i):
        row = dst_ref[ext_tile_idx * tile + i]
        return pltpu.make_async_copy(ybuf.at[s, pl.ds(i * sub, sub), :], hbm_rows(y_hbm, row, sub), osem.at[s])

    @pl.when(t == 0)
    def _():
        ybuf[...] = jnp.zeros(ybuf.shape, F32)
        for i in range(tile):
            row_in(0, 0, i).start(priority=i % 2)
        for i in range(tile):
            pltpu.make_async_copy(ybuf.at[0, pl.ds(i * sub, sub), :], hbm_rows(y_hbm, n_rows + i, sub),
                                  osem.at[0]).start(priority=i % 2)

    g = tg_ref[t]
    g_prev = tg_ref[jnp.maximum(t - 1, 0)]

    @pl.when(((t == 0) | (g != g_prev)) & (t <= last))
    def _():
        w1b[...] = w1_ref[...].astype(BF16)
        w3b[...] = w3_ref[...].astype(BF16)
        w2b[...] = w2_ref[...].astype(BF16)

    nxt = jnp.minimum(t + 1, last)
    n_parts = 4 * EXPERTS_PER_GROUP
    per = tile // n_parts

    def step(slot, other):
        for i in range(tile):
            row_in(nxt, other, i).start(priority=i % 2)
        for i in range(tile):
            row_in(0, slot, i).wait()
        part = [0]

        def issue_part():
            k = part[0]
            part[0] += 1
            for i in range(k * per, (k + 1) * per):
                row_out(t, other, i).start(priority=i % 2)

        xb, cw = _unpack_routed(xbuf.at[slot], tile)
        acc = jnp.zeros((tile, d), F32)
        for e in range(EXPERTS_PER_GROUP):
            a = jnp.dot(xb, w1b[e], preferred_element_type=F32)
            issue_part()
            b = jnp.dot(xb, w3b[e], preferred_element_type=F32)
            issue_part()
            hid = (_silu(a) * b * cw[:, e:e + 1]).astype(BF16)
            issue_part()
            acc = acc + jnp.dot(hid, w2b[e], preferred_element_type=F32)
            issue_part()
        for i in range(tile):
            row_out(0, slot, i).wait()
        _tile_rows_store(ybuf.at[slot], acc)

        @pl.when(t == last)
        def _():
            for i in range(tile):
                row_in(0, other, i).wait()
            for i in range(tile):
                row_out(0, other, i).wait()
            for i in range(tile):
                row_out(t + 1, slot, i).start(priority=i % 2)
            for i in range(tile):
                row_out(0, slot, i).wait()

    @pl.when((slot == 0) & (t <= last))
    def _():
        step(0, 1)

    @pl.when((slot == 1) & (t <= last))
    def _():
        step(1, 0)


def _moe(xr, route, w1, w3, w2, layer, n, d):
    tile = MOE_TILE
    assert d == SUBLANES * LANES
    tile_gid, src, dst, n_used, n_tiles = _dispatch(route, n, tile)
    g0 = layer * MOE_GROUPS
    wspec13 = pl.BlockSpec((None,) + w1.shape[1:], lambda t, tg, s, ds_, nu: (g0 + tg[t], 0, 0, 0))
    wspec2 = pl.BlockSpec((None,) + w2.shape[1:], lambda t, tg, s, ds_, nu: (g0 + tg[t], 0, 0, 0))
    grid_spec = pltpu.PrefetchScalarGridSpec(
        num_scalar_prefetch=4,
        grid=(n_tiles,),
        in_specs=[pl.BlockSpec(memory_space=pl.ANY), wspec13, wspec13, wspec2],
        out_specs=pl.BlockSpec(memory_space=pl.ANY),
        scratch_shapes=[pltpu.VMEM((2, tile * ROUTED_SUB, LANES), F32), pltpu.VMEM((2, tile * SUBLANES, LANES), F32),
                        pltpu.VMEM(w1.shape[1:], BF16), pltpu.VMEM(w3.shape[1:], BF16), pltpu.VMEM(w2.shape[1:], BF16),
                        pltpu.SemaphoreType.DMA((2,)), pltpu.SemaphoreType.DMA((2,))],
    )
    return pl.pallas_call(
        functools.partial(_moe_body, tile, d, n),
        grid_spec=grid_spec,
        out_shape=jax.ShapeDtypeStruct(((n + 2 * tile) * SUBLANES, LANES), F32),
        compiler_params=_params(1),
        name="moe_ffn",
    )(tile_gid, src, dst, n_used, xr, w1, w3, w2)


def _glu_body(xa, y, g2, gn, sh, sc, w, u_o, wb):
    @pl.when(pl.program_id(0) == 0)
    def _():
        wb[...] = w[...].astype(BF16)

    x = xa[...] + g2[...] * _tile_rows_load(y, TOKEN_TILE)
    h = _modulate(x, gn[...], sh[...], sc[...]).astype(BF16)
    ab = jnp.dot(h, wb[...], preferred_element_type=F32)
    c = ab.shape[1] // 2
    u_o[...] = ab[:, :c] * jax.nn.sigmoid(ab[:, c:])


def _glu(geom, mod, layer, xa, y, gn, w_all, o):
    tm = TOKEN_TILE
    d = geom.d
    tok = pl.BlockSpec((tm, d), lambda i: (i, 0))
    c = w_all.shape[2] // 2
    return pl.pallas_call(
        _glu_body,
        grid=(geom.n // tm,),
        in_specs=[tok, _Y_SPEC, geom.mod_spec(layer - 1, 5), _row_spec(d), geom.mod_spec(layer, 0),
                  geom.mod_spec(layer, 1), _resident_slab(w_all.shape, o)],
        out_specs=pl.BlockSpec((tm, c), lambda i: (i, 0)),
        out_shape=jax.ShapeDtypeStruct((geom.n, c), F32),
        scratch_shapes=[pltpu.VMEM(w_all.shape[1:], BF16)],
        compiler_params=_params(1),
        name="conv_glu",
    )(xa, y, mod, gn, mod, mod, w_all)


def _dwconv_body(t, u_r, dw_r, b_r, g_r, be_r, o_r, pad_scr, cv_scr, sh_scr):
    c = u_r.shape[1]
    halo = 16
    sh_len = sh_scr.shape[1]
    pad_scr[0:halo, :] = jnp.zeros((halo, c), F32)
    pad_scr[t + halo:t + 2 * halo, :] = jnp.zeros((halo, c), F32)
    pad_scr[halo:t + halo, :] = u_r[...]
    rb_n = CONV_ROWS
    n_lb = c // LANES
    first = halo - CONV_WIDTH // 2

    def blk(lb, carry):
        ls = pl.ds(pl.multiple_of(lb * LANES, LANES), LANES)
        bias = b_r[:, ls]
        for r0 in range(0, t, rb_n):
            win = pad_scr[r0:r0 + rb_n + 2 * halo, ls]
            for sft in range(1, SUBLANES):
                sh_scr[sft] = pltpu.roll(win, win.shape[0] - sft, 0)[:sh_len, :]
            acc = jnp.zeros((rb_n, LANES), F32)
            for k in range(CONV_WIDTH):
                sft = (first + k) % SUBLANES
                a = first + k - sft
                src = pad_scr[r0 + a:r0 + a + rb_n, ls] if sft == 0 else sh_scr[sft, a:a + rb_n, :]
                acc = acc + src * dw_r[k:k + 1, ls]
            cv_scr[r0:r0 + rb_n, ls] = acc + bias
        return carry

    lax.fori_loop(0, n_lb, blk, 0)

    def ln(rb, carry):
        rs = pl.ds(pl.multiple_of(rb * rb_n, rb_n), rb_n)
        x = cv_scr[rs, :]
        mu = jnp.mean(x, axis=-1, keepdims=True)
        xc = x - mu
        var = jnp.mean(xc * xc, axis=-1, keepdims=True)
        yv = xc * lax.rsqrt(var + EPS) * g_r[...] + be_r[...]
        o_r[rs, :] = _silu(yv).astype(BF16)
        return carry

    lax.fori_loop(0, t // rb_n, ln, 0, unroll=2)


def _dwconv(u, row0, n_batch, t, dw, b, g, be):
    c = u.shape[1]
    return pl.pallas_call(
        functools.partial(_dwconv_body, t),
        grid=(n_batch,),
        in_specs=[pl.BlockSpec((t, c), lambda bi: (row0 // t + bi, 0)), _full_spec(dw.shape),
                  _row_spec(c), _row_spec(c), _row_spec(c)],
        out_specs=pl.BlockSpec((t, c), lambda bi: (bi, 0)),
        out_shape=jax.ShapeDtypeStruct((n_batch * t, c), BF16),
        scratch_shapes=[pltpu.VMEM((t + 32, c), F32), pltpu.VMEM((t, c), F32),
                        pltpu.VMEM((SUBLANES, CONV_ROWS + (CONV_WIDTH // SUBLANES) * SUBLANES, LANES), F32)],
        compiler_params=_params(1),
        name="conv_dw",
    )(u, dw, b, g, be)


def _final_body(ctx_tiles, xa, y, g2, gf, oc, os_):
    x = xa[...] + g2[...] * _tile_rows_load(y, TOKEN_TILE)
    ms = jnp.mean(x * x, axis=-1, keepdims=True)
    val = x * lax.rsqrt(ms + EPS) * gf[...]
    i = pl.program_id(0)

    @pl.when(i < ctx_tiles)
    def _():
        oc[...] = val

    @pl.when(i >= ctx_tiles)
    def _():
        os_[...] = val


def _final(geom, mod, depth, xa, y, gf):
    tm = TOKEN_TILE
    d = geom.d
    ct = geom.ctx_tiles
    tok = pl.BlockSpec((tm, d), lambda i: (i, 0))
    return pl.pallas_call(
        functools.partial(_final_body, ct),
        grid=(geom.n // tm,),
        in_specs=[tok, _Y_SPEC, geom.mod_spec(depth - 1, 5), _row_spec(d)],
        out_specs=[pl.BlockSpec((tm, d), lambda i: (jnp.minimum(i, ct - 1), 0)),
                   pl.BlockSpec((tm, d), lambda i: (jnp.maximum(i - ct, 0), 0))],
        out_shape=[jax.ShapeDtypeStruct((geom.n_ctx, d), F32), jax.ShapeDtypeStruct((geom.n_smp, d), F32)],
        compiler_params=_params(1),
        name="final_norm",
    )(xa, y, mod, gf)


def _rope_tables(n_tokens):
    rows = n_tokens // GRID_W
    row = jnp.repeat(jnp.arange(rows, dtype=F32), GRID_W)
    col = jnp.tile(jnp.arange(GRID_W, dtype=F32), rows)
    half = HEAD_DIM // 2
    inv = ROPE_THETA ** (-jnp.arange(0, half, 2, dtype=F32) / half)
    ang_r = row[:, None] * inv[None, :]
    ang_c = col[:, None] * inv[None, :]
    cr, sr, cc, sc = jnp.cos(ang_r), jnp.sin(ang_r), jnp.cos(ang_c), jnp.sin(ang_c)
    cos = jnp.concatenate([cr, cr, cc, cc], axis=1)
    sin = jnp.concatenate([-sr, sr, -sc, sc], axis=1)
    return (jnp.tile(cos, (1, A_HEADS)), jnp.tile(sin, (1, A_HEADS)),
            jnp.tile(cos, (1, A_KV_HEADS)), jnp.tile(sin, (1, A_KV_HEADS)))


def _router_weights(w_group, b_group, w_expert, b_expert):
    depth, d, _ = w_group.shape
    n_used = MOE_GROUPS + MOE_GROUPS * EXPERTS_PER_GROUP
    w = jnp.concatenate([w_group, w_expert, jnp.zeros((depth, d, ROUTE_W - n_used), F32)], axis=2)
    b = jnp.concatenate([b_group, b_expert, jnp.zeros((depth, ROUTE_W - n_used), F32)], axis=1)[:, None, :]
    hi = w.astype(BF16)
    lo = (w - hi.astype(F32)).astype(BF16)
    return hi, lo, b


def kernel(x_prompt, x_sample, cache_k, cache_v, state_hgrn, c, c_ctx, ada_w, ada_b, norm1_g, norm2_g, mix_w_in, mix_w_out, attn_q_gain, attn_k_gain, hgrn_lower_bound, hgrn_out_gain, conv_pw1, conv_dw, conv_dw_b, conv_ln_g, conv_ln_b, conv_pw2, moe_w_group, moe_b_group, moe_w_expert, moe_b_expert, moe_w1, moe_w3, moe_w2, final_g):
    batch, seq, d = x_prompt.shape
    dec_batch, dec_seq, _ = x_sample.shape
    depth = ada_w.shape[0]
    past = cache_k.shape[2]
    n_ctx, n_smp = batch * seq, dec_batch * dec_seq
    n = n_ctx + n_smp
    geom = _Geom(n_ctx, n_smp, dec_seq, d)
    qw, kw, bw = A_HEADS * HEAD_DIM, A_KV_HEADS * HEAD_DIM, B_HEADS * HEAD_DIM
    n_nat = qw + 2 * kw

    cond8 = jnp.concatenate([c_ctx[None, :], c, jnp.zeros((8 - 1 - dec_batch, d), F32)], axis=0)
    mod = _ada(cond8, ada_w, ada_b).reshape(depth, 8, 6, 1, d)

    lb_soft = jax.nn.softmax(hgrn_lower_bound.astype(F32), axis=0)
    lower_bounds = jnp.cumsum(lb_soft, axis=0) - lb_soft[0]
    rope_tabs = _rope_tables(dec_seq)

    xa = (x_prompt.reshape(n_ctx, d), x_sample.reshape(n_smp, d))
    y = None
    f = moe_w1.shape[-1]
    w1 = moe_w1.reshape(depth * MOE_GROUPS, EXPERTS_PER_GROUP, d, f)
    w3 = moe_w3.reshape(depth * MOE_GROUPS, EXPERTS_PER_GROUP, d, f)
    w2 = moe_w2.reshape(depth * MOE_GROUPS, EXPERTS_PER_GROUP, f, d)
    n_ab = cache_k.shape[1]
    ck = cache_k.reshape(dec_batch, n_ab, past, kw)
    cv = cache_v.reshape(dec_batch, n_ab, past, kw)
    wrh, wrl, br = _router_weights(moe_w_group, moe_b_group, moe_w_expert, moe_b_expert)
    gn2 = norm2_g[:, None, :]
    new_k, new_v, new_s = [], [], []
    for l in range(depth):
        gn1 = norm1_g[l][None, :]
        if l % 2 == 0:
            e = l // 2
            qkv, hgt = _inproj(geom, mod, l, xa, y, gn1, mix_w_in, e, n_nat)
            qg = jnp.tile(attn_q_gain[e], A_HEADS)[None, :]
            kg = jnp.tile(attn_k_gain[e], A_KV_HEADS)[None, :]
            oa_c, khat = _attention(qkv, 0, batch, seq, qg, kg)
            oa_s = _attention(qkv, n_ctx, dec_batch, dec_seq, qg, kg, cache=(ck, cv, e), rope_tabs=rope_tabs)
            lb_b = jnp.broadcast_to(lower_bounds[e][:, :, None], (2, bw, LANES))
            gain_b = jnp.broadcast_to(hgrn_out_gain[e][:, None], (HEAD_DIM, LANES))
            ob_c, s_ctx = _hgrn(hgt, 0, batch, seq, lb_b, gain_b, None)
            s0t = jnp.swapaxes(state_hgrn[:, e], -1, -2)
            ob_s, _ = _hgrn(hgt, n_ctx, dec_batch, dec_seq, lb_b, gain_b, s0t)
            w_halves = [(mix_w_out, e, (0, qw)), (mix_w_out, e, (1, bw))]
            xnew, xr, route = _proj_res(geom, mod, l, [(oa_c, oa_s), (ob_c, ob_s)], w_halves, xa, y,
                                        gn2, wrh, wrl, br, "ab_outproj")
            new_k.append(khat.reshape(batch, seq, A_KV_HEADS, HEAD_DIM))
            new_v.append(qkv[:n_ctx, qw + kw:].reshape(batch, seq, A_KV_HEADS, HEAD_DIM))
            new_s.append(jnp.swapaxes(s_ctx, -1, -2))
        else:
            o = l // 2
            u = _glu(geom, mod, l, xa, y, gn1, conv_pw1, o)
            cargs = (conv_dw[o], conv_dw_b[o][None, :], conv_ln_g[o][None, :], conv_ln_b[o][None, :])
            c_c = _dwconv(u, 0, batch, seq, *cargs)
            c_s = _dwconv(u, n_ctx, dec_batch, dec_seq, *cargs)
            xnew, xr, route = _proj_res(geom, mod, l, [(c_c, c_s)], [(conv_pw2, o, None)], xa, y, gn2, wrh, wrl, br,
                                        "conv_outproj")
        y = _moe(xr, route, w1, w3, w2, l, n, d)
        xa = xnew
    y_prompt, y_sample = _final(geom, mod, depth, xa, y, final_g[None, :])
    return (y_prompt.reshape(batch, seq, d), y_sample.reshape(dec_batch, dec_seq, d),
            jnp.stack(new_k, axis=1), jnp.stack(new_v, axis=1), jnp.stack(new_s, axis=1))
```

```python
import functools

import jax
import jax.numpy as jnp
from jax import lax
from jax.experimental import pallas as pl
from jax.experimental.pallas import tpu as pltpu

F32 = jnp.float32
BF16 = jnp.bfloat16

HEAD_DIM = 64
A_HEADS = 8
A_KV_HEADS = 2
A_GROUP = A_HEADS // A_KV_HEADS
B_HEADS = 8
GRID_W = 64
ROPE_THETA = 10000.0
GLA_CHUNK = 16
CONV_WIDTH = 31
MOE_GROUPS = 4
EXPERTS_PER_GROUP = 4
EPS = 1e-6

LANES = 128
SUBLANES = 8
VMEM_LIMIT = 56 * 1024 * 1024
TOKEN_TILE = 512
MOE_TILE = 256
ATTN_Q_TILE = 256
ROUTE_W = LANES
ROUTE_GID_LANE = 8
CONV_ROWS = 64

_NT = (((1,), (1,)), ((), ()))


def _params(n_axes):
    return pltpu.CompilerParams(dimension_semantics=("arbitrary",) * n_axes, vmem_limit_bytes=VMEM_LIMIT)


def _modulate(x, g, sh, sc):
    ms = jnp.mean(x * x, axis=-1, keepdims=True)
    return (x * lax.rsqrt(ms + EPS) * g) * (1.0 + sc) + sh


def _silu(x):
    return x * jax.nn.sigmoid(x)


def _tile_rows_load(ref, n_rows):
    return jnp.concatenate([ref[pl.ds(s, n_rows, stride=SUBLANES), :] for s in range(SUBLANES)], axis=1)


def _tile_rows_store(ref, val):
    n_rows = val.shape[0]
    for s in range(SUBLANES):
        ref[pl.ds(s, n_rows, stride=SUBLANES), :] = val[:, s * LANES:(s + 1) * LANES]


ROUTED_SUB = 2 * SUBLANES


def _pack_routed(hm, route):
    pieces = [hm[:, s * LANES:(s + 1) * LANES] for s in range(SUBLANES)] + [route]
    return pieces + [jnp.zeros(route.shape, F32)] * (ROUTED_SUB - len(pieces))


def _unpack_routed(ref, n_rows):
    chunks = [ref[pl.ds(s, n_rows, stride=ROUTED_SUB), :] for s in range(SUBLANES)]
    route = ref[pl.ds(SUBLANES, n_rows, stride=ROUTED_SUB), :]
    return jnp.concatenate(chunks, axis=1).astype(BF16), route


def _split3(x):
    hi = x.astype(BF16)
    r1 = x - hi.astype(F32)
    mid = r1.astype(BF16)
    lo = (r1 - mid.astype(F32)).astype(BF16)
    return hi, mid, lo


def _ada_body(c_ref, w_ref, b_ref, o_ref):
    s = _silu(c_ref[...])
    o_ref[...] = jnp.dot(s.astype(BF16), w_ref[...].astype(BF16), preferred_element_type=F32) + b_ref[...]


def _ada(cond8, ada_w, ada_b):
    depth, d, d6 = ada_w.shape
    tn = d6 // 4
    return pl.pallas_call(
        _ada_body,
        grid=(depth, d6 // tn),
        in_specs=[
            pl.BlockSpec((8, d), lambda l, j: (0, 0)),
            pl.BlockSpec((None, d, tn), lambda l, j: (l, 0, j)),
            pl.BlockSpec((None, 1, tn), lambda l, j: (l, 0, j)),
        ],
        out_specs=pl.BlockSpec((None, 8, tn), lambda l, j: (l, 0, j)),
        out_shape=jax.ShapeDtypeStruct((depth, 8, d6), F32),
        compiler_params=_params(2),
        name="adaln",
    )(cond8, ada_w, ada_b.reshape(depth, 1, d6))


class _Geom:
    def __init__(self, n_ctx, n_smp, smp_len, d):
        self.n_ctx, self.n_smp, self.smp_len, self.d = n_ctx, n_smp, smp_len, d
        self.n = n_ctx + n_smp
        self.ctx_tiles = n_ctx // TOKEN_TILE

    def mod_spec(self, layer, j):
        ctx_tiles = self.ctx_tiles
        per_smp = self.smp_len // TOKEN_TILE

        def idx(i):
            row = jnp.where(i < ctx_tiles, 0, 1 + (i - ctx_tiles) // per_smp)
            return (layer, row, j, 0, 0)

        return pl.BlockSpec((None, None, None, 1, self.d), idx)

    def tok_specs(self, a):
        tm = TOKEN_TILE
        if isinstance(a, tuple):
            ct = self.ctx_tiles
            w = a[0].shape[1]
            return ([pl.BlockSpec((tm, w), lambda i: (jnp.minimum(i, ct - 1), 0)),
                     pl.BlockSpec((tm, w), lambda i: (jnp.maximum(i - ct, 0), 0))], list(a))
        return [pl.BlockSpec((tm, a.shape[1]), lambda i: (i, 0))], [a]

    def tok_load(self, refs):
        if len(refs) == 1:
            return refs[0][...]
        return jnp.where(pl.program_id(0) < self.ctx_tiles, refs[0][...], refs[1][...])


def _n_tok(a):
    return 2 if isinstance(a, tuple) else 1


def _row_spec(d):
    return pl.BlockSpec((1, d), lambda i: (0, 0))


def _layer_spec(shape, lead):
    return pl.BlockSpec((None,) + tuple(shape[1:]), lambda *_: (lead,) + (0,) * (len(shape) - 1))


def _resident_slab(shape, lead, row_block=None):
    rows = shape[1] if row_block is None else row_block[1]
    rb = 0 if row_block is None else row_block[0]
    return pl.BlockSpec((None, rows) + tuple(shape[2:]), lambda *_: (lead, rb) + (0,) * (len(shape) - 2),
                        pipeline_mode=pl.Buffered(1))


def _full_spec(shape):
    nd = len(shape)
    return pl.BlockSpec(shape, lambda *_: (0,) * nd)


_Y_SPEC = pl.BlockSpec((TOKEN_TILE * SUBLANES, LANES), lambda i: (i, 0))


def _inproj_body(geom, n_x, has_y, n_nat, *refs):
    xs, refs = refs[:n_x], refs[n_x:]
    x = geom.tok_load(xs)
    if has_y:
        y, g2, refs = refs[0], refs[1], refs[2:]
        x = x + g2[...] * _tile_rows_load(y, TOKEN_TILE)
    gn, sh, sc, w, qkv_o, hgt_o, wn_s, wt_s = refs
    step = B_HEADS * HEAD_DIM
    n_t = wt_s.shape[0]

    @pl.when(pl.program_id(0) == 0)
    def _():
        wn_s[...] = w[:, :n_nat].astype(BF16)
        for r in range(0, n_t, step):
            wt_s[r:r + step, :] = w[:, n_nat + r:n_nat + r + step].T.astype(BF16)

    h = _modulate(x, gn[...], sh[...], sc[...]).astype(BF16)
    qkv_o[...] = jnp.dot(h, wn_s[...], preferred_element_type=F32)
    for r in range(0, n_t, step):
        hgt_o[r:r + step, :] = lax.dot_general(wt_s[r:r + step, :], h, _NT, preferred_element_type=F32)


def _inproj(geom, mod, layer, xa, y, gn, w_in_all, e, n_nat):
    tm = TOKEN_TILE
    d = geom.d
    n_t = w_in_all.shape[2] - n_nat
    has_y = y is not None
    in_specs, args = geom.tok_specs(xa)
    if has_y:
        in_specs += [_Y_SPEC, geom.mod_spec(layer - 1, 5)]
        args += [y, mod]
    in_specs += [_row_spec(d), geom.mod_spec(layer, 0), geom.mod_spec(layer, 1), _resident_slab(w_in_all.shape, e)]
    args += [gn, mod, mod, w_in_all]
    return pl.pallas_call(
        functools.partial(_inproj_body, geom, _n_tok(xa), has_y, n_nat),
        grid=(geom.n // tm,),
        in_specs=in_specs,
        out_specs=[pl.BlockSpec((tm, n_nat), lambda i: (i, 0)), pl.BlockSpec((n_t, tm), lambda i: (0, i))],
        out_shape=[jax.ShapeDtypeStruct((geom.n, n_nat), F32), jax.ShapeDtypeStruct((n_t, geom.n), F32)],
        scratch_shapes=[pltpu.VMEM((d, n_nat), BF16), pltpu.VMEM((n_t, d), BF16)],
        compiler_params=_params(1),
        name="ab_inproj",
    )(*args)


def _head_norm(x, gain_row, n_heads):
    outs = []
    for h in range(n_heads):
        xh = x[:, h * HEAD_DIM:(h + 1) * HEAD_DIM]
        ms = jnp.mean(xh * xh, axis=-1, keepdims=True)
        outs.append(xh * lax.rsqrt(ms + EPS))
    return jnp.concatenate(outs, axis=1) * gain_row


def _rope(x, cos, sin_signed):
    w = x.shape[1]
    lane = lax.broadcasted_iota(jnp.int32, (1, w), 1)
    first_half = (lane & 16) == 0
    partner = jnp.where(first_half, pltpu.roll(x, w - 16, 1), pltpu.roll(x, 16, 1))
    return x * cos + partner * sin_signed


def _attn_body(rope, tq, *refs):
    if rope:
        q_r, k_r, v_r, ck_r, cv_r, qg_r, kg_r, cq_r, sq_r, ckk_r, skk_r, o_r, keys_s, vals_s = refs
    else:
        q_r, k_r, v_r, qg_r, kg_r, o_r, kh_r, keys_s, vals_s = refs

    @pl.when(pl.program_id(1) == 0)
    def _():
        kn = _head_norm(k_r[...], kg_r[...], A_KV_HEADS)
        if rope:
            kn = _rope(kn, ckk_r[...], skk_r[...])
            keys = jnp.concatenate([ck_r[...], kn], axis=0)
            vals = jnp.concatenate([cv_r[...], v_r[...]], axis=0)
        else:
            kh_r[...] = kn
            keys, vals = kn, v_r[...]
        keys_s[...] = keys.astype(BF16)
        ones = jnp.ones((keys.shape[0], HEAD_DIM), BF16)
        for j in range(A_KV_HEADS):
            vals_s[j] = jnp.concatenate([vals[:, j * HEAD_DIM:(j + 1) * HEAD_DIM].astype(BF16), ones], axis=1)

    qn = _head_norm(q_r[...], qg_r[...], A_HEADS)
    if rope:
        qn = _rope(qn, cq_r[...], sq_r[...])
    qn = qn * (HEAD_DIM ** -0.5)
    keys = keys_s[...]
    outs = [None] * A_HEADS
    for j in range(A_KV_HEADS):
        kj = keys[:, j * HEAD_DIM:(j + 1) * HEAD_DIM]
        vj = vals_s[j]
        heads = [A_GROUP * j + g for g in range(A_GROUP)]
        q4 = jnp.concatenate([qn[:, h * HEAD_DIM:(h + 1) * HEAD_DIM] for h in heads], axis=0).astype(BF16)
        s = lax.dot_general(q4, kj, _NT, preferred_element_type=F32)
        m = jnp.max(s, axis=-1, keepdims=True)
        p = jnp.exp(s - m).astype(BF16)
        ol = jnp.dot(p, vj, preferred_element_type=F32)
        o = ol[:, :HEAD_DIM] / ol[:, HEAD_DIM:HEAD_DIM + 1]
        for g, h in enumerate(heads):
            outs[h] = o[g * tq:(g + 1) * tq, :]
    o_r[...] = jnp.concatenate(outs, axis=1).astype(BF16)


def _attention(qkv, row0, n_batch, t, q_gain, k_gain, cache=None, rope_tabs=None):
    tq = ATTN_Q_TILE
    nq = t // tq
    qw, kw = A_HEADS * HEAD_DIM, A_KV_HEADS * HEAD_DIM
    rope = cache is not None
    q_spec = pl.BlockSpec((tq, qw), lambda b, qi: (row0 // tq + b * nq + qi, 0))
    k_spec = pl.BlockSpec((t, kw), lambda b, qi: (row0 // t + b, qw // kw))
    v_spec = pl.BlockSpec((t, kw), lambda b, qi: (row0 // t + b, qw // kw + 1))
    gq = pl.BlockSpec((1, qw), lambda b, qi: (0, 0))
    gk = pl.BlockSpec((1, kw), lambda b, qi: (0, 0))
    out_o = pl.BlockSpec((tq, qw), lambda b, qi: (b * nq + qi, 0))
    o_shape = jax.ShapeDtypeStruct((n_batch * t, qw), BF16)
    if rope:
        ck, cv, e = cache
        past = ck.shape[2]
        c_spec = pl.BlockSpec((None, None, past, kw), lambda b, qi: (b, e, 0, 0))
        cq, sq, ckk, skk = rope_tabs
        in_specs = [q_spec, k_spec, v_spec, c_spec, c_spec, gq, gk,
                    pl.BlockSpec((tq, qw), lambda b, qi: (qi, 0)), pl.BlockSpec((tq, qw), lambda b, qi: (qi, 0)),
                    pl.BlockSpec((t, kw), lambda b, qi: (0, 0)), pl.BlockSpec((t, kw), lambda b, qi: (0, 0))]
        args = [qkv, qkv, qkv, ck, cv, q_gain, k_gain, cq, sq, ckk, skk]
        out_specs, out_shape = out_o, o_shape
    else:
        in_specs = [q_spec, k_spec, v_spec, gq, gk]
        args = [qkv, qkv, qkv, q_gain, k_gain]
        out_specs = [out_o, pl.BlockSpec((t, kw), lambda b, qi: (b, 0))]
        out_shape = [o_shape, jax.ShapeDtypeStruct((n_batch * t, kw), F32)]
        past = 0
    n_keys = past + t
    return pl.pallas_call(
        functools.partial(_attn_body, rope, tq),
        grid=(n_batch, nq),
        in_specs=in_specs,
        out_specs=out_specs,
        out_shape=out_shape,
        scratch_shapes=[pltpu.VMEM((n_keys, kw), BF16), pltpu.VMEM((A_KV_HEADS, n_keys, 2 * HEAD_DIM), BF16)],
        compiler_params=_params(2),
        name="gqa_latent" if rope else "gqa_context",
    )(*args)


def _hgrn_body(zero_init, t, *refs):
    if zero_init:
        (q_r, ff_r, fb_r, v_r, g_r, lb_r, gain_r, ob_o, send_o,
         oi_scr, u_scr, dec_scr, qt_scr, sst_scr, on_scr) = refs
        s0_r = None
    else:
        (q_r, ff_r, fb_r, v_r, g_r, lb_r, gain_r, s0_r, ob_o, send_o,
         oi_scr, u_scr, dec_scr, qt_scr, sst_scr, on_scr) = refs
    nt = t // LANES
    hd = HEAD_DIM
    pw = 2 * hd
    cpt = LANES // GLA_CHUNK
    lane = lax.broadcasted_iota(jnp.int32, (1, LANES), 1)
    pos = lane & (GLA_CHUNK - 1)
    ri = lax.broadcasted_iota(jnp.int32, (LANES, LANES), 0)
    ci = lax.broadcasted_iota(jnp.int32, (LANES, LANES), 1)
    ch_sh = GLA_CHUNK.bit_length() - 1
    hd_sh = hd.bit_length() - 1
    pw_sh = pw.bit_length() - 1
    same = (ri >> ch_sh) == (ci >> ch_sh)
    same_f = jnp.where(same, 1.0, 0.0)

    def ind(cond):
        return jnp.where(cond, 1.0, 0.0).astype(BF16)

    m_fwd = jnp.concatenate([ind(same & (ri <= ci)), ind(same & (ri > ci))], axis=1)
    m_bwd = jnp.concatenate([ind(same & (ri >= ci)), ind(same & (ri < ci))], axis=1)
    eye = ind(ri == ci)
    sel_r = lax.broadcasted_iota(jnp.int32, (cpt, LANES), 0)
    sel_c = lax.broadcasted_iota(jnp.int32, (cpt, LANES), 1)
    sel = ind((sel_c >> ch_sh) == sel_r)
    bd_r = lax.broadcasted_iota(jnp.int32, (cpt * pw, LANES), 0)
    bd_c = lax.broadcasted_iota(jnp.int32, (cpt * pw, LANES), 1)
    bd = (bd_r >> pw_sh) == (bd_c >> ch_sh)
    top = ((bd_r >> hd_sh) & 1) == 0
    bd_all = ind(bd)
    bd_top = ind(bd & top)
    bd_bot = ind(bd & jnp.logical_not(top))
    u_lane = lax.broadcasted_iota(jnp.int32, (1, cpt * pw), 1)
    u_top = ((u_lane >> hd_sh) & 1) == 0
    gain2 = jnp.concatenate([gain_r[...], gain_r[...]], axis=0)

    def colsum(a):
        return jnp.sum(a, axis=0, keepdims=True)

    def dot(a, b):
        return jnp.dot(a, b, preferred_element_type=F32)

    def pair_step(pr, carry):
        rows = pl.ds(pl.multiple_of(pr * pw, pw), pw)

        def phase_a(ti, c):
            ls = pl.ds(pl.multiple_of(ti * LANES, LANES), LANES)
            q = q_r[rows, ls]
            v = v_r[rows, ls]
            vb = v.astype(BF16)
            skews = [None, None]
            for bwd in (False, True):
                d_idx = int(bwd)
                raw_r = fb_r if bwd else ff_r
                lbv = lb_r[d_idx, rows, :]
                f = lbv + (1.0 - lbv) * jax.nn.sigmoid(raw_r[rows, ls])
                pm = (GLA_CHUNK - 1 - pos) if bwd else pos
                g_prev = [colsum(q[:hd]), colsum(q[hd:])]
                qd = q
                a_rows = [[], []]
                for dl in range(GLA_CHUNK):
                    qd = qd * (pltpu.roll(f, (LANES - dl) if bwd else dl, 1) if dl > 0 else f)
                    g_cur = [colsum(qd[:hd]), colsum(qd[hd:])]
                    for k in range(2):
                        a_rows[k].append(jnp.where(pm >= dl, g_prev[k] - g_cur[k], 0.0))
                    g_prev = g_cur
                for k in range(2):
                    order = a_rows[k] if bwd else [a_rows[k][0]] + a_rows[k][:0:-1]
                    stack = jnp.concatenate(order * cpt, axis=0).astype(BF16)
                    tall = lax.dot_general(eye, stack, _NT, preferred_element_type=F32)
                    sk = pltpu.roll(tall, 0, 1, stride=1, stride_axis=0)
                    skews[k] = sk if skews[k] is None else skews[k] + sk
                hi, mid, lo = _split3(jnp.log(f))
                m_cs = m_bwd if bwd else m_fwd
                cs = dot(hi, m_cs) + dot(mid, m_cs) + dot(lo, m_cs)
                qt_scr[d_idx, :, ls] = q * jnp.exp(cs[:, :LANES])
                kt = (1.0 - f) * jnp.exp(cs[:, LANES:])
                dsum = (lax.dot_general(sel, hi, _NT, preferred_element_type=F32)
                        + lax.dot_general(sel, mid, _NT, preferred_element_type=F32)
                        + lax.dot_general(sel, lo, _NT, preferred_element_type=F32))
                dec_scr[d_idx, ti] = jnp.exp(dsum)
                kbd = jnp.concatenate([kt.astype(BF16)] * cpt, axis=0) * bd_all
                uo = lax.dot_general(vb, kbd, _NT, preferred_element_type=F32)
                u_scr[d_idx, ti] = jnp.where(u_top, uo[:hd], uo[hd:])
            amats = [(sk * same_f).astype(BF16) for sk in skews]
            oi_scr[:, ls] = jnp.concatenate(
                [lax.dot_general(vb[:hd], amats[0], _NT, preferred_element_type=F32),
                 lax.dot_general(vb[hd:], amats[1], _NT, preferred_element_type=F32)], axis=0)
            return c

        lax.fori_loop(0, nt, phase_a, 0, unroll=2)

        def phase_b(ti, st):
            sf, sb = st
            u = u_scr[0, ti]
            dec = dec_scr[0, ti]
            pieces = [None] * cpt
            for c in range(cpt):
                pieces[c] = sf
                sf = dec[c:c + 1, :] * sf + u[:, c * pw:(c + 1) * pw]
            sst_scr[0, ti] = jnp.concatenate(pieces, axis=1).astype(BF16)
            tb = nt - 1 - ti
            u = u_scr[1, tb]
            dec = dec_scr[1, tb]
            for c in range(cpt - 1, -1, -1):
                pieces[c] = sb
                sb = dec[c:c + 1, :] * sb + u[:, c * pw:(c + 1) * pw]
            sst_scr[1, tb] = jnp.concatenate(pieces, axis=1).astype(BF16)
            return sf, sb

        if zero_init:
            s0 = (jnp.zeros((hd, pw), F32), jnp.zeros((hd, pw), F32))
        else:
            s0 = tuple(jnp.concatenate([s0_r[k, 2 * pr], s0_r[k, 2 * pr + 1]], axis=1) for k in range(2))
        s_end = lax.fori_loop(0, nt, phase_b, s0)
        for k in range(2):
            send_o[k, 2 * pr] = s_end[k][:, :hd]
            send_o[k, 2 * pr + 1] = s_end[k][:, hd:]

        def phase_c(ti, c):
            ls = pl.ds(pl.multiple_of(ti * LANES, LANES), LANES)
            o = oi_scr[:, ls]
            for k in range(2):
                t8 = jnp.concatenate([qt_scr[k, :, ls].astype(BF16)] * cpt, axis=0)
                qb = jnp.concatenate([t8 * bd_top, t8 * bd_bot], axis=1)
                oo = dot(sst_scr[k, ti], qb)
                o = o + jnp.concatenate([oo[:, :LANES], oo[:, LANES:]], axis=0)
            o0, o1 = o[:hd], o[hd:]
            on = jnp.concatenate([o0 * lax.rsqrt(jnp.mean(o0 * o0, axis=0, keepdims=True) + EPS),
                                  o1 * lax.rsqrt(jnp.mean(o1 * o1, axis=0, keepdims=True) + EPS)], axis=0)
            on_scr[rows, ls] = on * gain2 * _silu(g_r[rows, ls])
            return c

        lax.fori_loop(0, nt, phase_c, 0, unroll=2)
        return carry

    lax.fori_loop(0, B_HEADS // 2, pair_step, 0)
    ob_o[...] = on_scr[...].T.astype(BF16)


def _hgrn(hgt, col0, n_batch, t, lb_b, gain_b, s0t):
    bw = B_HEADS * HEAD_DIM
    hd, pw = HEAD_DIM, 2 * HEAD_DIM
    nt = t // LANES
    cpt = LANES // GLA_CHUNK
    zero_init = s0t is None
    specs = [pl.BlockSpec((bw, t), functools.partial(lambda r, b: (r, col0 // t + b), r)) for r in range(5)]
    in_specs = specs + [_full_spec(lb_b.shape), _full_spec(gain_b.shape)]
    args = [hgt] * 5 + [lb_b, gain_b]
    st_spec = pl.BlockSpec((None, 2, B_HEADS, HEAD_DIM, HEAD_DIM), lambda b: (b, 0, 0, 0, 0))
    if not zero_init:
        in_specs.append(st_spec)
        args.append(s0t)
    scratch = [pltpu.VMEM((pw, t), F32), pltpu.VMEM((2, nt, hd, cpt * pw), F32), pltpu.VMEM((2, nt, cpt, pw), F32),
               pltpu.VMEM((2, pw, t), F32), pltpu.VMEM((2, nt, hd, cpt * pw), BF16), pltpu.VMEM((bw, t), F32)]
    return pl.pallas_call(
        functools.partial(_hgrn_body, zero_init, t),
        grid=(n_batch,),
        in_specs=in_specs,
        out_specs=[pl.BlockSpec((t, bw), lambda b: (b, 0)), st_spec],
        out_shape=[jax.ShapeDtypeStruct((n_batch * t, bw), BF16),
                   jax.ShapeDtypeStruct((n_batch, 2, B_HEADS, HEAD_DIM, HEAD_DIM), F32)],
        scratch_shapes=scratch,
        compiler_params=_params(1),
        name="hgrn2_context" if zero_init else "hgrn2_latent",
    )(*args)


def _route(logits):
    lane_i = lax.broadcasted_iota(jnp.int32, logits.shape, 1)
    lane = lane_i.astype(F32)
    neg = -jnp.inf
    big = 1e9
    gl = jnp.where(lane_i < MOE_GROUPS, logits, neg)
    gmax = jnp.max(gl, axis=-1, keepdims=True)
    g_val = 1.0 / jnp.sum(jnp.exp(gl - gmax), axis=-1, keepdims=True)
    g_idx = jnp.min(jnp.where(gl == gmax, lane, big), axis=-1, keepdims=True)
    n_e = MOE_GROUPS * EXPERTS_PER_GROUP
    lane_group = ((lane_i - MOE_GROUPS) >> 2).astype(F32)
    in_group = (lane_i >= MOE_GROUPS) & (lane_i < MOE_GROUPS + n_e) & (lane_group == g_idx)
    el = jnp.where(in_group, logits, neg)
    m1 = jnp.max(el, axis=-1, keepdims=True)
    i1 = jnp.min(jnp.where(el == m1, lane, big), axis=-1, keepdims=True)
    el2 = jnp.where(lane == i1, neg, el)
    m2 = jnp.max(el2, axis=-1, keepdims=True)
    i2 = jnp.min(jnp.where(el2 == m2, lane, big), axis=-1, keepdims=True)
    tt = jnp.exp(m2 - m1)
    w1 = g_val / (1.0 + tt)
    w2 = g_val * tt / (1.0 + tt)
    base = MOE_GROUPS + EXPERTS_PER_GROUP * g_idx
    route = jnp.where(lane == i1 - base, w1, 0.0) + jnp.where(lane == i2 - base, w2, 0.0)
    return jnp.where(lane_i == ROUTE_GID_LANE, g_idx, route)


def _proj_res_body(geom, lhs_n, n_x, has_y, *refs):
    lhs = []
    for n_a in lhs_n:
        lhs.append(geom.tok_load(refs[:n_a]))
        refs = refs[n_a:]
    ws, refs = refs[:len(lhs_n)], refs[len(lhs_n):]
    xs, refs = refs[:n_x], refs[n_x:]
    x = geom.tok_load(xs)
    if has_y:
        y, g2, refs = refs[0], refs[1], refs[2:]
        x = x + g2[...] * _tile_rows_load(y, TOKEN_TILE)
    g1, gn2, sh2, sc2, wrh, wrl, br, xnew_o, xr_o, route_o = refs[:10]
    wbs = refs[10:]

    @pl.when(pl.program_id(0) == 0)
    def _():
        for w, wb in zip(ws, wbs):
            wb[...] = w[...].astype(BF16)

    acc = jnp.dot(lhs[0], wbs[0][...], preferred_element_type=F32)
    for a, wb in zip(lhs[1:], wbs[1:]):
        acc = acc + jnp.dot(a, wb[...], preferred_element_type=F32)
    xn = x + g1[...] * acc
    xnew_o[...] = xn
    hm = _modulate(xn, gn2[...], sh2[...], sc2[...])
    hi = hm.astype(BF16)
    lo = (hm - hi.astype(F32)).astype(BF16)
    logits = (jnp.dot(hi, wrh[...], preferred_element_type=F32) + jnp.dot(lo, wrh[...], preferred_element_type=F32)
              + jnp.dot(hi, wrl[...], preferred_element_type=F32)) + br[...]
    route = _route(logits)
    for s, piece in enumerate(_pack_routed(hm, route)):
        xr_o[pl.ds(s, TOKEN_TILE, stride=ROUTED_SUB), :] = piece
    route_o[...] = route


def _proj_res(geom, mod, layer, lhs, ws, xa, y, gn2, wrh, wrl, br, name):
    tm = TOKEN_TILE
    d = geom.d
    has_y = y is not None
    tok = pl.BlockSpec((tm, d), lambda i: (i, 0))
    in_specs, args = [], []
    for a in lhs:
        s, ar = geom.tok_specs(a)
        in_specs += s
        args += ar
    w_shapes = []
    for w_all, lead, row_block in ws:
        in_specs.append(_resident_slab(w_all.shape, lead, row_block))
        args.append(w_all)
        w_shapes.append((w_all.shape[1] if row_block is None else row_block[1],) + tuple(w_all.shape[2:]))
    s, ar = geom.tok_specs(xa)
    in_specs += s
    args += ar
    if has_y:
        in_specs += [_Y_SPEC, geom.mod_spec(layer - 1, 5)]
        args += [y, mod]
    in_specs += [geom.mod_spec(layer, 2), _layer_spec(gn2.shape, layer), geom.mod_spec(layer, 3),
                 geom.mod_spec(layer, 4), _layer_spec(wrh.shape, layer), _layer_spec(wrl.shape, layer),
                 _layer_spec(br.shape, layer)]
    args += [mod, gn2, mod, mod, wrh, wrl, br]
    return pl.pallas_call(
        functools.partial(_proj_res_body, geom, tuple(_n_tok(a) for a in lhs), _n_tok(xa), has_y),
        grid=(geom.n // tm,),
        in_specs=in_specs,
        out_specs=[tok, pl.BlockSpec((tm * ROUTED_SUB, LANES), lambda i: (i, 0)),
                   pl.BlockSpec((tm, ROUTE_W), lambda i: (i, 0))],
        out_shape=[jax.ShapeDtypeStruct((geom.n, d), F32), jax.ShapeDtypeStruct((geom.n * ROUTED_SUB, LANES), F32),
                   jax.ShapeDtypeStruct((geom.n, ROUTE_W), F32)],
        scratch_shapes=[pltpu.VMEM(shp, BF16) for shp in w_shapes],
        compiler_params=_params(1),
        name=name,
    )(*args)


def _dispatch(route, n, tile):
    n_tiles = n // tile + MOE_GROUPS
    n_slots = n_tiles * tile
    gid = route[:, ROUTE_GID_LANE].astype(jnp.int32)
    onehot = (gid[:, None] == jnp.arange(MOE_GROUPS, dtype=jnp.int32)[None, :]).astype(jnp.int32)
    csum = jnp.cumsum(onehot, axis=0)
    counts = csum[-1]
    rank = jnp.sum(csum * onehot, axis=1) - 1
    tiles_g = (counts + tile - 1) // tile
    tile_end = jnp.cumsum(tiles_g)
    tile_start = tile_end - tiles_g
    dest = (tile_start * tile)[gid] + rank
    tok = jnp.arange(n, dtype=jnp.int32)
    src = jnp.zeros((n_slots,), jnp.int32).at[dest].set(tok)
    t_idx = jnp.arange(n_tiles, dtype=jnp.int32)
    tile_gid = jnp.minimum(jnp.sum((t_idx[:, None] >= tile_end[None, :]).astype(jnp.int32), axis=1), MOE_GROUPS - 1)
    tile_rows = jnp.clip(counts[tile_gid] - (t_idx - tile_start[tile_gid]) * tile, 0, tile)
    tile_rows = jnp.where(t_idx < tile_end[-1], tile_rows, 0)
    in_tile = jnp.arange(tile, dtype=jnp.int32)
    valid = (in_tile[None, :] < tile_rows[:, None]).reshape(n_slots)
    spare = (n + (t_idx[:, None] & 1) * tile + in_tile[None, :]).reshape(n_slots)
    dst = jnp.concatenate([n + tile + in_tile, jnp.where(valid, src, spare)])
    return tile_gid, src, dst, tile_end[-1:].astype(jnp.int32), n_tiles


def _moe_body(tile, d, n_rows, tg_ref, src_ref, dst_ref, nu_ref, xr_hbm, w1_ref, w3_ref, w2_ref, y_hbm,
              xbuf, ybuf, w1b, w3b, w2b, isem, osem):
    t = pl.program_id(0)
    last = nu_ref[0] - 1
    slot = t % 2
    other = 1 - slot

    sub = SUBLANES
    rsub = ROUTED_SUB

    def hbm_rows(ref, row, n_sub):
        start = row * n_sub if isinstance(row, int) else pl.multiple_of(row * n_sub, n_sub)
        return ref.at[pl.ds(start, n_sub), :]

    def row_in(tile_idx, s, i):
        row = src_ref[tile_idx * tile + i]
        return pltpu.make_async_copy(hbm_rows(xr_hbm, row, rsub), xbuf.at[s, pl.ds(i * rsub, rsub), :], isem.at[s])

    def row_out(ext_tile_idx, s, i):
        row = dst_ref[ext_tile_idx * tile + i]
        return pltpu.make_async_copy(ybuf.at[s, pl.ds(i * sub, sub), :], hbm_rows(y_hbm, row, sub), osem.at[s])

    @pl.when(t == 0)
    def _():
        ybuf[...] = jnp.zeros(ybuf.shape, F32)
        for i in range(tile):
            row_in(0, 0, i).start(priority=i % 2)
        for i in range(tile):
            pltpu.make_async_copy(ybuf.at[0, pl.ds(i * sub, sub), :], hbm_rows(y_hbm, n_rows + i, sub),
                                  osem.at[0]).start(priority=i % 2)

    g = tg_ref[t]
    g_prev = tg_ref[jnp.maximum(t - 1, 0)]

    @pl.when(((t == 0) | (g != g_prev)) & (t <= last))
    def _():
        w1b[...] = w1_ref[...].astype(BF16)
        w3b[...] = w3_ref[...].astype(BF16)
        w2b[...] = w2_ref[...].astype(BF16)

    nxt = jnp.minimum(t + 1, last)
    n_parts = 4 * EXPERTS_PER_GROUP
    per = tile // n_parts

    def step(slot, other):
        for i in range(tile):
            row_in(0, slot, i).wait()
        part = [0]

        def issue_part():
            k = part[0]
            part[0] += 1
            for i in range(k * per, (k + 1) * per):
                row_in(nxt, other, i).start(priority=i % 2)
                row_out(t, other, i).start(priority=i % 2)

        xb, cw = _unpack_routed(xbuf.at[slot], tile)
        acc = jnp.zeros((tile, d), F32)
        for e in range(EXPERTS_PER_GROUP):
            a = jnp.dot(xb, w1b[e], preferred_element_type=F32)
            issue_part()
            b = jnp.dot(xb, w3b[e], preferred_element_type=F32)
            issue_part()
            hid = (_silu(a) * b * cw[:, e:e + 1]).astype(BF16)
            issue_part()
            acc = acc + jnp.dot(hid, w2b[e], preferred_element_type=F32)
            issue_part()
        for i in range(tile):
            row_out(0, slot, i).wait()
        _tile_rows_store(ybuf.at[slot], acc)

        @pl.when(t == last)
        def _():
            for i in range(tile):
                row_in(0, other, i).wait()
            for i in range(tile):
                row_out(0, other, i).wait()
            for i in range(tile):
                row_out(t + 1, slot, i).start(priority=i % 2)
            for i in range(tile):
                row_out(0, slot, i).wait()

    @pl.when((slot == 0) & (t <= last))
    def _():
        step(0, 1)

    @pl.when((slot == 1) & (t <= last))
    def _():
        step(1, 0)


def _moe(xr, route, w1, w3, w2, layer, n, d):
    tile = MOE_TILE
    assert d == SUBLANES * LANES
    tile_gid, src, dst, n_used, n_tiles = _dispatch(route, n, tile)
    g0 = layer * MOE_GROUPS
    wspec13 = pl.BlockSpec((None,) + w1.shape[1:], lambda t, tg, s, ds_, nu: (g0 + tg[t], 0, 0, 0))
    wspec2 = pl.BlockSpec((None,) + w2.shape[1:], lambda t, tg, s, ds_, nu: (g0 + tg[t], 0, 0, 0))
    grid_spec = pltpu.PrefetchScalarGridSpec(
        num_scalar_prefetch=4,
        grid=(n_tiles,),
        in_specs=[pl.BlockSpec(memory_space=pl.ANY), wspec13, wspec13, wspec2],
        out_specs=pl.BlockSpec(memory_space=pl.ANY),
        scratch_shapes=[pltpu.VMEM((2, tile * ROUTED_SUB, LANES), F32), pltpu.VMEM((2, tile * SUBLANES, LANES), F32),
                        pltpu.VMEM(w1.shape[1:], BF16), pltpu.VMEM(w3.shape[1:], BF16), pltpu.VMEM(w2.shape[1:], BF16),
                        pltpu.SemaphoreType.DMA((2,)), pltpu.SemaphoreType.DMA((2,))],
    )
    return pl.pallas_call(
        functools.partial(_moe_body, tile, d, n),
        grid_spec=grid_spec,
        out_shape=jax.ShapeDtypeStruct(((n + 2 * tile) * SUBLANES, LANES), F32),
        compiler_params=_params(1),
        name="moe_ffn",
    )(tile_gid, src, dst, n_used, xr, w1, w3, w2)


def _glu_body(xa, y, g2, gn, sh, sc, w, u_o, wb):
    @pl.when(pl.program_id(0) == 0)
    def _():
        wb[...] = w[...].astype(BF16)

    x = xa[...] + g2[...] * _tile_rows_load(y, TOKEN_TILE)
    h = _modulate(x, gn[...], sh[...], sc[...]).astype(BF16)
    ab = jnp.dot(h, wb[...], preferred_element_type=F32)
    c = ab.shape[1] // 2
    u_o[...] = ab[:, :c] * jax.nn.sigmoid(ab[:, c:])


def _glu(geom, mod, layer, xa, y, gn, w_all, o):
    tm = TOKEN_TILE
    d = geom.d
    tok = pl.BlockSpec((tm, d), lambda i: (i, 0))
    c = w_all.shape[2] // 2
    return pl.pallas_call(
        _glu_body,
        grid=(geom.n // tm,),
        in_specs=[tok, _Y_SPEC, geom.mod_spec(layer - 1, 5), _row_spec(d), geom.mod_spec(layer, 0),
                  geom.mod_spec(layer, 1), _resident_slab(w_all.shape, o)],
        out_specs=pl.BlockSpec((tm, c), lambda i: (i, 0)),
        out_shape=jax.ShapeDtypeStruct((geom.n, c), F32),
        scratch_shapes=[pltpu.VMEM(w_all.shape[1:], BF16)],
        compiler_params=_params(1),
        name="conv_glu",
    )(xa, y, mod, gn, mod, mod, w_all)


def _dwconv_body(t, u_r, dw_r, b_r, g_r, be_r, o_r, pad_scr, cv_scr, sh_scr):
    c = u_r.shape[1]
    halo = 2 * SUBLANES
    sh_len = sh_scr.shape[1]
    pad_scr[0:halo, :] = jnp.zeros((halo, c), F32)
    pad_scr[t + halo:t + 2 * halo, :] = jnp.zeros((halo, c), F32)
    pad_scr[halo:t + halo, :] = u_r[...]
    rb_n = CONV_ROWS
    n_lb = c // LANES
    first = halo - CONV_WIDTH // 2

    def blk(lb, carry):
        ls = pl.ds(pl.multiple_of(lb * LANES, LANES), LANES)
        bias = b_r[:, ls]
        for r0 in range(0, t, rb_n):
            win = pad_scr[r0:r0 + rb_n + 2 * halo, ls]
            for sft in range(1, SUBLANES):
                sh_scr[sft] = pltpu.roll(win, win.shape[0] - sft, 0)[:sh_len, :]
            acc = jnp.zeros((rb_n, LANES), F32)
            for k in range(CONV_WIDTH):
                sft = (first + k) % SUBLANES
                a = first + k - sft
                src = pad_scr[r0 + a:r0 + a + rb_n, ls] if sft == 0 else sh_scr[sft, a:a + rb_n, :]
                acc = acc + src * dw_r[k:k + 1, ls]
            cv_scr[r0:r0 + rb_n, ls] = acc + bias
        return carry

    lax.fori_loop(0, n_lb, blk, 0)

    def ln(rb, carry):
        rs = pl.ds(pl.multiple_of(rb * rb_n, rb_n), rb_n)
        x = cv_scr[rs, :]
        mu = jnp.mean(x, axis=-1, keepdims=True)
        xc = x - mu
        var = jnp.mean(xc * xc, axis=-1, keepdims=True)
        yv = xc * lax.rsqrt(var + EPS) * g_r[...] + be_r[...]
        o_r[rs, :] = _silu(yv).astype(BF16)
        return carry

    lax.fori_loop(0, t // rb_n, ln, 0, unroll=2)


def _dwconv(u, row0, n_batch, t, dw, b, g, be):
    c = u.shape[1]
    return pl.pallas_call(
        functools.partial(_dwconv_body, t),
        grid=(n_batch,),
        in_specs=[pl.BlockSpec((t, c), lambda bi: (row0 // t + bi, 0)), _full_spec(dw.shape),
                  _row_spec(c), _row_spec(c), _row_spec(c)],
        out_specs=pl.BlockSpec((t, c), lambda bi: (bi, 0)),
        out_shape=jax.ShapeDtypeStruct((n_batch * t, c), BF16),
        scratch_shapes=[pltpu.VMEM((t + 4 * SUBLANES, c), F32), pltpu.VMEM((t, c), F32),
                        pltpu.VMEM((SUBLANES, CONV_ROWS + (CONV_WIDTH // SUBLANES) * SUBLANES, LANES), F32)],
        compiler_params=_params(1),
        name="conv_dw",
    )(u, dw, b, g, be)


def _final_body(ctx_tiles, xa, y, g2, gf, oc, os_):
    x = xa[...] + g2[...] * _tile_rows_load(y, TOKEN_TILE)
    ms = jnp.mean(x * x, axis=-1, keepdims=True)
    val = x * lax.rsqrt(ms + EPS) * gf[...]
    i = pl.program_id(0)

    @pl.when(i < ctx_tiles)
    def _():
        oc[...] = val

    @pl.when(i >= ctx_tiles)
    def _():
        os_[...] = val


def _final(geom, mod, depth, xa, y, gf):
    tm = TOKEN_TILE
    d = geom.d
    ct = geom.ctx_tiles
    tok = pl.BlockSpec((tm, d), lambda i: (i, 0))
    return pl.pallas_call(
        functools.partial(_final_body, ct),
        grid=(geom.n // tm,),
        in_specs=[tok, _Y_SPEC, geom.mod_spec(depth - 1, 5), _row_spec(d)],
        out_specs=[pl.BlockSpec((tm, d), lambda i: (jnp.minimum(i, ct - 1), 0)),
                   pl.BlockSpec((tm, d), lambda i: (jnp.maximum(i - ct, 0), 0))],
        out_shape=[jax.ShapeDtypeStruct((geom.n_ctx, d), F32), jax.ShapeDtypeStruct((geom.n_smp, d), F32)],
        compiler_params=_params(1),
        name="final_norm",
    )(xa, y, mod, gf)


def _rope_tables(n_tokens):
    rows = n_tokens // GRID_W
    row = jnp.repeat(jnp.arange(rows, dtype=F32), GRID_W)
    col = jnp.tile(jnp.arange(GRID_W, dtype=F32), rows)
    half = HEAD_DIM // 2
    inv = ROPE_THETA ** (-jnp.arange(0, half, 2, dtype=F32) / half)
    ang_r = row[:, None] * inv[None, :]
    ang_c = col[:, None] * inv[None, :]
    cr, sr, cc, sc = jnp.cos(ang_r), jnp.sin(ang_r), jnp.cos(ang_c), jnp.sin(ang_c)
    cos = jnp.concatenate([cr, cr, cc, cc], axis=1)
    sin = jnp.concatenate([-sr, sr, -sc, sc], axis=1)
    return (jnp.tile(cos, (1, A_HEADS)), jnp.tile(sin, (1, A_HEADS)),
            jnp.tile(cos, (1, A_KV_HEADS)), jnp.tile(sin, (1, A_KV_HEADS)))


def _router_weights(w_group, b_group, w_expert, b_expert):
    depth, d, _ = w_group.shape
    n_used = MOE_GROUPS + MOE_GROUPS * EXPERTS_PER_GROUP
    w = jnp.concatenate([w_group, w_expert, jnp.zeros((depth, d, ROUTE_W - n_used), F32)], axis=2)
    b = jnp.concatenate([b_group, b_expert, jnp.zeros((depth, ROUTE_W - n_used), F32)], axis=1)[:, None, :]
    hi = w.astype(BF16)
    lo = (w - hi.astype(F32)).astype(BF16)
    return hi, lo, b


def kernel(x_prompt, x_sample, cache_k, cache_v, state_hgrn, c, c_ctx, ada_w, ada_b, norm1_g, norm2_g, mix_w_in, mix_w_out, attn_q_gain, attn_k_gain, hgrn_lower_bound, hgrn_out_gain, conv_pw1, conv_dw, conv_dw_b, conv_ln_g, conv_ln_b, conv_pw2, moe_w_group, moe_b_group, moe_w_expert, moe_b_expert, moe_w1, moe_w3, moe_w2, final_g):
    batch, seq, d = x_prompt.shape
    dec_batch, dec_seq, _ = x_sample.shape
    depth = ada_w.shape[0]
    past = cache_k.shape[2]
    n_ctx, n_smp = batch * seq, dec_batch * dec_seq
    n = n_ctx + n_smp
    geom = _Geom(n_ctx, n_smp, dec_seq, d)
    qw, kw, bw = A_HEADS * HEAD_DIM, A_KV_HEADS * HEAD_DIM, B_HEADS * HEAD_DIM
    n_nat = qw + 2 * kw

    cond8 = jnp.concatenate([c_ctx[None, :], c, jnp.zeros((8 - 1 - dec_batch, d), F32)], axis=0)
    mod = _ada(cond8, ada_w, ada_b).reshape(depth, 8, 6, 1, d)

    lb_soft = jax.nn.softmax(hgrn_lower_bound.astype(F32), axis=0)
    lower_bounds = jnp.cumsum(lb_soft, axis=0) - lb_soft[0]
    rope_tabs = _rope_tables(dec_seq)

    xa = (x_prompt.reshape(n_ctx, d), x_sample.reshape(n_smp, d))
    y = None
    f = moe_w1.shape[-1]
    w1 = moe_w1.reshape(depth * MOE_GROUPS, EXPERTS_PER_GROUP, d, f)
    w3 = moe_w3.reshape(depth * MOE_GROUPS, EXPERTS_PER_GROUP, d, f)
    w2 = moe_w2.reshape(depth * MOE_GROUPS, EXPERTS_PER_GROUP, f, d)
    n_ab = cache_k.shape[1]
    ck = cache_k.reshape(dec_batch, n_ab, past, kw)
    cv = cache_v.reshape(dec_batch, n_ab, past, kw)
    wrh, wrl, br = _router_weights(moe_w_group, moe_b_group, moe_w_expert, moe_b_expert)
    gn2 = norm2_g[:, None, :]
    new_k, new_v, new_s = [], [], []
    for l in range(depth):
        gn1 = norm1_g[l][None, :]
        if l % 2 == 0:
            e = l // 2
            qkv, hgt = _inproj(geom, mod, l, xa, y, gn1, mix_w_in, e, n_nat)
            qg = jnp.tile(attn_q_gain[e], A_HEADS)[None, :]
            kg = jnp.tile(attn_k_gain[e], A_KV_HEADS)[None, :]
            oa_c, khat = _attention(qkv, 0, batch, seq, qg, kg)
            oa_s = _attention(qkv, n_ctx, dec_batch, dec_seq, qg, kg, cache=(ck, cv, e), rope_tabs=rope_tabs)
            lb_b = jnp.broadcast_to(lower_bounds[e][:, :, None], (2, bw, LANES))
            gain_b = jnp.broadcast_to(hgrn_out_gain[e][:, None], (HEAD_DIM, LANES))
            ob_c, s_ctx = _hgrn(hgt, 0, batch, seq, lb_b, gain_b, None)
            s0t = jnp.swapaxes(state_hgrn[:, e], -1, -2)
            ob_s, _ = _hgrn(hgt, n_ctx, dec_batch, dec_seq, lb_b, gain_b, s0t)
            w_halves = [(mix_w_out, e, (0, qw)), (mix_w_out, e, (1, bw))]
            xnew, xr, route = _proj_res(geom, mod, l, [(oa_c, oa_s), (ob_c, ob_s)], w_halves, xa, y,
                                        gn2, wrh, wrl, br, "ab_outproj")
            new_k.append(khat.reshape(batch, seq, A_KV_HEADS, HEAD_DIM))
            new_v.append(qkv[:n_ctx, qw + kw:].reshape(batch, seq, A_KV_HEADS, HEAD_DIM))
            new_s.append(jnp.swapaxes(s_ctx, -1, -2))
        else:
            o = l // 2
            u = _glu(geom, mod, l, xa, y, gn1, conv_pw1, o)
            cargs = (conv_dw[o], conv_dw_b[o][None, :], conv_ln_g[o][None, :], conv_ln_b[o][None, :])
            c_c = _dwconv(u, 0, batch, seq, *cargs)
            c_s = _dwconv(u, n_ctx, dec_batch, dec_seq, *cargs)
            xnew, xr, route = _proj_res(geom, mod, l, [(c_c, c_s)], [(conv_pw2, o, None)], xa, y, gn2, wrh, wrl, br,
                                        "conv_outproj")
        y = _moe(xr, route, w1, w3, w2, l, n, d)
        xa = xnew
    y_prompt, y_sample = _final(geom, mod, depth, xa, y, final_g[None, :])
    return (y_prompt.reshape(batch, seq, d), y_sample.reshape(dec_batch, dec_seq, d),
            jnp.stack(new_k, axis=1), jnp.stack(new_v, axis=1), jnp.stack(new_s, axis=1))
```

```python
import functools

import jax
import jax.numpy as jnp
from jax import lax
from jax.experimental import pallas as pl
from jax.experimental.pallas import tpu as pltpu

F32 = jnp.float32
BF16 = jnp.bfloat16

HEAD_DIM = 64
A_HEADS = 8
A_KV_HEADS = 2
A_GROUP = A_HEADS // A_KV_HEADS
B_HEADS = 8
GRID_W = 64
ROPE_THETA = 10000.0
GLA_CHUNK = 16
CONV_WIDTH = 31
MOE_GROUPS = 4
EXPERTS_PER_GROUP = 4
EPS = 1e-6

LANES = 128
SUBLANES = 8
VMEM_LIMIT = 56 * 1024 * 1024
TOKEN_TILE = 512
MOE_TILE = 256
ATTN_Q_TILE = 256
ROUTE_W = LANES
ROUTE_GID_LANE = 8
CONV_ROWS = 64

_NT = (((1,), (1,)), ((), ()))


def _params(n_axes):
    return pltpu.CompilerParams(dimension_semantics=("arbitrary",) * n_axes, vmem_limit_bytes=VMEM_LIMIT)


def _modulate(x, g, sh, sc):
    ms = jnp.mean(x * x, axis=-1, keepdims=True)
    return (x * lax.rsqrt(ms + EPS) * g) * (1.0 + sc) + sh


def _silu(x):
    return x * jax.nn.sigmoid(x)


def _tile_rows_load(ref, n_rows):
    return jnp.concatenate([ref[pl.ds(s, n_rows, stride=SUBLANES), :] for s in range(SUBLANES)], axis=1)


def _tile_rows_store(ref, val):
    n_rows = val.shape[0]
    for s in range(SUBLANES):
        ref[pl.ds(s, n_rows, stride=SUBLANES), :] = val[:, s * LANES:(s + 1) * LANES]


ROUTED_SUB = 2 * SUBLANES


def _pack_routed(hm, route):
    pieces = [hm[:, s * LANES:(s + 1) * LANES] for s in range(SUBLANES)] + [route]
    return pieces + [jnp.zeros(route.shape, F32)] * (ROUTED_SUB - len(pieces))


def _unpack_routed(ref, n_rows):
    chunks = [ref[pl.ds(s, n_rows, stride=ROUTED_SUB), :] for s in range(SUBLANES)]
    route = ref[pl.ds(SUBLANES, n_rows, stride=ROUTED_SUB), :]
    return jnp.concatenate(chunks, axis=1).astype(BF16), route


def _split3(x):
    hi = x.astype(BF16)
    r1 = x - hi.astype(F32)
    mid = r1.astype(BF16)
    lo = (r1 - mid.astype(F32)).astype(BF16)
    return hi, mid, lo


def _ada_body(c_ref, w_ref, b_ref, o_ref):
    s = _silu(c_ref[...])
    o_ref[...] = jnp.dot(s.astype(BF16), w_ref[...].astype(BF16), preferred_element_type=F32) + b_ref[...]


def _ada(cond8, ada_w, ada_b):
    depth, d, d6 = ada_w.shape
    tn = d6 // 4
    return pl.pallas_call(
        _ada_body,
        grid=(depth, d6 // tn),
        in_specs=[
            pl.BlockSpec((8, d), lambda l, j: (0, 0)),
            pl.BlockSpec((None, d, tn), lambda l, j: (l, 0, j)),
            pl.BlockSpec((None, 1, tn), lambda l, j: (l, 0, j)),
        ],
        out_specs=pl.BlockSpec((None, 8, tn), lambda l, j: (l, 0, j)),
        out_shape=jax.ShapeDtypeStruct((depth, 8, d6), F32),
        compiler_params=_params(2),
        name="adaln",
    )(cond8, ada_w, ada_b.reshape(depth, 1, d6))


class _Geom:
    def __init__(self, n_ctx, n_smp, smp_len, d):
        self.n_ctx, self.n_smp, self.smp_len, self.d = n_ctx, n_smp, smp_len, d
        self.n = n_ctx + n_smp
        self.ctx_tiles = n_ctx // TOKEN_TILE

    def mod_spec(self, layer, j):
        ctx_tiles = self.ctx_tiles
        per_smp = self.smp_len // TOKEN_TILE

        def idx(i):
            row = jnp.where(i < ctx_tiles, 0, 1 + (i - ctx_tiles) // per_smp)
            return (layer, row, j, 0, 0)

        return pl.BlockSpec((None, None, None, 1, self.d), idx)

    def tok_specs(self, a):
        tm = TOKEN_TILE
        if isinstance(a, tuple):
            ct = self.ctx_tiles
            w = a[0].shape[1]
            return ([pl.BlockSpec((tm, w), lambda i: (jnp.minimum(i, ct - 1), 0)),
                     pl.BlockSpec((tm, w), lambda i: (jnp.maximum(i - ct, 0), 0))], list(a))
        return [pl.BlockSpec((tm, a.shape[1]), lambda i: (i, 0))], [a]

    def tok_load(self, refs):
        if len(refs) == 1:
            return refs[0][...]
        return jnp.where(pl.program_id(0) < self.ctx_tiles, refs[0][...], refs[1][...])


def _n_tok(a):
    return 2 if isinstance(a, tuple) else 1


def _row_spec(d):
    return pl.BlockSpec((1, d), lambda i: (0, 0))


def _layer_spec(shape, lead):
    return pl.BlockSpec((None,) + tuple(shape[1:]), lambda *_: (lead,) + (0,) * (len(shape) - 1))


def _resident_slab(shape, lead, row_block=None):
    rows = shape[1] if row_block is None else row_block[1]
    rb = 0 if row_block is None else row_block[0]
    return pl.BlockSpec((None, rows) + tuple(shape[2:]), lambda *_: (lead, rb) + (0,) * (len(shape) - 2),
                        pipeline_mode=pl.Buffered(1))


def _full_spec(shape):
    nd = len(shape)
    return pl.BlockSpec(shape, lambda *_: (0,) * nd)


_Y_SPEC = pl.BlockSpec((TOKEN_TILE * SUBLANES, LANES), lambda i: (i, 0))


def _inproj_body(geom, n_x, has_y, n_nat, *refs):
    xs, refs = refs[:n_x], refs[n_x:]
    x = geom.tok_load(xs)
    if has_y:
        y, g2, refs = refs[0], refs[1], refs[2:]
        x = x + g2[...] * _tile_rows_load(y, TOKEN_TILE)
    gn, sh, sc, w, qkv_o, hgt_o, wn_s, wt_s = refs
    step = B_HEADS * HEAD_DIM
    n_t = wt_s.shape[0]

    @pl.when(pl.program_id(0) == 0)
    def _():
        wn_s[...] = w[:, :n_nat].astype(BF16)
        for r in range(0, n_t, step):
            wt_s[r:r + step, :] = w[:, n_nat + r:n_nat + r + step].T.astype(BF16)

    h = _modulate(x, gn[...], sh[...], sc[...]).astype(BF16)
    qkv_o[...] = jnp.dot(h, wn_s[...], preferred_element_type=F32)
    for r in range(0, n_t, step):
        hgt_o[r:r + step, :] = lax.dot_general(wt_s[r:r + step, :], h, _NT, preferred_element_type=F32)


def _inproj(geom, mod, layer, xa, y, gn, w_in_all, e, n_nat):
    tm = TOKEN_TILE
    d = geom.d
    n_t = w_in_all.shape[2] - n_nat
    has_y = y is not None
    in_specs, args = geom.tok_specs(xa)
    if has_y:
        in_specs += [_Y_SPEC, geom.mod_spec(layer - 1, 5)]
        args += [y, mod]
    in_specs += [_row_spec(d), geom.mod_spec(layer, 0), geom.mod_spec(layer, 1), _resident_slab(w_in_all.shape, e)]
    args += [gn, mod, mod, w_in_all]
    return pl.pallas_call(
        functools.partial(_inproj_body, geom, _n_tok(xa), has_y, n_nat),
        grid=(geom.n // tm,),
        in_specs=in_specs,
        out_specs=[pl.BlockSpec((tm, n_nat), lambda i: (i, 0)), pl.BlockSpec((n_t, tm), lambda i: (0, i))],
        out_shape=[jax.ShapeDtypeStruct((geom.n, n_nat), F32), jax.ShapeDtypeStruct((n_t, geom.n), F32)],
        scratch_shapes=[pltpu.VMEM((d, n_nat), BF16), pltpu.VMEM((n_t, d), BF16)],
        compiler_params=_params(1),
        name="ab_inproj",
    )(*args)


def _head_norm(x, gain_row, n_heads):
    outs = []
    for h in range(n_heads):
        xh = x[:, h * HEAD_DIM:(h + 1) * HEAD_DIM]
        ms = jnp.mean(xh * xh, axis=-1, keepdims=True)
        outs.append(xh * lax.rsqrt(ms + EPS))
    return jnp.concatenate(outs, axis=1) * gain_row


def _rope(x, cos, sin_signed):
    w = x.shape[1]
    lane = lax.broadcasted_iota(jnp.int32, (1, w), 1)
    first_half = (lane & 16) == 0
    partner = jnp.where(first_half, pltpu.roll(x, w - 16, 1), pltpu.roll(x, 16, 1))
    return x * cos + partner * sin_signed


def _attn_body(rope, tq, *refs):
    if rope:
        q_r, k_r, v_r, ck_r, cv_r, qg_r, kg_r, cq_r, sq_r, ckk_r, skk_r, o_r, keys_s, vals_s = refs
    else:
        q_r, k_r, v_r, qg_r, kg_r, o_r, kh_r, keys_s, vals_s = refs

    @pl.when(pl.program_id(1) == 0)
    def _():
        kn = _head_norm(k_r[...], kg_r[...], A_KV_HEADS)
        if rope:
            kn = _rope(kn, ckk_r[...], skk_r[...])
            keys = jnp.concatenate([ck_r[...], kn], axis=0)
            vals = jnp.concatenate([cv_r[...], v_r[...]], axis=0)
        else:
            kh_r[...] = kn
            keys, vals = kn, v_r[...]
        keys_s[...] = keys.astype(BF16)
        ones = jnp.ones((keys.shape[0], HEAD_DIM), BF16)
        for j in range(A_KV_HEADS):
            vals_s[j] = jnp.concatenate([vals[:, j * HEAD_DIM:(j + 1) * HEAD_DIM].astype(BF16), ones], axis=1)

    qn = _head_norm(q_r[...], qg_r[...], A_HEADS)
    if rope:
        qn = _rope(qn, cq_r[...], sq_r[...])
    qn = qn * (HEAD_DIM ** -0.5)
    keys = keys_s[...]
    outs = [None] * A_HEADS
    for j in range(A_KV_HEADS):
        kj = keys[:, j * HEAD_DIM:(j + 1) * HEAD_DIM]
        vj = vals_s[j]
        heads = [A_GROUP * j + g for g in range(A_GROUP)]
        q4 = jnp.concatenate([qn[:, h * HEAD_DIM:(h + 1) * HEAD_DIM] for h in heads], axis=0).astype(BF16)
        s = lax.dot_general(q4, kj, _NT, preferred_element_type=F32)
        m = jnp.max(s, axis=-1, keepdims=True)
        p = jnp.exp(s - m).astype(BF16)
        ol = jnp.dot(p, vj, preferred_element_type=F32)
        o = ol[:, :HEAD_DIM] / ol[:, HEAD_DIM:HEAD_DIM + 1]
        for g, h in enumerate(heads):
            outs[h] = o[g * tq:(g + 1) * tq, :]
    o_r[...] = jnp.concatenate(outs, axis=1).astype(BF16)


def _attention(qkv, row0, n_batch, t, q_gain, k_gain, cache=None, rope_tabs=None):
    tq = ATTN_Q_TILE
    nq = t // tq
    qw, kw = A_HEADS * HEAD_DIM, A_KV_HEADS * HEAD_DIM
    rope = cache is not None
    q_spec = pl.BlockSpec((tq, qw), lambda b, qi: (row0 // tq + b * nq + qi, 0))
    k_spec = pl.BlockSpec((t, kw), lambda b, qi: (row0 // t + b, qw // kw))
    v_spec = pl.BlockSpec((t, kw), lambda b, qi: (row0 // t + b, qw // kw + 1))
    gq = pl.BlockSpec((1, qw), lambda b, qi: (0, 0))
    gk = pl.BlockSpec((1, kw), lambda b, qi: (0, 0))
    out_o = pl.BlockSpec((tq, qw), lambda b, qi: (b * nq + qi, 0))
    o_shape = jax.ShapeDtypeStruct((n_batch * t, qw), BF16)
    if rope:
        ck, cv, e = cache
        past = ck.shape[2]
        c_spec = pl.BlockSpec((None, None, past, kw), lambda b, qi: (b, e, 0, 0))
        cq, sq, ckk, skk = rope_tabs
        in_specs = [q_spec, k_spec, v_spec, c_spec, c_spec, gq, gk,
                    pl.BlockSpec((tq, qw), lambda b, qi: (qi, 0)), pl.BlockSpec((tq, qw), lambda b, qi: (qi, 0)),
                    pl.BlockSpec((t, kw), lambda b, qi: (0, 0)), pl.BlockSpec((t, kw), lambda b, qi: (0, 0))]
        args = [qkv, qkv, qkv, ck, cv, q_gain, k_gain, cq, sq, ckk, skk]
        out_specs, out_shape = out_o, o_shape
    else:
        in_specs = [q_spec, k_spec, v_spec, gq, gk]
        args = [qkv, qkv, qkv, q_gain, k_gain]
        out_specs = [out_o, pl.BlockSpec((t, kw), lambda b, qi: (b, 0))]
        out_shape = [o_shape, jax.ShapeDtypeStruct((n_batch * t, kw), F32)]
        past = 0
    n_keys = past + t
    return pl.pallas_call(
        functools.partial(_attn_body, rope, tq),
        grid=(n_batch, nq),
        in_specs=in_specs,
        out_specs=out_specs,
        out_shape=out_shape,
        scratch_shapes=[pltpu.VMEM((n_keys, kw), BF16), pltpu.VMEM((A_KV_HEADS, n_keys, 2 * HEAD_DIM), BF16)],
        compiler_params=_params(2),
        name="gqa_latent" if rope else "gqa_context",
    )(*args)


def _hgrn_body(zero_init, t, *refs):
    if zero_init:
        (q_r, ff_r, fb_r, v_r, g_r, lb_r, gain_r, ob_o, send_o,
         oi_scr, u_scr, dec_scr, qt_scr, sst_scr, on_scr) = refs
        s0_r = None
    else:
        (q_r, ff_r, fb_r, v_r, g_r, lb_r, gain_r, s0_r, ob_o, send_o,
         oi_scr, u_scr, dec_scr, qt_scr, sst_scr, on_scr) = refs
    nt = t // LANES
    hd = HEAD_DIM
    pw = 2 * hd
    cpt = LANES // GLA_CHUNK
    lane = lax.broadcasted_iota(jnp.int32, (1, LANES), 1)
    pos = lane & (GLA_CHUNK - 1)
    ri = lax.broadcasted_iota(jnp.int32, (LANES, LANES), 0)
    ci = lax.broadcasted_iota(jnp.int32, (LANES, LANES), 1)
    ch_sh = GLA_CHUNK.bit_length() - 1
    hd_sh = hd.bit_length() - 1
    pw_sh = pw.bit_length() - 1
    same = (ri >> ch_sh) == (ci >> ch_sh)
    same_f = jnp.where(same, 1.0, 0.0)

    def ind(cond):
        return jnp.where(cond, 1.0, 0.0).astype(BF16)

    m_fwd = jnp.concatenate([ind(same & (ri <= ci)), ind(same & (ri > ci))], axis=1)
    m_bwd = jnp.concatenate([ind(same & (ri >= ci)), ind(same & (ri < ci))], axis=1)
    eye = ind(ri == ci)
    sel_r = lax.broadcasted_iota(jnp.int32, (cpt, LANES), 0)
    sel_c = lax.broadcasted_iota(jnp.int32, (cpt, LANES), 1)
    sel = ind((sel_c >> ch_sh) == sel_r)
    bd_r = lax.broadcasted_iota(jnp.int32, (cpt * pw, LANES), 0)
    bd_c = lax.broadcasted_iota(jnp.int32, (cpt * pw, LANES), 1)
    bd = (bd_r >> pw_sh) == (bd_c >> ch_sh)
    top = ((bd_r >> hd_sh) & 1) == 0
    bd_all = ind(bd)
    bd_top = ind(bd & top)
    bd_bot = ind(bd & jnp.logical_not(top))
    u_lane = lax.broadcasted_iota(jnp.int32, (1, cpt * pw), 1)
    u_top = ((u_lane >> hd_sh) & 1) == 0
    gain2 = jnp.concatenate([gain_r[...], gain_r[...]], axis=0)

    def colsum(a):
        return jnp.sum(a, axis=0, keepdims=True)

    def dot(a, b):
        return jnp.dot(a, b, preferred_element_type=F32)

    def pair_step(pr, carry):
        rows = pl.ds(pl.multiple_of(pr * pw, pw), pw)

        def phase_a(ti, c):
            ls = pl.ds(pl.multiple_of(ti * LANES, LANES), LANES)
            q = q_r[rows, ls]
            v = v_r[rows, ls]
            vb = v.astype(BF16)
            skews = [None, None]
            for bwd in (False, True):
                d_idx = int(bwd)
                raw_r = fb_r if bwd else ff_r
                lbv = lb_r[d_idx, rows, :]
                f = lbv + (1.0 - lbv) * jax.nn.sigmoid(raw_r[rows, ls])
                pm = (GLA_CHUNK - 1 - pos) if bwd else pos
                g_prev = [colsum(q[:hd]), colsum(q[hd:])]
                qd = q
                a_rows = [[], []]
                for dl in range(GLA_CHUNK):
                    qd = qd * (pltpu.roll(f, (LANES - dl) if bwd else dl, 1) if dl > 0 else f)
                    g_cur = [colsum(qd[:hd]), colsum(qd[hd:])]
                    for k in range(2):
                        a_rows[k].append(jnp.where(pm >= dl, g_prev[k] - g_cur[k], 0.0))
                    g_prev = g_cur
                for k in range(2):
                    order = a_rows[k] if bwd else [a_rows[k][0]] + a_rows[k][:0:-1]
                    stack = jnp.concatenate(order * cpt, axis=0).astype(BF16)
                    tall = lax.dot_general(eye, stack, _NT, preferred_element_type=F32)
                    sk = pltpu.roll(tall, 0, 1, stride=1, stride_axis=0)
                    skews[k] = sk if skews[k] is None else skews[k] + sk
                hi, mid, lo = _split3(jnp.log(f))
                m_cs = m_bwd if bwd else m_fwd
                cs = dot(hi, m_cs) + dot(mid, m_cs) + dot(lo, m_cs)
                qt_scr[d_idx, :, ls] = q * jnp.exp(cs[:, :LANES])
                kt = (1.0 - f) * jnp.exp(cs[:, LANES:])
                dsum = (lax.dot_general(sel, hi, _NT, preferred_element_type=F32)
                        + lax.dot_general(sel, mid, _NT, preferred_element_type=F32)
                        + lax.dot_general(sel, lo, _NT, preferred_element_type=F32))
                dec_scr[d_idx, ti] = jnp.exp(dsum)
                kbd = jnp.concatenate([kt.astype(BF16)] * cpt, axis=0) * bd_all
                uo = lax.dot_general(vb, kbd, _NT, preferred_element_type=F32)
                u_scr[d_idx, ti] = jnp.where(u_top, uo[:hd], uo[hd:])
            amats = [(sk * same_f).astype(BF16) for sk in skews]
            oi_scr[:, ls] = jnp.concatenate(
                [lax.dot_general(vb[:hd], amats[0], _NT, preferred_element_type=F32),
                 lax.dot_general(vb[hd:], amats[1], _NT, preferred_element_type=F32)], axis=0)
            return c

        lax.fori_loop(0, nt, phase_a, 0, unroll=min(nt, 4))

        def phase_b(ti, st):
            sf, sb = st
            u = u_scr[0, ti]
            dec = dec_scr[0, ti]
            pieces = [None] * cpt
            for c in range(cpt):
                pieces[c] = sf
                sf = dec[c:c + 1, :] * sf + u[:, c * pw:(c + 1) * pw]
            sst_scr[0, ti] = jnp.concatenate(pieces, axis=1).astype(BF16)
            tb = nt - 1 - ti
            u = u_scr[1, tb]
            dec = dec_scr[1, tb]
            for c in range(cpt - 1, -1, -1):
                pieces[c] = sb
                sb = dec[c:c + 1, :] * sb + u[:, c * pw:(c + 1) * pw]
            sst_scr[1, tb] = jnp.concatenate(pieces, axis=1).astype(BF16)
            return sf, sb

        if zero_init:
            s0 = (jnp.zeros((hd, pw), F32), jnp.zeros((hd, pw), F32))
        else:
            s0 = tuple(jnp.concatenate([s0_r[k, 2 * pr], s0_r[k, 2 * pr + 1]], axis=1) for k in range(2))
        s_end = lax.fori_loop(0, nt, phase_b, s0)
        for k in range(2):
            send_o[k, 2 * pr] = s_end[k][:, :hd]
            send_o[k, 2 * pr + 1] = s_end[k][:, hd:]

        def phase_c(ti, c):
            ls = pl.ds(pl.multiple_of(ti * LANES, LANES), LANES)
            o = oi_scr[:, ls]
            for k in range(2):
                t8 = jnp.concatenate([qt_scr[k, :, ls].astype(BF16)] * cpt, axis=0)
                qb = jnp.concatenate([t8 * bd_top, t8 * bd_bot], axis=1)
                oo = dot(sst_scr[k, ti], qb)
                o = o + jnp.concatenate([oo[:, :LANES], oo[:, LANES:]], axis=0)
            o0, o1 = o[:hd], o[hd:]
            on = jnp.concatenate([o0 * lax.rsqrt(jnp.mean(o0 * o0, axis=0, keepdims=True) + EPS),
                                  o1 * lax.rsqrt(jnp.mean(o1 * o1, axis=0, keepdims=True) + EPS)], axis=0)
            on_scr[rows, ls] = on * gain2 * _silu(g_r[rows, ls])
            return c

        lax.fori_loop(0, nt, phase_c, 0, unroll=2)
        return carry

    lax.fori_loop(0, B_HEADS // 2, pair_step, 0)
    ob_o[...] = on_scr[...].T.astype(BF16)


def _hgrn(hgt, col0, n_batch, t, lb_b, gain_b, s0t):
    bw = B_HEADS * HEAD_DIM
    hd, pw = HEAD_DIM, 2 * HEAD_DIM
    nt = t // LANES
    cpt = LANES // GLA_CHUNK
    zero_init = s0t is None
    specs = [pl.BlockSpec((bw, t), functools.partial(lambda r, b: (r, col0 // t + b), r)) for r in range(5)]
    in_specs = specs + [_full_spec(lb_b.shape), _full_spec(gain_b.shape)]
    args = [hgt] * 5 + [lb_b, gain_b]
    st_spec = pl.BlockSpec((None, 2, B_HEADS, HEAD_DIM, HEAD_DIM), lambda b: (b, 0, 0, 0, 0))
    if not zero_init:
        in_specs.append(st_spec)
        args.append(s0t)
    scratch = [pltpu.VMEM((pw, t), F32), pltpu.VMEM((2, nt, hd, cpt * pw), F32), pltpu.VMEM((2, nt, cpt, pw), F32),
               pltpu.VMEM((2, pw, t), F32), pltpu.VMEM((2, nt, hd, cpt * pw), BF16), pltpu.VMEM((bw, t), F32)]
    return pl.pallas_call(
        functools.partial(_hgrn_body, zero_init, t),
        grid=(n_batch,),
        in_specs=in_specs,
        out_specs=[pl.BlockSpec((t, bw), lambda b: (b, 0)), st_spec],
        out_shape=[jax.ShapeDtypeStruct((n_batch * t, bw), BF16),
                   jax.ShapeDtypeStruct((n_batch, 2, B_HEADS, HEAD_DIM, HEAD_DIM), F32)],
        scratch_shapes=scratch,
        compiler_params=_params(1),
        name="hgrn2_context" if zero_init else "hgrn2_latent",
    )(*args)


def _route(logits):
    lane_i = lax.broadcasted_iota(jnp.int32, logits.shape, 1)
    lane = lane_i.astype(F32)
    neg = -jnp.inf
    big = 1e9
    gl = jnp.where(lane_i < MOE_GROUPS, logits, neg)
    gmax = jnp.max(gl, axis=-1, keepdims=True)
    g_val = 1.0 / jnp.sum(jnp.exp(gl - gmax), axis=-1, keepdims=True)
    g_idx = jnp.min(jnp.where(gl == gmax, lane, big), axis=-1, keepdims=True)
    n_e = MOE_GROUPS * EXPERTS_PER_GROUP
    lane_group = ((lane_i - MOE_GROUPS) >> 2).astype(F32)
    in_group = (lane_i >= MOE_GROUPS) & (lane_i < MOE_GROUPS + n_e) & (lane_group == g_idx)
    el = jnp.where(in_group, logits, neg)
    m1 = jnp.max(el, axis=-1, keepdims=True)
    i1 = jnp.min(jnp.where(el == m1, lane, big), axis=-1, keepdims=True)
    el2 = jnp.where(lane == i1, neg, el)
    m2 = jnp.max(el2, axis=-1, keepdims=True)
    i2 = jnp.min(jnp.where(el2 == m2, lane, big), axis=-1, keepdims=True)
    tt = jnp.exp(m2 - m1)
    w1 = g_val / (1.0 + tt)
    w2 = g_val * tt / (1.0 + tt)
    base = MOE_GROUPS + EXPERTS_PER_GROUP * g_idx
    route = jnp.where(lane == i1 - base, w1, 0.0) + jnp.where(lane == i2 - base, w2, 0.0)
    return jnp.where(lane_i == ROUTE_GID_LANE, g_idx, route)


def _proj_res_body(geom, lhs_n, n_x, has_y, *refs):
    lhs = []
    for n_a in lhs_n:
        lhs.append(geom.tok_load(refs[:n_a]))
        refs = refs[n_a:]
    ws, refs = refs[:len(lhs_n)], refs[len(lhs_n):]
    xs, refs = refs[:n_x], refs[n_x:]
    x = geom.tok_load(xs)
    if has_y:
        y, g2, refs = refs[0], refs[1], refs[2:]
        x = x + g2[...] * _tile_rows_load(y, TOKEN_TILE)
    g1, gn2, sh2, sc2, wrh, wrl, br, xnew_o, xr_o, route_o = refs[:10]
    wbs = refs[10:]

    @pl.when(pl.program_id(0) == 0)
    def _():
        for w, wb in zip(ws, wbs):
            wb[...] = w[...].astype(BF16)

    acc = jnp.dot(lhs[0], wbs[0][...], preferred_element_type=F32)
    for a, wb in zip(lhs[1:], wbs[1:]):
        acc = acc + jnp.dot(a, wb[...], preferred_element_type=F32)
    xn = x + g1[...] * acc
    xnew_o[...] = xn
    hm = _modulate(xn, gn2[...], sh2[...], sc2[...])
    hi = hm.astype(BF16)
    lo = (hm - hi.astype(F32)).astype(BF16)
    logits = (jnp.dot(hi, wrh[...], preferred_element_type=F32) + jnp.dot(lo, wrh[...], preferred_element_type=F32)
              + jnp.dot(hi, wrl[...], preferred_element_type=F32)) + br[...]
    route = _route(logits)
    for s, piece in enumerate(_pack_routed(hm, route)):
        xr_o[pl.ds(s, TOKEN_TILE, stride=ROUTED_SUB), :] = piece
    route_o[...] = route


def _proj_res(geom, mod, layer, lhs, ws, xa, y, gn2, wrh, wrl, br, name):
    tm = TOKEN_TILE
    d = geom.d
    has_y = y is not None
    tok = pl.BlockSpec((tm, d), lambda i: (i, 0))
    in_specs, args = [], []
    for a in lhs:
        s, ar = geom.tok_specs(a)
        in_specs += s
        args += ar
    w_shapes = []
    for w_all, lead, row_block in ws:
        in_specs.append(_resident_slab(w_all.shape, lead, row_block))
        args.append(w_all)
        w_shapes.append((w_all.shape[1] if row_block is None else row_block[1],) + tuple(w_all.shape[2:]))
    s, ar = geom.tok_specs(xa)
    in_specs += s
    args += ar
    if has_y:
        in_specs += [_Y_SPEC, geom.mod_spec(layer - 1, 5)]
        args += [y, mod]
    in_specs += [geom.mod_spec(layer, 2), _layer_spec(gn2.shape, layer), geom.mod_spec(layer, 3),
                 geom.mod_spec(layer, 4), _layer_spec(wrh.shape, layer), _layer_spec(wrl.shape, layer),
                 _layer_spec(br.shape, layer)]
    args += [mod, gn2, mod, mod, wrh, wrl, br]
    return pl.pallas_call(
        functools.partial(_proj_res_body, geom, tuple(_n_tok(a) for a in lhs), _n_tok(xa), has_y),
        grid=(geom.n // tm,),
        in_specs=in_specs,
        out_specs=[tok, pl.BlockSpec((tm * ROUTED_SUB, LANES), lambda i: (i, 0)),
                   pl.BlockSpec((tm, ROUTE_W), lambda i: (i, 0))],
        out_shape=[jax.ShapeDtypeStruct((geom.n, d), F32), jax.ShapeDtypeStruct((geom.n * ROUTED_SUB, LANES), F32),
                   jax.ShapeDtypeStruct((geom.n, ROUTE_W), F32)],
        scratch_shapes=[pltpu.VMEM(shp, BF16) for shp in w_shapes],
        compiler_params=_params(1),
        name=name,
    )(*args)


def _dispatch(route, n, tile):
    n_tiles = n // tile + MOE_GROUPS
    n_slots = n_tiles * tile
    gid = route[:, ROUTE_GID_LANE].astype(jnp.int32)
    onehot = (gid[:, None] == jnp.arange(MOE_GROUPS, dtype=jnp.int32)[None, :]).astype(jnp.int32)
    csum = jnp.cumsum(onehot, axis=0)
    counts = csum[-1]
    rank = jnp.sum(csum * onehot, axis=1) - 1
    tiles_g = (counts + tile - 1) // tile
    tile_end = jnp.cumsum(tiles_g)
    tile_start = tile_end - tiles_g
    dest = (tile_start * tile)[gid] + rank
    tok = jnp.arange(n, dtype=jnp.int32)
    src = jnp.zeros((n_slots,), jnp.int32).at[dest].set(tok)
    t_idx = jnp.arange(n_tiles, dtype=jnp.int32)
    tile_gid = jnp.minimum(jnp.sum((t_idx[:, None] >= tile_end[None, :]).astype(jnp.int32), axis=1), MOE_GROUPS - 1)
    tile_rows = jnp.clip(counts[tile_gid] - (t_idx - tile_start[tile_gid]) * tile, 0, tile)
    tile_rows = jnp.where(t_idx < tile_end[-1], tile_rows, 0)
    in_tile = jnp.arange(tile, dtype=jnp.int32)
    valid = (in_tile[None, :] < tile_rows[:, None]).reshape(n_slots)
    spare = (n + (t_idx[:, None] & 1) * tile + in_tile[None, :]).reshape(n_slots)
    dst = jnp.concatenate([n + tile + in_tile, jnp.where(valid, src, spare)])
    return tile_gid, src, dst, tile_end[-1:].astype(jnp.int32), n_tiles


def _moe_body(tile, d, n_rows, tg_ref, src_ref, dst_ref, nu_ref, xr_hbm, w1_ref, w3_ref, w2_ref, y_hbm,
              xbuf, ybuf, w1b, w3b, w2b, isem, osem):
    t = pl.program_id(0)
    last = nu_ref[0] - 1
    slot = t % 2
    other = 1 - slot

    sub = SUBLANES
    rsub = ROUTED_SUB

    def hbm_rows(ref, row, n_sub):
        start = row * n_sub if isinstance(row, int) else pl.multiple_of(row * n_sub, n_sub)
        return ref.at[pl.ds(start, n_sub), :]

    def row_in(tile_idx, s, i):
        row = src_ref[tile_idx * tile + i]
        return pltpu.make_async_copy(hbm_rows(xr_hbm, row, rsub), xbuf.at[s, pl.ds(i * rsub, rsub), :], isem.at[s])

    def row_out(ext_tile_idx, s, i):
        row = dst_ref[ext_tile_idx * tile + i]
        return pltpu.make_async_copy(ybuf.at[s, pl.ds(i * sub, sub), :], hbm_rows(y_hbm, row, sub), osem.at[s])

    @pl.when(t == 0)
    def _():
        ybuf[...] = jnp.zeros(ybuf.shape, F32)
        for i in range(tile):
            row_in(0, 0, i).start(priority=i % 2)
        for i in range(tile):
            pltpu.make_async_copy(ybuf.at[0, pl.ds(i * sub, sub), :], hbm_rows(y_hbm, n_rows + i, sub),
                                  osem.at[0]).start(priority=i % 2)

    g = tg_ref[t]
    g_prev = tg_ref[jnp.maximum(t - 1, 0)]

    @pl.when(((t == 0) | (g != g_prev)) & (t <= last))
    def _():
        w1b[...] = w1_ref[...].astype(BF16)
        w3b[...] = w3_ref[...].astype(BF16)
        w2b[...] = w2_ref[...].astype(BF16)

    nxt = jnp.minimum(t + 1, last)
    n_parts = 4 * EXPERTS_PER_GROUP
    per = tile // n_parts

    def step(slot, other):
        for i in range(tile):
            row_in(0, slot, i).wait()
        part = [0]

        def issue_part():
            k = part[0]
            part[0] += 1
            for i in range(k * per, (k + 1) * per):
                row_in(nxt, other, i).start(priority=i % 2)
                row_out(t, other, i).start(priority=i % 2)

        xb, cw = _unpack_routed(xbuf.at[slot], tile)
        acc = jnp.zeros((tile, d), F32)
        for e in range(EXPERTS_PER_GROUP):
            a = jnp.dot(xb, w1b[e], preferred_element_type=F32)
            issue_part()
            b = jnp.dot(xb, w3b[e], preferred_element_type=F32)
            issue_part()
            hid = (_silu(a) * b * cw[:, e:e + 1]).astype(BF16)
            issue_part()
            acc = acc + jnp.dot(hid, w2b[e], preferred_element_type=F32)
            issue_part()
        for i in range(tile):
            row_out(0, slot, i).wait()
        _tile_rows_store(ybuf.at[slot], acc)

        @pl.when(t == last)
        def _():
            for i in range(tile):
                row_in(0, other, i).wait()
            for i in range(tile):
                row_out(0, other, i).wait()
            for i in range(tile):
                row_out(t + 1, slot, i).start(priority=i % 2)
            for i in range(tile):
                row_out(0, slot, i).wait()

    @pl.when((slot == 0) & (t <= last))
    def _():
        step(0, 1)

    @pl.when((slot == 1) & (t <= last))
    def _():
        step(1, 0)


def _moe(xr, route, w1, w3, w2, layer, n, d):
    tile = MOE_TILE
    assert d == SUBLANES * LANES
    tile_gid, src, dst, n_used, n_tiles = _dispatch(route, n, tile)
    g0 = layer * MOE_GROUPS
    wspec13 = pl.BlockSpec((None,) + w1.shape[1:], lambda t, tg, s, ds_, nu: (g0 + tg[t], 0, 0, 0))
    wspec2 = pl.BlockSpec((None,) + w2.shape[1:], lambda t, tg, s, ds_, nu: (g0 + tg[t], 0, 0, 0))
    grid_spec = pltpu.PrefetchScalarGridSpec(
        num_scalar_prefetch=4,
        grid=(n_tiles,),
        in_specs=[pl.BlockSpec(memory_space=pl.ANY), wspec13, wspec13, wspec2],
        out_specs=pl.BlockSpec(memory_space=pl.ANY),
        scratch_shapes=[pltpu.VMEM((2, tile * ROUTED_SUB, LANES), F32), pltpu.VMEM((2, tile * SUBLANES, LANES), F32),
                        pltpu.VMEM(w1.shape[1:], BF16), pltpu.VMEM(w3.shape[1:], BF16), pltpu.VMEM(w2.shape[1:], BF16),
                        pltpu.SemaphoreType.DMA((2,)), pltpu.SemaphoreType.DMA((2,))],
    )
    return pl.pallas_call(
        functools.partial(_moe_body, tile, d, n),
        grid_spec=grid_spec,
        out_shape=jax.ShapeDtypeStruct(((n + 2 * tile) * SUBLANES, LANES), F32),
        compiler_params=_params(1),
        name="moe_ffn",
    )(tile_gid, src, dst, n_used, xr, w1, w3, w2)


def _glu_body(xa, y, g2, gn, sh, sc, w, u_o, wb):
    @pl.when(pl.program_id(0) == 0)
    def _():
        wb[...] = w[...].astype(BF16)

    x = xa[...] + g2[...] * _tile_rows_load(y, TOKEN_TILE)
    h = _modulate(x, gn[...], sh[...], sc[...]).astype(BF16)
    ab = jnp.dot(h, wb[...], preferred_element_type=F32)
    c = ab.shape[1] // 2
    u_o[...] = ab[:, :c] * jax.nn.sigmoid(ab[:, c:])


def _glu(geom, mod, layer, xa, y, gn, w_all, o):
    tm = TOKEN_TILE
    d = geom.d
    tok = pl.BlockSpec((tm, d), lambda i: (i, 0))
    c = w_all.shape[2] // 2
    return pl.pallas_call(
        _glu_body,
        grid=(geom.n // tm,),
        in_specs=[tok, _Y_SPEC, geom.mod_spec(layer - 1, 5), _row_spec(d), geom.mod_spec(layer, 0),
                  geom.mod_spec(layer, 1), _resident_slab(w_all.shape, o)],
        out_specs=pl.BlockSpec((tm, c), lambda i: (i, 0)),
        out_shape=jax.ShapeDtypeStruct((geom.n, c), F32),
        scratch_shapes=[pltpu.VMEM(w_all.shape[1:], BF16)],
        compiler_params=_params(1),
        name="conv_glu",
    )(xa, y, mod, gn, mod, mod, w_all)


def _dwconv_body(t, u_r, dw_r, b_r, g_r, be_r, o_r, pad_scr, cv_scr, sh_scr):
    c = u_r.shape[1]
    halo = 2 * SUBLANES
    sh_len = sh_scr.shape[1]
    pad_scr[0:halo, :] = jnp.zeros((halo, c), F32)
    pad_scr[t + halo:t + 2 * halo, :] = jnp.zeros((halo, c), F32)
    pad_scr[halo:t + halo, :] = u_r[...]
    rb_n = CONV_ROWS
    n_lb = c // LANES
    first = halo - CONV_WIDTH // 2

    def blk(lb, carry):
        ls = pl.ds(pl.multiple_of(lb * LANES, LANES), LANES)
        bias = b_r[:, ls]
        for r0 in range(0, t, rb_n):
            win = pad_scr[r0:r0 + rb_n + 2 * halo, ls]
            for sft in range(1, SUBLANES):
                sh_scr[sft] = pltpu.roll(win, win.shape[0] - sft, 0)[:sh_len, :]
            acc = jnp.zeros((rb_n, LANES), F32)
            for k in range(CONV_WIDTH):
                sft = (first + k) % SUBLANES
                a = first + k - sft
                src = pad_scr[r0 + a:r0 + a + rb_n, ls] if sft == 0 else sh_scr[sft, a:a + rb_n, :]
                acc = acc + src * dw_r[k:k + 1, ls]
            cv_scr[r0:r0 + rb_n, ls] = acc + bias
        return carry

    lax.fori_loop(0, n_lb, blk, 0)

    def ln(rb, carry):
        rs = pl.ds(pl.multiple_of(rb * rb_n, rb_n), rb_n)
        x = cv_scr[rs, :]
        mu = jnp.mean(x, axis=-1, keepdims=True)
        xc = x - mu
        var = jnp.mean(xc * xc, axis=-1, keepdims=True)
        yv = xc * lax.rsqrt(var + EPS) * g_r[...] + be_r[...]
        o_r[rs, :] = _silu(yv).astype(BF16)
        return carry

    lax.fori_loop(0, t // rb_n, ln, 0, unroll=2)


def _dwconv(u, row0, n_batch, t, dw, b, g, be):
    c = u.shape[1]
    return pl.pallas_call(
        functools.partial(_dwconv_body, t),
        grid=(n_batch,),
        in_specs=[pl.BlockSpec((t, c), lambda bi: (row0 // t + bi, 0)), _full_spec(dw.shape),
                  _row_spec(c), _row_spec(c), _row_spec(c)],
        out_specs=pl.BlockSpec((t, c), lambda bi: (bi, 0)),
        out_shape=jax.ShapeDtypeStruct((n_batch * t, c), BF16),
        scratch_shapes=[pltpu.VMEM((t + 4 * SUBLANES, c), F32), pltpu.VMEM((t, c), F32),
                        pltpu.VMEM((SUBLANES, CONV_ROWS + (CONV_WIDTH // SUBLANES) * SUBLANES, LANES), F32)],
        compiler_params=_params(1),
        name="conv_dw",
    )(u, dw, b, g, be)


def _final_body(ctx_tiles, xa, y, g2, gf, oc, os_):
    x = xa[...] + g2[...] * _tile_rows_load(y, TOKEN_TILE)
    ms = jnp.mean(x * x, axis=-1, keepdims=True)
    val = x * lax.rsqrt(ms + EPS) * gf[...]
    i = pl.program_id(0)

    @pl.when(i < ctx_tiles)
    def _():
        oc[...] = val

    @pl.when(i >= ctx_tiles)
    def _():
        os_[...] = val


def _final(geom, mod, depth, xa, y, gf):
    tm = TOKEN_TILE
    d = geom.d
    ct = geom.ctx_tiles
    tok = pl.BlockSpec((tm, d), lambda i: (i, 0))
    return pl.pallas_call(
        functools.partial(_final_body, ct),
        grid=(geom.n // tm,),
        in_specs=[tok, _Y_SPEC, geom.mod_spec(depth - 1, 5), _row_spec(d)],
        out_specs=[pl.BlockSpec((tm, d), lambda i: (jnp.minimum(i, ct - 1), 0)),
                   pl.BlockSpec((tm, d), lambda i: (jnp.maximum(i - ct, 0), 0))],
        out_shape=[jax.ShapeDtypeStruct((geom.n_ctx, d), F32), jax.ShapeDtypeStruct((geom.n_smp, d), F32)],
        compiler_params=_params(1),
        name="final_norm",
    )(xa, y, mod, gf)


def _rope_tables(n_tokens):
    rows = n_tokens // GRID_W
    row = jnp.repeat(jnp.arange(rows, dtype=F32), GRID_W)
    col = jnp.tile(jnp.arange(GRID_W, dtype=F32), rows)
    half = HEAD_DIM // 2
    inv = ROPE_THETA ** (-jnp.arange(0, half, 2, dtype=F32) / half)
    ang_r = row[:, None] * inv[None, :]
    ang_c = col[:, None] * inv[None, :]
    cr, sr, cc, sc = jnp.cos(ang_r), jnp.sin(ang_r), jnp.cos(ang_c), jnp.sin(ang_c)
    cos = jnp.concatenate([cr, cr, cc, cc], axis=1)
    sin = jnp.concatenate([-sr, sr, -sc, sc], axis=1)
    return (jnp.tile(cos, (1, A_HEADS)), jnp.tile(sin, (1, A_HEADS)),
            jnp.tile(cos, (1, A_KV_HEADS)), jnp.tile(sin, (1, A_KV_HEADS)))


def _router_weights(w_group, b_group, w_expert, b_expert):
    depth, d, _ = w_group.shape
    n_used = MOE_GROUPS + MOE_GROUPS * EXPERTS_PER_GROUP
    w = jnp.concatenate([w_group, w_expert, jnp.zeros((depth, d, ROUTE_W - n_used), F32)], axis=2)
    b = jnp.concatenate([b_group, b_expert, jnp.zeros((depth, ROUTE_W - n_used), F32)], axis=1)[:, None, :]
    hi = w.astype(BF16)
    lo = (w - hi.astype(F32)).astype(BF16)
    return hi, lo, b


def kernel(x_prompt, x_sample, cache_k, cache_v, state_hgrn, c, c_ctx, ada_w, ada_b, norm1_g, norm2_g, mix_w_in, mix_w_out, attn_q_gain, attn_k_gain, hgrn_lower_bound, hgrn_out_gain, conv_pw1, conv_dw, conv_dw_b, conv_ln_g, conv_ln_b, conv_pw2, moe_w_group, moe_b_group, moe_w_expert, moe_b_expert, moe_w1, moe_w3, moe_w2, final_g):
    batch, seq, d = x_prompt.shape
    dec_batch, dec_seq, _ = x_sample.shape
    depth = ada_w.shape[0]
    past = cache_k.shape[2]
    n_ctx, n_smp = batch * seq, dec_batch * dec_seq
    n = n_ctx + n_smp
    geom = _Geom(n_ctx, n_smp, dec_seq, d)
    qw, kw, bw = A_HEADS * HEAD_DIM, A_KV_HEADS * HEAD_DIM, B_HEADS * HEAD_DIM
    n_nat = qw + 2 * kw

    cond8 = jnp.concatenate([c_ctx[None, :], c, jnp.zeros((8 - 1 - dec_batch, d), F32)], axis=0)
    mod = _ada(cond8, ada_w, ada_b).reshape(depth, 8, 6, 1, d)

    lb_soft = jax.nn.softmax(hgrn_lower_bound.astype(F32), axis=0)
    lower_bounds = jnp.cumsum(lb_soft, axis=0) - lb_soft[0]
    rope_tabs = _rope_tables(dec_seq)

    xa = (x_prompt.reshape(n_ctx, d), x_sample.reshape(n_smp, d))
    y = None
    f = moe_w1.shape[-1]
    w1 = moe_w1.reshape(depth * MOE_GROUPS, EXPERTS_PER_GROUP, d, f)
    w3 = moe_w3.reshape(depth * MOE_GROUPS, EXPERTS_PER_GROUP, d, f)
    w2 = moe_w2.reshape(depth * MOE_GROUPS, EXPERTS_PER_GROUP, f, d)
    n_ab = cache_k.shape[1]
    ck = cache_k.reshape(dec_batch, n_ab, past, kw)
    cv = cache_v.reshape(dec_batch, n_ab, past, kw)
    wrh, wrl, br = _router_weights(moe_w_group, moe_b_group, moe_w_expert, moe_b_expert)
    gn2 = norm2_g[:, None, :]
    new_k, new_v, new_s = [], [], []
    for l in range(depth):
        gn1 = norm1_g[l][None, :]
        if l % 2 == 0:
            e = l // 2
            qkv, hgt = _inproj(geom, mod, l, xa, y, gn1, mix_w_in, e, n_nat)
            qg = jnp.tile(attn_q_gain[e], A_HEADS)[None, :]
            kg = jnp.tile(attn_k_gain[e], A_KV_HEADS)[None, :]
            oa_c, khat = _attention(qkv, 0, batch, seq, qg, kg)
            oa_s = _attention(qkv, n_ctx, dec_batch, dec_seq, qg, kg, cache=(ck, cv, e), rope_tabs=rope_tabs)
            lb_b = jnp.broadcast_to(lower_bounds[e][:, :, None], (2, bw, LANES))
            gain_b = jnp.broadcast_to(hgrn_out_gain[e][:, None], (HEAD_DIM, LANES))
            ob_c, s_ctx = _hgrn(hgt, 0, batch, seq, lb_b, gain_b, None)
            s0t = jnp.swapaxes(state_hgrn[:, e], -1, -2)
            ob_s, _ = _hgrn(hgt, n_ctx, dec_batch, dec_seq, lb_b, gain_b, s0t)
            w_halves = [(mix_w_out, e, (0, qw)), (mix_w_out, e, (1, bw))]
            xnew, xr, route = _proj_res(geom, mod, l, [(oa_c, oa_s), (ob_c, ob_s)], w_halves, xa, y,
                                        gn2, wrh, wrl, br, "ab_outproj")
            new_k.append(khat.reshape(batch, seq, A_KV_HEADS, HEAD_DIM))
            new_v.append(qkv[:n_ctx, qw + kw:].reshape(batch, seq, A_KV_HEADS, HEAD_DIM))
            new_s.append(jnp.swapaxes(s_ctx, -1, -2))
        else:
            o = l // 2
            u = _glu(geom, mod, l, xa, y, gn1, conv_pw1, o)
            cargs = (conv_dw[o], conv_dw_b[o][None, :], conv_ln_g[o][None, :], conv_ln_b[o][None, :])
            c_c = _dwconv(u, 0, batch, seq, *cargs)
            c_s = _dwconv(u, n_ctx, dec_batch, dec_seq, *cargs)
            xnew, xr, route = _proj_res(geom, mod, l, [(c_c, c_s)], [(conv_pw2, o, None)], xa, y, gn2, wrh, wrl, br,
                                        "conv_outproj")
        y = _moe(xr, route, w1, w3, w2, l, n, d)
        xa = xnew
    y_prompt, y_sample = _final(geom, mod, depth, xa, y, final_g[None, :])
    return (y_prompt.reshape(batch, seq, d), y_sample.reshape(dec_batch, dec_seq, d),
            jnp.stack(new_k, axis=1), jnp.stack(new_v, axis=1), jnp.stack(new_s, axis=1))
```
